```python
import math
import jax, jax.numpy as jnp
from jax import lax
import numpy as np

D_MODEL = 1024
BATCH = 4
SEQ = 8192
DEPTH = 4

N_MIXERS = 3
D_FF = 2816
N_SUB = 3
EPS = 1e-6
A_HEADS = 8
A_DK = D_MODEL // A_HEADS
A_DV = D_MODEL // A_HEADS
A_CHUNK = 64
B_HEADS = 8
B_DH = D_MODEL // (2 * B_HEADS)
B_ROT = B_DH // 4
ROPE_THETA = 500000.0
Q_BLOCK = 128
C_WIDTH = 3
N_A = (DEPTH + 2) // N_MIXERS
N_B = (DEPTH + 1) // N_MIXERS
N_C = DEPTH // N_MIXERS

kernel_name = "hybrid_hgrn2_diffattn_shortconv_macaron"


def rmsnorm(x, g):
    xf = x.astype(jnp.float32)
    y = xf * lax.rsqrt(jnp.mean(xf * xf, axis=-1, keepdims=True) + EPS)
    return (y * g.astype(jnp.float32)).astype(x.dtype)


def swiglu(h, wi, wo):
    gt, up = jnp.split(h @ wi, 2, axis=-1)
    return (jax.nn.silu(gt) * up) @ wo


def partial_rope(x, cos, sin):
    xr, xp = x[..., :B_ROT], x[..., B_ROT:]
    x1 = xr[..., :B_ROT // 2].astype(jnp.float32)
    x2 = xr[..., B_ROT // 2:].astype(jnp.float32)
    rot = jnp.concatenate([x1 * cos - x2 * sin, x2 * cos + x1 * sin], axis=-1)
    return jnp.concatenate([rot.astype(x.dtype), xp], axis=-1)


def hgrn2_mixer(h, w_in, w_out, lb, onorm_g):
    Bn, S, _ = h.shape
    nc = S // A_CHUNK
    q, fz, i, g = jnp.split(h @ w_in, 4, axis=-1)
    q = jax.nn.silu(q).astype(jnp.float32)
    f = lb.astype(jnp.float32) + (1.0 - lb.astype(jnp.float32)) * jax.nn.sigmoid(fz.astype(jnp.float32))
    logf = jnp.log(f)
    k = 1.0 - f

    def heads(t, dh):
        t = t.astype(jnp.float32).reshape(Bn, nc, A_CHUNK, A_HEADS, dh)
        return t.transpose(1, 0, 3, 2, 4)

    causal = jnp.tril(jnp.ones((A_CHUNK, A_CHUNK), dtype=bool))

    def step(state, inp):
        q_c, k_c, v_c, lf_c = inp
        b = jnp.cumsum(lf_c, axis=2)
        inter = jnp.einsum('bhtk,bhkv->bhtv', q_c * jnp.exp(b), state)
        diff = b[:, :, :, None, :] - b[:, :, None, :, :]
        decay = jnp.exp(jnp.where(causal[:, :, None], diff, -jnp.inf))
        att = jnp.einsum('bhtk,bhsk,bhtsk->bhts', q_c, k_c, decay)
        intra = jnp.einsum('bhts,bhsv->bhtv', att, v_c)
        b_last = b[:, :, -1]
        new_state = jnp.exp(b_last)[..., None] * state + jnp.einsum(
            'bhsk,bhsv->bhkv', k_c * jnp.exp(b_last[:, :, None] - b), v_c)
        return new_state, inter + intra

    s0 = jnp.zeros((Bn, A_HEADS, A_DK, A_DV), jnp.float32)
    _, o = lax.scan(step, s0, (heads(q, A_DK), heads(k, A_DK), heads(i, A_DV), heads(logf, A_DK)))
    o = o.transpose(1, 0, 3, 2, 4).reshape(Bn, S, A_HEADS, A_DV)
    o = rmsnorm(o, onorm_g) * jax.nn.silu(g.astype(jnp.float32).reshape(Bn, S, A_HEADS, A_DV))
    return o.reshape(Bn, S, D_MODEL).astype(h.dtype) @ w_out


def diff_attention_mixer(h, w_in, w_out, qk_g, lam_p, subln_g, cos, sin, lambda_init):
    Bn, S, _ = h.shape
    q, k, v = jnp.split(h @ w_in, 3, axis=-1)
    q = q.reshape(Bn, S, B_HEADS, 2, B_DH)
    k = k.reshape(Bn, S, B_HEADS, 2, B_DH)
    v = v.reshape(Bn, S, B_HEADS, 2 * B_DH).transpose(0, 2, 1, 3)
    q = partial_rope(rmsnorm(q, qk_g[0]), cos, sin).transpose(0, 2, 3, 1, 4)
    k = partial_rope(rmsnorm(k, qk_g[1]), cos, sin).transpose(0, 2, 3, 1, 4)
    lp = lam_p.astype(jnp.float32)
    lam = jnp.exp(jnp.sum(lp[0] * lp[1])) - jnp.exp(jnp.sum(lp[2] * lp[3])) + lambda_init
    scale = B_DH ** -0.5
    outs = []
    for blk in range(S // Q_BLOCK):
        s0 = blk * Q_BLOCK
        s1 = s0 + Q_BLOCK
        sc = jnp.einsum('bhcqd,bhckd->bhcqk', q[:, :, :, s0:s1], k[:, :, :, :s1]).astype(jnp.float32) * scale
        mask = (s0 + jnp.arange(Q_BLOCK))[:, None] >= jnp.arange(s1)[None, :]
        p = jax.nn.softmax(jnp.where(mask, sc, -jnp.inf), axis=-1)
        a = p[:, :, 0] - lam * p[:, :, 1]
        outs.append(jnp.einsum('bhqk,bhkv->bhqv', a.astype(v.dtype), v[:, :, :s1]))
    o = jnp.concatenate(outs, axis=2)
    o = rmsnorm(o, subln_g) * (1.0 - lambda_init)
    return o.transpose(0, 2, 1, 3).reshape(Bn, S, D_MODEL) @ w_out


def short_conv_mixer(h, w_in, conv_w, w_out):
    bg, cg, u = jnp.split(h @ w_in, 3, axis=-1)
    u = cg * u
    up = jnp.pad(u, ((0, 0), (C_WIDTH - 1, 0), (0, 0)))
    y = conv_w[0] * up[:, :-2] + conv_w[1] * up[:, 1:-1] + conv_w[2] * up[:, 2:]
    return (bg * y) @ w_out


def setup_inputs(seed: int = 0) -> dict:
    key = jax.random.key(seed)
    ks = jax.random.split(key, 24)
    D, F = D_MODEL, D_FF
    sD, sF = D ** -0.5, F ** -0.5
    n = jax.random.normal
    return {
        "x": n(ks[0], (BATCH, SEQ, D), jnp.float32),
        "c": n(ks[1], (BATCH, D), jnp.float32),
        "positions": (jnp.arange(SEQ, dtype=jnp.int32)[None, :]
                      + jax.random.randint(ks[2], (BATCH, 1), 0, 4096, dtype=jnp.int32)),
        "ada_w": n(ks[3], (DEPTH, D, 3 * N_SUB * D), jnp.float32) * (0.2 * sD),
        "ada_b": n(ks[4], (DEPTH, 3 * N_SUB * D), jnp.float32) * 0.02,
        "norm_g": 1.0 + 0.02 * n(ks[5], (DEPTH, N_SUB, D), jnp.float32),
        "ffn_wi": n(ks[6], (DEPTH, 2, D, 2 * F), jnp.float32) * sD,
        "ffn_wo": n(ks[7], (DEPTH, 2, F, D), jnp.float32) * sF,
        "a_w_in": n(ks[8], (N_A, D, 4 * D), jnp.float32) * sD,
        "a_w_out": n(ks[9], (N_A, D, D), jnp.float32) * sD,
        "a_lb": 0.5 * n(ks[10], (N_A, D), jnp.float32),
        "a_onorm": 1.0 + 0.02 * n(ks[11], (N_A, A_DV), jnp.float32),
        "b_w_in": n(ks[12], (N_B, D, 3 * D), jnp.float32) * sD,
        "b_w_out": n(ks[13], (N_B, D, D), jnp.float32) * sD,
        "b_qk_g": 1.0 + 0.02 * n(ks[14], (N_B, 2, B_DH), jnp.float32),
        "b_lam": 0.1 * n(ks[15], (N_B, 4, B_DH), jnp.float32),
        "b_subln": 1.0 + 0.02 * n(ks[16], (N_B, 2 * B_DH), jnp.float32),
        "c_w_in": n(ks[17], (N_C, D, 3 * D), jnp.float32) * sD,
        "c_conv": n(ks[18], (N_C, C_WIDTH, D), jnp.float32) * (C_WIDTH ** -0.5),
        "c_w_out": n(ks[19], (N_C, D, D), jnp.float32) * sD,
    }


def reference(x, c, positions, ada_w, ada_b, norm_g, ffn_wi, ffn_wo,
              a_w_in, a_w_out, a_lb, a_onorm,
              b_w_in, b_w_out, b_qk_g, b_lam, b_subln,
              c_w_in, c_conv, c_w_out):
    Bn, S, D = x.shape
    inv_freq = ROPE_THETA ** (-jnp.arange(0, B_ROT, 2, dtype=jnp.float32) / B_ROT)
    ang = positions.astype(jnp.float32)[..., None] * inv_freq
    cos = jnp.cos(ang)[:, :, None, None, :]
    sin = jnp.sin(ang)[:, :, None, None, :]
    lb_sm = jax.nn.softmax(a_lb.astype(jnp.float32), axis=0)
    lb_all = jnp.cumsum(lb_sm, axis=0) - lb_sm[0]
    c_act = jax.nn.silu(c)

    for l in range(DEPTH):
        mod = (c_act @ ada_w[l] + ada_b[l]).reshape(Bn, N_SUB, 3, 1, D)

        def pre(h, j):
            return rmsnorm(h, norm_g[l, j]) * (1.0 + mod[:, j, 1]) + mod[:, j, 0]

        x = x + 0.5 * (1.0 + mod[:, 0, 2]) * swiglu(pre(x, 0), ffn_wi[l, 0], ffn_wo[l, 0])
        h = pre(x, 1)
        kind, idx = l % N_MIXERS, l // N_MIXERS
        if kind == 0:
            y = hgrn2_mixer(h, a_w_in[idx], a_w_out[idx], lb_all[idx], a_onorm[idx])
        elif kind == 1:
            lambda_init = 0.8 - 0.6 * math.exp(-0.3 * l)
            y = diff_attention_mixer(h, b_w_in[idx], b_w_out[idx], b_qk_g[idx], b_lam[idx],
                                     b_subln[idx], cos, sin, lambda_init)
        else:
            y = short_conv_mixer(h, c_w_in[idx], c_conv[idx], c_w_out[idx])
        x = x + (1.0 + mod[:, 1, 2]) * y
        x = x + 0.5 * (1.0 + mod[:, 2, 2]) * swiglu(pre(x, 2), ffn_wi[l, 1], ffn_wo[l, 1])
    return x
```

```python
import functools
import math

import jax
import jax.numpy as jnp
from jax import lax
from jax.experimental import pallas as pl
from jax.experimental.pallas import tpu as pltpu

F32 = jnp.float32
BF16 = jnp.bfloat16

EPS = 1e-6
N_SUB = 3
N_MIXERS = 3
HEAD_W = 128
N_HEADS = 8
B_DH = 64
B_ROT = 16
ROPE_THETA = 500000.0
MIB = 1024 * 1024

FFN_TM = 512
FFN_FC = 256
MIX_TM = 512
A_CHUNK = 256
A_DIAG = 16
ATT_T = 512


def _silu(z):
    return z * jax.nn.sigmoid(z)


def _dot(a, b):
    return jnp.dot(a, b, preferred_element_type=F32)


def _dot_nt(a, b):
    return lax.dot_general(a, b, (((1,), (1,)), ((), ())), preferred_element_type=F32)


def _dot_tn(a, b):
    return lax.dot_general(a, b, (((0,), (0,)), ((), ())), preferred_element_type=F32)


def _prenorm(x, g, mod):
    ms = jnp.mean(x * x, axis=-1, keepdims=True)
    y = x * lax.rsqrt(ms + EPS) * g
    return y * (1.0 + mod[1:2]) + mod[0:1]


def _resident(shape):
    nd = len(shape)
    return pl.BlockSpec(shape, lambda *_: (0,) * nd, pipeline_mode=pl.Buffered(1))


def _params(sem, vmem_mib):
    return pltpu.CompilerParams(dimension_semantics=sem, vmem_limit_bytes=vmem_mib * MIB)


def _adaln_kernel(c_ref, w_ref, b_ref, o_ref):
    ca = _silu(c_ref[...]).astype(BF16)
    o_ref[0] = _dot(ca, w_ref[0].astype(BF16)) + b_ref[0]


def _adaln(c, ada_w, ada_b):
    depth, d, n = ada_w.shape
    bn = c.shape[0]
    rows = 8 * pl.cdiv(bn, 8)
    cp = jnp.zeros((rows, d), F32).at[:bn].set(c)
    tn = n // 6
    out = pl.pallas_call(
        _adaln_kernel,
        grid=(depth, n // tn),
        in_specs=[
            pl.BlockSpec((rows, d), lambda l, j: (0, 0)),
            pl.BlockSpec((1, d, tn), lambda l, j: (l, 0, j)),
            pl.BlockSpec((1, 1, tn), lambda l, j: (l, 0, j)),
        ],
        out_specs=pl.BlockSpec((1, rows, tn), lambda l, j: (l, 0, j)),
        out_shape=jax.ShapeDtypeStruct((depth, rows, n), F32),
        compiler_params=_params(("arbitrary", "arbitrary"), 40),
        name="adaln",
    )(cp, ada_w, ada_b.reshape(depth, 1, n))
    return out[:, :bn]


def _ffn_kernel(x_ref, mod_ref, g_ref, wi_ref, wo_ref, o_ref, h_ref, *, fc):
    x = x_ref[0]
    mod = mod_ref[0]
    xn = _prenorm(x, g_ref[...], mod).astype(BF16)
    ff = h_ref.shape[1]
    for c in range(ff // fc):
        gu = _dot(xn, wi_ref[:, 2 * c * fc:2 * (c + 1) * fc])
        h_ref[:, c * fc:(c + 1) * fc] = (_silu(gu[:, :fc]) * gu[:, fc:]).astype(BF16)
    y = _dot(h_ref[...], wo_ref[...])
    o_ref[0] = x + (0.5 * (1.0 + mod[2:3])) * y


def _ffn(x, mod, g, wi, wo):
    bn, s, d = x.shape
    ff = wo.shape[0]
    tm, fc = FFN_TM, FFN_FC
    return pl.pallas_call(
        functools.partial(_ffn_kernel, fc=fc),
        grid=(bn, s // tm),
        in_specs=[
            pl.BlockSpec((1, tm, d), lambda b, i: (b, i, 0)),
            pl.BlockSpec((1, 3, d), lambda b, i: (b, 0, 0)),
            _resident((1, d)),
            _resident((d, 2 * ff)),
            _resident((ff, d)),
        ],
        out_specs=pl.BlockSpec((1, tm, d), lambda b, i: (b, i, 0)),
        out_shape=jax.ShapeDtypeStruct(x.shape, x.dtype),
        scratch_shapes=[pltpu.VMEM((tm, ff), BF16)],
        compiler_params=_params(("arbitrary", "arbitrary"), 48),
        name="ffn",
    )(x, mod, g, wi, wo)


def _ffn_weights(wi, wo, fc):
    d, two_f = wi.shape
    ff = two_f // 2
    wi_r = wi.reshape(d, 2, ff // fc, fc).transpose(0, 2, 1, 3).reshape(d, two_f)
    return wi_r.astype(BF16), wo.astype(BF16)


def _hgrn2_kernel(x_ref, mod_ref, g_ref, win_ref, alb_ref, on_ref, wout_ref, o_ref,
                  st_ref, q_ref, k_ref, b_ref, v_ref, gt_ref, oall_ref, *, idx, diag):
    chunk, d = x_ref.shape[1], x_ref.shape[2]
    hw = HEAD_W

    @pl.when(pl.program_id(1) == 0)
    def _():
        st_ref[...] = jnp.zeros_like(st_ref)

    x = x_ref[0]
    mod = mod_ref[0]
    xn = _prenorm(x, g_ref[...], mod).astype(BF16)

    alb = alb_ref[...]
    e = jnp.exp(alb - jnp.max(alb, axis=0, keepdims=True))
    sm = e / jnp.sum(e, axis=0, keepdims=True)
    lb = jnp.sum(sm[0:idx + 1], axis=0, keepdims=True) - sm[0:1]

    q_ref[...] = _silu(_dot(xn, win_ref[:, 0:d]))
    f = lb + (1.0 - lb) * jax.nn.sigmoid(_dot(xn, win_ref[:, d:2 * d]))
    k_ref[...] = 1.0 - f
    v_ref[...] = _dot(xn, win_ref[:, 2 * d:3 * d])
    gt_ref[...] = _silu(_dot(xn, win_ref[:, 3 * d:4 * d]))

    lf = jnp.log(f)
    hi = lf.astype(BF16)
    r1 = lf - hi.astype(F32)
    mid = r1.astype(BF16)
    lo = (r1 - mid.astype(F32)).astype(BF16)
    row = lax.broadcasted_iota(jnp.int32, (chunk, chunk), 0)
    col = lax.broadcasted_iota(jnp.int32, (chunk, chunk), 1)
    tri = (row >= col).astype(BF16)
    b_ref[...] = _dot(tri, hi) + _dot(tri, mid) + _dot(tri, lo)

    levels = []
    w = chunk // 2
    while w >= diag:
        sh = int(math.log2(w))
        rb, cb = row >> sh, col >> sh
        levels.append((w, (rb == cb + 1) & ((cb & 1) == 0)))
        w //= 2
    tmod = lax.broadcasted_iota(jnp.int32, (chunk, 1), 0) & (diag - 1)

    for h in range(d // hw):
        hs = slice(h * hw, (h + 1) * hw)
        q, k, b, v = q_ref[:, hs], k_ref[:, hs], b_ref[:, hs], v_ref[:, hs]
        st = st_ref[h]
        o = _dot_nt((q * jnp.exp(b)).astype(BF16), st.astype(BF16))

        att = jnp.zeros((chunk, chunk), F32)
        for w, mask in levels:
            b3 = b.reshape(chunk // w, w, hw)
            last = b3[:, w - 1:w, :]
            start = jnp.concatenate([jnp.zeros((1, 1, hw), F32), last[:-1]], axis=0)
            qe = q * jnp.exp(b3 - start).reshape(chunk, hw)
            ke = k * jnp.exp(last - b3).reshape(chunk, hw)
            att = jnp.where(mask, _dot_nt(qe.astype(BF16), ke.astype(BF16)), att)
        o = o + _dot(att.astype(BF16), v.astype(BF16))

        for dd in range(diag):
            kd = pltpu.roll(k, dd, 0) if dd else k
            bd = pltpu.roll(b, dd, 0) if dd else b
            vd = pltpu.roll(v, dd, 0) if dd else v
            ok = tmod >= dd
            wgt = q * kd * jnp.exp(jnp.minimum(b - bd, 0.0))
            a = jnp.where(ok, jnp.sum(wgt, axis=-1, keepdims=True), 0.0)
            o = o + a * vd

        bl = b[chunk - 1:chunk, :]
        kt = k * jnp.exp(bl - b)
        st_ref[h] = st * jnp.exp(bl) + _dot_tn(v.astype(BF16), kt.astype(BF16))

        ms = jnp.mean(o * o, axis=-1, keepdims=True)
        on = o * lax.rsqrt(ms + EPS) * on_ref[...]
        oall_ref[:, hs] = (on * gt_ref[:, hs]).astype(BF16)

    y = _dot(oall_ref[...], wout_ref[...])
    o_ref[0] = x + (1.0 + mod[2:3]) * y


def _hgrn2(x, mod, g, w_in, a_lb, onorm, w_out, idx):
    bn, s, d = x.shape
    chunk = A_CHUNK
    nh = d // HEAD_W
    act = pltpu.VMEM((chunk, d), F32)
    return pl.pallas_call(
        functools.partial(_hgrn2_kernel, idx=idx, diag=A_DIAG),
        grid=(bn, s // chunk),
        in_specs=[
            pl.BlockSpec((1, chunk, d), lambda b, i: (b, i, 0)),
            pl.BlockSpec((1, 3, d), lambda b, i: (b, 0, 0)),
            _resident((1, d)),
            _resident((d, 4 * d)),
            _resident(a_lb.shape),
            _resident((1, HEAD_W)),
            _resident((d, d)),
        ],
        out_specs=pl.BlockSpec((1, chunk, d), lambda b, i: (b, i, 0)),
        out_shape=jax.ShapeDtypeStruct(x.shape, x.dtype),
        scratch_shapes=[pltpu.VMEM((nh, HEAD_W, HEAD_W), F32), act, act, act, act, act,
                        pltpu.VMEM((chunk, d), BF16)],
        compiler_params=_params(("arbitrary", "arbitrary"), 48),
        name="hgrn2",
    )(x, mod, g, w_in, a_lb, onorm, w_out)


def _conv_kernel(x_ref, mod_ref, g_ref, win_ref, cw_ref, wout_ref, o_ref, carry_ref):
    tm, d = x_ref.shape[1], x_ref.shape[2]

    @pl.when(pl.program_id(1) == 0)
    def _():
        carry_ref[...] = jnp.zeros_like(carry_ref)

    x = x_ref[0]
    mod = mod_ref[0]
    xn = _prenorm(x, g_ref[...], mod).astype(BF16)
    u = _dot(xn, win_ref[:, d:2 * d]) * _dot(xn, win_ref[:, 2 * d:3 * d])
    carry = carry_ref[...]
    row = lax.broadcasted_iota(jnp.int32, (tm, 1), 0)
    u1 = jnp.where(row == 0, carry[7:8], pltpu.roll(u, 1, 0))
    u2 = jnp.where(row == 0, carry[6:7], jnp.where(row == 1, carry[7:8], pltpu.roll(u, 2, 0)))
    cw = cw_ref[...]
    y = cw[0:1] * u2 + cw[1:2] * u1 + cw[2:3] * u
    carry_ref[...] = u[tm - 8:tm]
    z = (_dot(xn, win_ref[:, 0:d]) * y).astype(BF16)
    o_ref[0] = x + (1.0 + mod[2:3]) * _dot(z, wout_ref[...])


def _short_conv(x, mod, g, w_in, conv_w, w_out):
    bn, s, d = x.shape
    tm = MIX_TM
    return pl.pallas_call(
        _conv_kernel,
        grid=(bn, s // tm),
        in_specs=[
            pl.BlockSpec((1, tm, d), lambda b, i: (b, i, 0)),
            pl.BlockSpec((1, 3, d), lambda b, i: (b, 0, 0)),
            _resident((1, d)),
            _resident((d, 3 * d)),
            _resident(conv_w.shape),
            _resident((d, d)),
        ],
        out_specs=pl.BlockSpec((1, tm, d), lambda b, i: (b, i, 0)),
        out_shape=jax.ShapeDtypeStruct(x.shape, x.dtype),
        scratch_shapes=[pltpu.VMEM((8, d), F32)],
        compiler_params=_params(("arbitrary", "arbitrary"), 48),
        name="short_conv",
    )(x, mod, g, w_in, conv_w, w_out)


def _qkv_kernel(x_ref, mod_ref, g_ref, win_ref, gq_ref, gk_ref, bd_ref, pos_ref, invf_ref,
                q_ref, k_ref, v_ref):
    d = x_ref.shape[2]
    hw = HEAD_W
    xn = _prenorm(x_ref[0], g_ref[...], mod_ref[0]).astype(BF16)

    ang = pos_ref[0].astype(F32) * invf_ref[...]
    lane = lax.broadcasted_iota(jnp.int32, (1, hw), 1) & (B_DH - 1)
    cs, sn = jnp.cos(ang), jnp.sin(ang)
    ca = jnp.where(lane < B_ROT, cs, 1.0)
    cm = jnp.where(lane < B_ROT // 2, -sn, 0.0)
    cp = jnp.where((lane >= B_ROT // 2) & (lane < B_ROT), sn, 0.0)
    half = B_ROT // 2

    gw = 2 * hw
    for off, gain_ref, out_ref, post in ((0, gq_ref, q_ref, B_DH ** -0.5), (d, gk_ref, k_ref, 1.0)):
        for j in range(d // gw):
            z = _dot(xn, win_ref[:, off + j * gw:off + (j + 1) * gw])
            ss = _dot((z * z).astype(BF16), bd_ref[...])
            zn = z * lax.rsqrt(ss * (1.0 / B_DH) + EPS) * gain_ref[:, j * gw:(j + 1) * gw]
            for t in range(gw // hw):
                zz = zn[:, t * hw:(t + 1) * hw]
                r = zz * ca + pltpu.roll(zz, hw - half, 1) * cm + pltpu.roll(zz, half, 1) * cp
                out_ref[0, :, j * gw + t * hw:j * gw + (t + 1) * hw] = (r * post).astype(BF16)
    v_ref[0] = _dot(xn, win_ref[:, 2 * d:3 * d]).astype(BF16)


def _qkv(x, mod, g, w_in, qk_g, positions):
    bn, s, d = x.shape
    tm = MIX_TM
    reps = d // B_DH
    gq = jnp.tile(qk_g[0].astype(F32), reps)[None, :]
    gk = jnp.tile(qk_g[1].astype(F32), reps)[None, :]
    blk = jnp.arange(2 * HEAD_W) // B_DH
    bd = (blk[:, None] == blk[None, :]).astype(BF16)
    inv_freq = ROPE_THETA ** (-jnp.arange(0, B_ROT, 2, dtype=F32) / B_ROT)
    lane = jnp.arange(HEAD_W) % B_DH
    invf = jnp.where(lane < B_ROT, inv_freq[lane % (B_ROT // 2)], 0.0)[None, :]
    tok = pl.BlockSpec((1, tm, d), lambda b, i: (b, i, 0))
    out = jax.ShapeDtypeStruct((bn, s, d), BF16)
    return pl.pallas_call(
        _qkv_kernel,
        grid=(bn, s // tm),
        in_specs=[
            tok,
            pl.BlockSpec((1, 3, d), lambda b, i: (b, 0, 0)),
            _resident((1, d)),
            _resident((d, 3 * d)),
            _resident((1, d)),
            _resident((1, d)),
            _resident((2 * HEAD_W, 2 * HEAD_W)),
            pl.BlockSpec((1, tm, 1), lambda b, i: (b, i, 0)),
            _resident((1, HEAD_W)),
        ],
        out_specs=[tok, tok, tok],
        out_shape=[out, out, out],
        compiler_params=_params(("arbitrary", "arbitrary"), 48),
        name="attn_qkv",
    )(x, mod, g, w_in, gq, gk, bd, positions.reshape(bn, s, 1), invf)


def _attn_kernel(q_ref, k_ref, v_ref, lam_ref, sg_ref, o_ref, m_ref, l_ref, acc_ref, *, lambda_init):
    t = q_ref.shape[1]
    hw = HEAD_W
    i = pl.program_id(2)
    q = q_ref[0]
    lane = lax.broadcasted_iota(jnp.int32, (1, hw), 1)
    qs = (jnp.where(lane < B_DH, q, jnp.zeros_like(q)), jnp.where(lane >= B_DH, q, jnp.zeros_like(q)))

    def block(j, causal):
        kb = k_ref[0, pl.ds(j * t, t), :]
        vb = v_ref[0, pl.ds(j * t, t), :]
        for c in range(2):
            sc = _dot_nt(qs[c], kb)
            if causal:
                r = lax.broadcasted_iota(jnp.int32, (t, t), 0)
                cc = lax.broadcasted_iota(jnp.int32, (t, t), 1)
                sc = jnp.where(r >= cc, sc, -jnp.inf)
            m_old = m_ref[c]
            m_new = jnp.maximum(m_old, jnp.max(sc, axis=-1, keepdims=True))
            p = jnp.exp(sc - m_new)
            alpha = jnp.exp(m_old - m_new)
            l_ref[c] = alpha * l_ref[c] + jnp.sum(p, axis=-1, keepdims=True)
            acc_ref[c] = alpha * acc_ref[c] + _dot(p.astype(BF16), vb)
            m_ref[c] = m_new

    m_ref[...] = jnp.full_like(m_ref, -jnp.inf)
    l_ref[...] = jnp.zeros_like(l_ref)
    acc_ref[...] = jnp.zeros_like(acc_ref)
    block(i, True)

    def body(j, carry):
        block(j, False)
        return carry

    lax.fori_loop(0, i, body, 0)

    lp = lam_ref[...]
    lam = (jnp.exp(jnp.sum(lp[0:1] * lp[1:2], axis=-1, keepdims=True))
           - jnp.exp(jnp.sum(lp[2:3] * lp[3:4], axis=-1, keepdims=True)) + lambda_init)
    o = acc_ref[0] / l_ref[0] - lam * (acc_ref[1] / l_ref[1])
    ms = jnp.mean(o * o, axis=-1, keepdims=True)
    o_ref[0] = (o * lax.rsqrt(ms + EPS) * sg_ref[...] * (1.0 - lambda_init)).astype(o_ref.dtype)


def _attention(q, k, v, lam_p, subln, lambda_init):
    bn, s, d = q.shape
    t = ATT_T
    nh = d // HEAD_W
    seq = pl.BlockSpec((1, s, HEAD_W), lambda b, h, i: (b, 0, h))
    tile = pl.BlockSpec((1, t, HEAD_W), lambda b, h, i: (b, i, h))
    return pl.pallas_call(
        functools.partial(_attn_kernel, lambda_init=lambda_init),
        grid=(bn, nh, s // t),
        in_specs=[tile, seq, seq, _resident(lam_p.shape), _resident((1, HEAD_W))],
        out_specs=tile,
        out_shape=jax.ShapeDtypeStruct((bn, s, d), BF16),
        scratch_shapes=[pltpu.VMEM((2, t, 1), F32), pltpu.VMEM((2, t, 1), F32),
                        pltpu.VMEM((2, t, HEAD_W), F32)],
        compiler_params=_params(("arbitrary", "arbitrary", "arbitrary"), 48),
        name="attn_core",
    )(q, k, v, lam_p, subln)


def _outproj_kernel(a_ref, x_ref, mod_ref, w_ref, o_ref):
    o_ref[0] = x_ref[0] + (1.0 + mod_ref[0][2:3]) * _dot(a_ref[0], w_ref[...])


def _outproj(a, x, mod, w_out):
    bn, s, d = x.shape
    tm = MIX_TM
    tok = pl.BlockSpec((1, tm, d), lambda b, i: (b, i, 0))
    return pl.pallas_call(
        _outproj_kernel,
        grid=(bn, s // tm),
        in_specs=[tok, tok, pl.BlockSpec((1, 3, d), lambda b, i: (b, 0, 0)), _resident((d, d))],
        out_specs=tok,
        out_shape=jax.ShapeDtypeStruct(x.shape, x.dtype),
        compiler_params=_params(("arbitrary", "arbitrary"), 32),
        name="attn_out",
    )(a, x, mod, w_out)


def kernel(x, c, positions, ada_w, ada_b, norm_g, ffn_wi, ffn_wo, a_w_in, a_w_out, a_lb, a_onorm,
           b_w_in, b_w_out, b_qk_g, b_lam, b_subln, c_w_in, c_conv, c_w_out):
    bn, s, d = x.shape
    depth = ada_w.shape[0]
    mod = _adaln(c, ada_w, ada_b).reshape(depth, bn, N_SUB, 3, d)

    for l in range(depth):
        def sub(j):
            return mod[l, :, j], norm_g[l, j][None, :]

        m, g = sub(0)
        x = _ffn(x, m, g, *_ffn_weights(ffn_wi[l, 0], ffn_wo[l, 0], FFN_FC))
        m, g = sub(1)
        kind, idx = l % N_MIXERS, l // N_MIXERS
        if kind == 0:
            x = _hgrn2(x, m, g, a_w_in[idx].astype(BF16), a_lb.astype(F32), a_onorm[idx][None, :].astype(F32),
                       a_w_out[idx].astype(BF16), idx)
        elif kind == 1:
            lambda_init = 0.8 - 0.6 * math.exp(-0.3 * l)
            q, k, v = _qkv(x, m, g, b_w_in[idx].astype(BF16), b_qk_g[idx], positions)
            a = _attention(q, k, v, b_lam[idx].astype(F32), b_subln[idx][None, :].astype(F32), lambda_init)
            x = _outproj(a, x, m, b_w_out[idx].astype(BF16))
        else:
            x = _short_conv(x, m, g, c_w_in[idx].astype(BF16), c_conv[idx].astype(F32), c_w_out[idx].astype(BF16))
        m, g = sub(2)
        x = _ffn(x, m, g, *_ffn_weights(ffn_wi[l, 1], ffn_wo[l, 1], FFN_FC))
    return x
```

```python
import functools
import math

import jax
import jax.numpy as jnp
from jax import lax
from jax.experimental import pallas as pl
from jax.experimental.pallas import tpu as pltpu

F32 = jnp.float32
BF16 = jnp.bfloat16

EPS = 1e-6
N_SUB = 3
N_MIXERS = 3
HEAD_W = 128
N_HEADS = 8
B_DH = 64
B_ROT = 16
ROPE_THETA = 500000.0
MIB = 1024 * 1024

FFN_TM = 512
FFN_FC = 256
MIX_TM = 512
A_CHUNK = 256
A_DIAG = 16
ATT_T = MIX_TM
ATT_R = 128


def _silu(z):
    return z * jax.nn.sigmoid(z)


def _dot(a, b):
    return jnp.dot(a, b, preferred_element_type=F32)


def _dot_nt(a, b):
    return lax.dot_general(a, b, (((1,), (1,)), ((), ())), preferred_element_type=F32)


def _dot_tn(a, b):
    return lax.dot_general(a, b, (((0,), (0,)), ((), ())), preferred_element_type=F32)


def _prenorm(x, g, mod):
    ms = jnp.mean(x * x, axis=-1, keepdims=True)
    y = x * lax.rsqrt(ms + EPS) * g
    return y * (1.0 + mod[1:2]) + mod[0:1]


def _resident(shape):
    nd = len(shape)
    return pl.BlockSpec(shape, lambda *_: (0,) * nd, pipeline_mode=pl.Buffered(1))


def _params(sem, vmem_mib):
    return pltpu.CompilerParams(dimension_semantics=sem, vmem_limit_bytes=vmem_mib * MIB)


def _adaln_kernel(c_ref, w_ref, b_ref, o_ref):
    ca = _silu(c_ref[...]).astype(BF16)
    o_ref[0] = _dot(ca, w_ref[0].astype(BF16)) + b_ref[0]


def _adaln(c, ada_w, ada_b):
    depth, d, n = ada_w.shape
    bn = c.shape[0]
    rows = 8 * pl.cdiv(bn, 8)
    cp = jnp.zeros((rows, d), F32).at[:bn].set(c)
    tn = n // 6
    out = pl.pallas_call(
        _adaln_kernel,
        grid=(depth, n // tn),
        in_specs=[
            pl.BlockSpec((rows, d), lambda l, j: (0, 0)),
            pl.BlockSpec((1, d, tn), lambda l, j: (l, 0, j)),
            pl.BlockSpec((1, 1, tn), lambda l, j: (l, 0, j)),
        ],
        out_specs=pl.BlockSpec((1, rows, tn), lambda l, j: (l, 0, j)),
        out_shape=jax.ShapeDtypeStruct((depth, rows, n), F32),
        compiler_params=_params(("arbitrary", "arbitrary"), 40),
        name="adaln",
    )(cp, ada_w, ada_b.reshape(depth, 1, n))
    return out[:, :bn]


def _ffn_kernel(x_ref, mod_ref, g_ref, wi_ref, wo_ref, o_ref, h_ref, *, fc):
    x = x_ref[0]
    mod = mod_ref[0]
    xn = _prenorm(x, g_ref[...], mod).astype(BF16)
    ff = h_ref.shape[1]
    for c in range(ff // fc):
        gu = _dot(xn, wi_ref[:, 2 * c * fc:2 * (c + 1) * fc])
        h_ref[:, c * fc:(c + 1) * fc] = (_silu(gu[:, :fc]) * gu[:, fc:]).astype(BF16)
    y = _dot(h_ref[...], wo_ref[...])
    o_ref[0] = x + (0.5 * (1.0 + mod[2:3])) * y


def _ffn(x, mod, g, wi, wo):
    bn, s, d = x.shape
    ff = wo.shape[0]
    tm, fc = FFN_TM, FFN_FC
    return pl.pallas_call(
        functools.partial(_ffn_kernel, fc=fc),
        grid=(bn, s // tm),
        in_specs=[
            pl.BlockSpec((1, tm, d), lambda b, i: (b, i, 0)),
            pl.BlockSpec((1, 3, d), lambda b, i: (b, 0, 0)),
            _resident((1, d)),
            _resident((d, 2 * ff)),
            _resident((ff, d)),
        ],
        out_specs=pl.BlockSpec((1, tm, d), lambda b, i: (b, i, 0)),
        out_shape=jax.ShapeDtypeStruct(x.shape, x.dtype),
        scratch_shapes=[pltpu.VMEM((tm, ff), BF16)],
        compiler_params=_params(("arbitrary", "arbitrary"), 48),
        name="ffn",
    )(x, mod, g, wi, wo)


def _ffn_weights(wi, wo, fc):
    d, two_f = wi.shape
    ff = two_f // 2
    wi_r = wi.reshape(d, 2, ff // fc, fc).transpose(0, 2, 1, 3).reshape(d, two_f)
    return wi_r.astype(BF16), wo.astype(BF16)


def _hgrn2_kernel(x_ref, mod_ref, g_ref, win_ref, alb_ref, on_ref, wout_ref, o_ref,
                  st_ref, q_ref, k_ref, b_ref, v_ref, gt_ref, oall_ref, *, idx, diag):
    chunk, d = x_ref.shape[1], x_ref.shape[2]
    hw = HEAD_W

    @pl.when(pl.program_id(1) == 0)
    def _():
        st_ref[...] = jnp.zeros_like(st_ref)

    x = x_ref[0]
    mod = mod_ref[0]
    xn = _prenorm(x, g_ref[...], mod).astype(BF16)

    alb = alb_ref[...]
    e = jnp.exp(alb - jnp.max(alb, axis=0, keepdims=True))
    sm = e / jnp.sum(e, axis=0, keepdims=True)
    lb = jnp.sum(sm[0:idx + 1], axis=0, keepdims=True) - sm[0:1]

    q_ref[...] = _silu(_dot(xn, win_ref[:, 0:d]))
    f = lb + (1.0 - lb) * jax.nn.sigmoid(_dot(xn, win_ref[:, d:2 * d]))
    k_ref[...] = 1.0 - f
    v_ref[...] = _dot(xn, win_ref[:, 2 * d:3 * d])
    gt_ref[...] = _silu(_dot(xn, win_ref[:, 3 * d:4 * d]))

    lf = jnp.log(f)
    hi = lf.astype(BF16)
    r1 = lf - hi.astype(F32)
    mid = r1.astype(BF16)
    lo = (r1 - mid.astype(F32)).astype(BF16)
    row = lax.broadcasted_iota(jnp.int32, (chunk, chunk), 0)
    col = lax.broadcasted_iota(jnp.int32, (chunk, chunk), 1)
    tri = (row >= col).astype(BF16)
    b_ref[...] = _dot(tri, hi) + _dot(tri, mid) + _dot(tri, lo)

    levels = []
    w = chunk // 2
    while w >= diag:
        sh = int(math.log2(w))
        rb, cb = row >> sh, col >> sh
        levels.append((w, (rb == cb + 1) & ((cb & 1) == 0)))
        w //= 2
    tmod = lax.broadcasted_iota(jnp.int32, (chunk, 1), 0) & (diag - 1)

    for h in range(d // hw):
        hs = slice(h * hw, (h + 1) * hw)
        q, k, b, v = q_ref[:, hs], k_ref[:, hs], b_ref[:, hs], v_ref[:, hs]
        st = st_ref[h]
        o = _dot_nt((q * jnp.exp(b)).astype(BF16), st.astype(BF16))

        att = jnp.zeros((chunk, chunk), F32)
        for w, mask in levels:
            b3 = b.reshape(chunk // w, w, hw)
            last = b3[:, w - 1:w, :]
            start = jnp.concatenate([jnp.zeros((1, 1, hw), F32), last[:-1]], axis=0)
            qe = q * jnp.exp(b3 - start).reshape(chunk, hw)
            ke = k * jnp.exp(last - b3).reshape(chunk, hw)
            att = jnp.where(mask, _dot_nt(qe.astype(BF16), ke.astype(BF16)), att)
        o = o + _dot(att.astype(BF16), v.astype(BF16))

        for dd in range(diag):
            kd = pltpu.roll(k, dd, 0) if dd else k
            bd = pltpu.roll(b, dd, 0) if dd else b
            vd = pltpu.roll(v, dd, 0) if dd else v
            ok = tmod >= dd
            wgt = q * kd * jnp.exp(jnp.minimum(b - bd, 0.0))
            a = jnp.where(ok, jnp.sum(wgt, axis=-1, keepdims=True), 0.0)
            o = o + a * vd

        bl = b[chunk - 1:chunk, :]
        kt = k * jnp.exp(bl - b)
        st_ref[h] = st * jnp.exp(bl) + _dot_tn(v.astype(BF16), kt.astype(BF16))

        ms = jnp.mean(o * o, axis=-1, keepdims=True)
        on = o * lax.rsqrt(ms + EPS) * on_ref[...]
        oall_ref[:, hs] = (on * gt_ref[:, hs]).astype(BF16)

    y = _dot(oall_ref[...], wout_ref[...])
    o_ref[0] = x + (1.0 + mod[2:3]) * y


def _hgrn2(x, mod, g, w_in, a_lb, onorm, w_out, idx):
    bn, s, d = x.shape
    chunk = A_CHUNK
    nh = d // HEAD_W
    act = pltpu.VMEM((chunk, d), F32)
    return pl.pallas_call(
        functools.partial(_hgrn2_kernel, idx=idx, diag=A_DIAG),
        grid=(bn, s // chunk),
        in_specs=[
            pl.BlockSpec((1, chunk, d), lambda b, i: (b, i, 0)),
            pl.BlockSpec((1, 3, d), lambda b, i: (b, 0, 0)),
            _resident((1, d)),
            _resident((d, 4 * d)),
            _resident(a_lb.shape),
            _resident((1, HEAD_W)),
            _resident((d, d)),
        ],
        out_specs=pl.BlockSpec((1, chunk, d), lambda b, i: (b, i, 0)),
        out_shape=jax.ShapeDtypeStruct(x.shape, x.dtype),
        scratch_shapes=[pltpu.VMEM((nh, HEAD_W, HEAD_W), F32), act, act, act, act, act,
                        pltpu.VMEM((chunk, d), BF16)],
        compiler_params=_params(("arbitrary", "arbitrary"), 48),
        name="hgrn2",
    )(x, mod, g, w_in, a_lb, onorm, w_out)


def _conv_kernel(x_ref, mod_ref, g_ref, win_ref, cw_ref, wout_ref, o_ref, carry_ref):
    tm, d = x_ref.shape[1], x_ref.shape[2]

    @pl.when(pl.program_id(1) == 0)
    def _():
        carry_ref[...] = jnp.zeros_like(carry_ref)

    x = x_ref[0]
    mod = mod_ref[0]
    xn = _prenorm(x, g_ref[...], mod).astype(BF16)
    u = _dot(xn, win_ref[:, d:2 * d]) * _dot(xn, win_ref[:, 2 * d:3 * d])
    carry = carry_ref[...]
    row = lax.broadcasted_iota(jnp.int32, (tm, 1), 0)
    u1 = jnp.where(row == 0, carry[7:8], pltpu.roll(u, 1, 0))
    u2 = jnp.where(row == 0, carry[6:7], jnp.where(row == 1, carry[7:8], pltpu.roll(u, 2, 0)))
    cw = cw_ref[...]
    y = cw[0:1] * u2 + cw[1:2] * u1 + cw[2:3] * u
    carry_ref[...] = u[tm - 8:tm]
    z = (_dot(xn, win_ref[:, 0:d]) * y).astype(BF16)
    o_ref[0] = x + (1.0 + mod[2:3]) * _dot(z, wout_ref[...])


def _short_conv(x, mod, g, w_in, conv_w, w_out):
    bn, s, d = x.shape
    tm = MIX_TM
    return pl.pallas_call(
        _conv_kernel,
        grid=(bn, s // tm),
        in_specs=[
            pl.BlockSpec((1, tm, d), lambda b, i: (b, i, 0)),
            pl.BlockSpec((1, 3, d), lambda b, i: (b, 0, 0)),
            _resident((1, d)),
            _resident((d, 3 * d)),
            _resident(conv_w.shape),
            _resident((d, d)),
        ],
        out_specs=pl.BlockSpec((1, tm, d), lambda b, i: (b, i, 0)),
        out_shape=jax.ShapeDtypeStruct(x.shape, x.dtype),
        scratch_shapes=[pltpu.VMEM((8, d), F32)],
        compiler_params=_params(("arbitrary", "arbitrary"), 48),
        name="short_conv",
    )(x, mod, g, w_in, conv_w, w_out)


def _qkv_kernel(x_ref, mod_ref, g_ref, win_ref, gq_ref, gk_ref, bd_ref, pos_ref, invf_ref,
                q_ref, k_ref, v_ref):
    d = x_ref.shape[2]
    hw = HEAD_W
    xn = _prenorm(x_ref[0], g_ref[...], mod_ref[0]).astype(BF16)

    ang = pos_ref[0].astype(F32) * invf_ref[...]
    lane = lax.broadcasted_iota(jnp.int32, (1, hw), 1) & (B_DH - 1)
    cs, sn = jnp.cos(ang), jnp.sin(ang)
    ca = jnp.where(lane < B_ROT, cs, 1.0)
    cm = jnp.where(lane < B_ROT // 2, -sn, 0.0)
    cp = jnp.where((lane >= B_ROT // 2) & (lane < B_ROT), sn, 0.0)
    half = B_ROT // 2

    gw = 2 * hw
    for off, gain_ref, is_q in ((0, gq_ref, True), (d, gk_ref, False)):
        for j in range(d // gw):
            z = _dot(xn, win_ref[:, off + j * gw:off + (j + 1) * gw])
            ss = _dot((z * z).astype(BF16), bd_ref[...])
            zn = z * lax.rsqrt(ss * (1.0 / B_DH) + EPS) * gain_ref[:, j * gw:(j + 1) * gw]
            for t in range(gw // hw):
                zz = zn[:, t * hw:(t + 1) * hw]
                r = zz * ca + pltpu.roll(zz, hw - half, 1) * cm + pltpu.roll(zz, half, 1) * cp
                head = j * (gw // hw) + t
                if is_q:
                    q_ref[0, :, head * hw:(head + 1) * hw] = (r * B_DH ** -0.5).astype(BF16)
                else:
                    k_ref[0, head, 0] = r.T.astype(BF16)
    v_ref[0] = _dot(xn, win_ref[:, 2 * d:3 * d]).astype(BF16)


def _qkv(x, mod, g, w_in, qk_g, positions):
    bn, s, d = x.shape
    tm = MIX_TM
    reps = d // B_DH
    gq = jnp.tile(qk_g[0].astype(F32), reps)[None, :]
    gk = jnp.tile(qk_g[1].astype(F32), reps)[None, :]
    blk = jnp.arange(2 * HEAD_W) // B_DH
    bd = (blk[:, None] == blk[None, :]).astype(BF16)
    inv_freq = ROPE_THETA ** (-jnp.arange(0, B_ROT, 2, dtype=F32) / B_ROT)
    lane = jnp.arange(HEAD_W) % B_DH
    invf = jnp.where(lane < B_ROT, inv_freq[lane % (B_ROT // 2)], 0.0)[None, :]
    tok = pl.BlockSpec((1, tm, d), lambda b, i: (b, i, 0))
    out = jax.ShapeDtypeStruct((bn, s, d), BF16)
    nh = d // HEAD_W
    kt_spec = pl.BlockSpec((1, nh, 1, HEAD_W, tm), lambda b, i: (b, 0, i, 0, 0))
    kt_out = jax.ShapeDtypeStruct((bn, nh, s // tm, HEAD_W, tm), BF16)
    return pl.pallas_call(
        _qkv_kernel,
        grid=(bn, s // tm),
        in_specs=[
            tok,
            pl.BlockSpec((1, 3, d), lambda b, i: (b, 0, 0)),
            _resident((1, d)),
            _resident((d, 3 * d)),
            _resident((1, d)),
            _resident((1, d)),
            _resident((2 * HEAD_W, 2 * HEAD_W)),
            pl.BlockSpec((1, tm, 1), lambda b, i: (b, i, 0)),
            _resident((1, HEAD_W)),
        ],
        out_specs=[tok, kt_spec, tok],
        out_shape=[out, kt_out, out],
        compiler_params=_params(("arbitrary", "arbitrary"), 48),
        name="attn_qkv",
    )(x, mod, g, w_in, gq, gk, bd, positions.reshape(bn, s, 1), invf)


def _attn_kernel(q_ref, kt_ref, v_ref, lam_ref, sg_ref, o_ref, qs_ref, m_ref, acc_ref, *, lambda_init, rows):
    t = q_ref.shape[1]
    hw = HEAD_W
    i = pl.program_id(2)
    q = q_ref[0]
    lane = lax.broadcasted_iota(jnp.int32, (1, hw), 1)
    qs_ref[0] = jnp.where(lane < B_DH, q, jnp.zeros_like(q))
    qs_ref[1] = jnp.where(lane >= B_DH, q, jnp.zeros_like(q))
    ones = jnp.ones((t, hw), BF16)

    def unit(c, rc, kt, vx, first):
        rs = slice(rc * rows, (rc + 1) * rows)
        nk = kt.shape[1]
        sc = _dot(qs_ref[c, rs, :], kt)
        if first:
            r = rc * rows + lax.broadcasted_iota(jnp.int32, (rows, nk), 0)
            cc = lax.broadcasted_iota(jnp.int32, (rows, nk), 1)
            sc = jnp.where(r >= cc, sc, -jnp.inf)
        cols = [sc[:, a * hw:(a + 1) * hw] for a in range(nk // hw)]
        rmax = jnp.max(functools.reduce(jnp.maximum, cols), axis=-1, keepdims=True)
        if first:
            m_new = jnp.broadcast_to(rmax, (rows, hw))
        else:
            m_old = m_ref[c, rs, :]
            m_new = jnp.maximum(m_old, rmax)
        p = jnp.concatenate([jnp.exp(cl - m_new) for cl in cols], axis=1).astype(BF16)
        pv = _dot(p, vx)
        if first:
            acc_ref[c, rs, :] = pv
        else:
            alpha = jnp.exp(m_old - m_new)
            acc_ref[c, rs, :] = acc_ref[c, rs, :] * jnp.concatenate([alpha, alpha], axis=1) + pv
        m_ref[c, rs, :] = m_new

    ktd = kt_ref[0, 0, i]
    for rc in range(t // rows):
        nk = (rc + 1) * rows
        vx = jnp.concatenate([v_ref[0, pl.ds(i * t, nk), :], ones[:nk]], axis=1)
        for c in range(2):
            unit(c, rc, ktd[:, :nk], vx, True)

    def body(j, carry):
        kt = kt_ref[0, 0, j]
        vx = jnp.concatenate([v_ref[0, pl.ds(j * t, t), :], ones], axis=1)
        for rc in range(t // rows):
            for c in range(2):
                unit(c, rc, kt, vx, False)
        return carry

    lax.fori_loop(0, i, body, 0)

    lp = lam_ref[...]
    lam = (jnp.exp(jnp.sum(lp[0:1] * lp[1:2], axis=-1, keepdims=True))
           - jnp.exp(jnp.sum(lp[2:3] * lp[3:4], axis=-1, keepdims=True)) + lambda_init)
    a0, a1 = acc_ref[0], acc_ref[1]
    o = a0[:, :hw] / a0[:, hw:] - lam * (a1[:, :hw] / a1[:, hw:])
    ms = jnp.mean(o * o, axis=-1, keepdims=True)
    o_ref[0] = (o * lax.rsqrt(ms + EPS) * sg_ref[...] * (1.0 - lambda_init)).astype(o_ref.dtype)


def _attention(q, kt, v, lam_p, subln, lambda_init):
    bn, s, d = q.shape
    t = ATT_T
    nh = d // HEAD_W
    assert kt.shape == (bn, nh, s // t, HEAD_W, t)
    tile = pl.BlockSpec((1, t, HEAD_W), lambda b, h, i: (b, i, h))
    return pl.pallas_call(
        functools.partial(_attn_kernel, lambda_init=lambda_init, rows=ATT_R),
        grid=(bn, nh, s // t),
        in_specs=[
            tile,
            pl.BlockSpec((1, 1, s // t, HEAD_W, t), lambda b, h, i: (b, h, 0, 0, 0)),
            pl.BlockSpec((1, s, HEAD_W), lambda b, h, i: (b, 0, h)),
            _resident(lam_p.shape),
            _resident((1, HEAD_W)),
        ],
        out_specs=tile,
        out_shape=jax.ShapeDtypeStruct((bn, s, d), BF16),
        scratch_shapes=[pltpu.VMEM((2, t, HEAD_W), BF16), pltpu.VMEM((2, t, HEAD_W), F32),
                        pltpu.VMEM((2, t, 2 * HEAD_W), F32)],
        compiler_params=_params(("arbitrary", "arbitrary", "arbitrary"), 48),
        name="attn_core",
    )(q, kt, v, lam_p, subln)


def _outproj_kernel(a_ref, x_ref, mod_ref, w_ref, o_ref):
    o_ref[0] = x_ref[0] + (1.0 + mod_ref[0][2:3]) * _dot(a_ref[0], w_ref[...])


def _outproj(a, x, mod, w_out):
    bn, s, d = x.shape
    tm = MIX_TM
    tok = pl.BlockSpec((1, tm, d), lambda b, i: (b, i, 0))
    return pl.pallas_call(
        _outproj_kernel,
        grid=(bn, s // tm),
        in_specs=[tok, tok, pl.BlockSpec((1, 3, d), lambda b, i: (b, 0, 0)), _resident((d, d))],
        out_specs=tok,
        out_shape=jax.ShapeDtypeStruct(x.shape, x.dtype),
        compiler_params=_params(("arbitrary", "arbitrary"), 32),
        name="attn_out",
    )(a, x, mod, w_out)


def kernel(x, c, positions, ada_w, ada_b, norm_g, ffn_wi, ffn_wo, a_w_in, a_w_out, a_lb, a_onorm,
           b_w_in, b_w_out, b_qk_g, b_lam, b_subln, c_w_in, c_conv, c_w_out):
    bn, s, d = x.shape
    depth = ada_w.shape[0]
    mod = _adaln(c, ada_w, ada_b).reshape(depth, bn, N_SUB, 3, d)

    for l in range(depth):
        def sub(j):
            return mod[l, :, j], norm_g[l, j][None, :]

        m, g = sub(0)
        x = _ffn(x, m, g, *_ffn_weights(ffn_wi[l, 0], ffn_wo[l, 0], FFN_FC))
        m, g = sub(1)
        kind, idx = l % N_MIXERS, l // N_MIXERS
        if kind == 0:
            x = _hgrn2(x, m, g, a_w_in[idx].astype(BF16), a_lb.astype(F32), a_onorm[idx][None, :].astype(F32),
                       a_w_out[idx].astype(BF16), idx)
        elif kind == 1:
            lambda_init = 0.8 - 0.6 * math.exp(-0.3 * l)
            q, k, v = _qkv(x, m, g, b_w_in[idx].astype(BF16), b_qk_g[idx], positions)
            a = _attention(q, k, v, b_lam[idx].astype(F32), b_subln[idx][None, :].astype(F32), lambda_init)
            x = _outproj(a, x, m, b_w_out[idx].astype(BF16))
        else:
            x = _short_conv(x, m, g, c_w_in[idx].astype(BF16), c_conv[idx].astype(F32), c_w_out[idx].astype(BF16))
        m, g = sub(2)
        x = _ffn(x, m, g, *_ffn_weights(ffn_wi[l, 1], ffn_wo[l, 1], FFN_FC))
    return x
```

```python
import functools
import math

import jax
import jax.numpy as jnp
from jax import lax
from jax.experimental import pallas as pl
from jax.experimental.pallas import tpu as pltpu

F32 = jnp.float32
BF16 = jnp.bfloat16

EPS = 1e-6
N_SUB = 3
N_MIXERS = 3
HEAD_W = 128
N_HEADS = 8
B_DH = 64
B_ROT = 16
ROPE_THETA = 500000.0
MIB = 1024 * 1024

FFN_TM = 512
FFN_FC = 256
MIX_TM = 512
A_CHUNK = 256
A_SUB = 128
ATT_T = MIX_TM
ATT_R = 128


def _silu(z):
    return z * jax.nn.sigmoid(z)


def _dot(a, b):
    return jnp.dot(a, b, preferred_element_type=F32)


def _dot_nt(a, b):
    return lax.dot_general(a, b, (((1,), (1,)), ((), ())), preferred_element_type=F32)


def _dot_tn(a, b):
    return lax.dot_general(a, b, (((0,), (0,)), ((), ())), preferred_element_type=F32)


def _prenorm(x, g, mod):
    ms = jnp.mean(x * x, axis=-1, keepdims=True)
    y = x * lax.rsqrt(ms + EPS) * g
    return y * (1.0 + mod[1:2]) + mod[0:1]


def _resident(shape):
    nd = len(shape)
    return pl.BlockSpec(shape, lambda *_: (0,) * nd, pipeline_mode=pl.Buffered(1))


def _params(sem, vmem_mib):
    return pltpu.CompilerParams(dimension_semantics=sem, vmem_limit_bytes=vmem_mib * MIB)


def _adaln_kernel(c_ref, w_ref, b_ref, o_ref):
    ca = _silu(c_ref[...]).astype(BF16)
    o_ref[0] = _dot(ca, w_ref[0].astype(BF16)) + b_ref[0]


def _adaln(c, ada_w, ada_b):
    depth, d, n = ada_w.shape
    bn = c.shape[0]
    rows = 8 * pl.cdiv(bn, 8)
    cp = jnp.zeros((rows, d), F32).at[:bn].set(c)
    tn = n // 6
    out = pl.pallas_call(
        _adaln_kernel,
        grid=(depth, n // tn),
        in_specs=[
            pl.BlockSpec((rows, d), lambda l, j: (0, 0)),
            pl.BlockSpec((1, d, tn), lambda l, j: (l, 0, j)),
            pl.BlockSpec((1, 1, tn), lambda l, j: (l, 0, j)),
        ],
        out_specs=pl.BlockSpec((1, rows, tn), lambda l, j: (l, 0, j)),
        out_shape=jax.ShapeDtypeStruct((depth, rows, n), F32),
        compiler_params=_params(("arbitrary", "arbitrary"), 40),
        name="adaln",
    )(cp, ada_w, ada_b.reshape(depth, 1, n))
    return out[:, :bn]


def _ffn_kernel(x_ref, mod_ref, g_ref, wi_ref, wo_ref, o_ref, h_ref, *, fc):
    x = x_ref[0]
    mod = mod_ref[0]
    xn = _prenorm(x, g_ref[...], mod).astype(BF16)
    ff = h_ref.shape[1]
    for c in range(ff // fc):
        gu = _dot(xn, wi_ref[:, 2 * c * fc:2 * (c + 1) * fc])
        h_ref[:, c * fc:(c + 1) * fc] = (_silu(gu[:, :fc]) * gu[:, fc:]).astype(BF16)
    y = _dot(h_ref[...], wo_ref[...])
    o_ref[0] = x + (0.5 * (1.0 + mod[2:3])) * y


def _ffn(x, mod, g, wi, wo):
    bn, s, d = x.shape
    ff = wo.shape[0]
    tm, fc = FFN_TM, FFN_FC
    return pl.pallas_call(
        functools.partial(_ffn_kernel, fc=fc),
        grid=(bn, s // tm),
        in_specs=[
            pl.BlockSpec((1, tm, d), lambda b, i: (b, i, 0)),
            pl.BlockSpec((1, 3, d), lambda b, i: (b, 0, 0)),
            _resident((1, d)),
            _resident((d, 2 * ff)),
            _resident((ff, d)),
        ],
        out_specs=pl.BlockSpec((1, tm, d), lambda b, i: (b, i, 0)),
        out_shape=jax.ShapeDtypeStruct(x.shape, x.dtype),
        scratch_shapes=[pltpu.VMEM((tm, ff), BF16)],
        compiler_params=_params(("arbitrary", "arbitrary"), 48),
        name="ffn",
    )(x, mod, g, wi, wo)


def _ffn_weights(wi, wo, fc):
    d, two_f = wi.shape
    ff = two_f // 2
    wi_r = wi.reshape(d, 2, ff // fc, fc).transpose(0, 2, 1, 3).reshape(d, two_f)
    return wi_r.astype(BF16), wo.astype(BF16)


def _block_masks(n):
    row = lax.broadcasted_iota(jnp.int32, (n, n), 0)
    col = lax.broadcasted_iota(jnp.int32, (n, n), 1)
    masks = [row == col]
    w = n // 2
    while w >= 1:
        sh = int(math.log2(w))
        rb, cb = row >> sh, col >> sh
        masks.append((rb == cb + 1) & ((cb & 1) == 0))
        w //= 2
    return masks


def _block_scores(q, k, lf, b, masks):
    n, hw = q.shape
    att = jnp.where(masks[0], _dot_nt(q.astype(BF16), k.astype(BF16)), 0.0)
    level = 1
    w = n // 2
    while w >= 8:
        b3 = b.reshape(n // w, w, hw)
        last = b3[:, w - 1:w, :]
        prev = jnp.concatenate([last[:1], last[:-1]], axis=0)
        qa = jnp.minimum(b3 - prev, 0.0).reshape(n, hw)
        ka = (last - b3).reshape(n, hw)
        sc = _dot_nt((q * jnp.exp(qa)).astype(BF16), (k * jnp.exp(ka)).astype(BF16))
        att = jnp.where(masks[level], sc, att)
        level += 1
        w //= 2
    t = lax.broadcasted_iota(jnp.int32, (n, 1), 0)
    back = [None] + [pltpu.roll(lf, i, 0) for i in range(1, 4)]
    fwd = [None] + [pltpu.roll(lf, n - i, 0) for i in range(1, 4)]
    while w >= 1:
        r = t & (w - 1)
        qa = lf
        ke = k
        if w > 1:
            ka = jnp.where(r <= w - 2, fwd[1], 0.0)
            for i in range(1, w):
                qa = qa + jnp.where(r >= i, back[i], 0.0)
            for i in range(2, w):
                ka = ka + jnp.where(r <= w - 1 - i, fwd[i], 0.0)
            ke = k * jnp.exp(ka)
        sc = _dot_nt((q * jnp.exp(qa)).astype(BF16), ke.astype(BF16))
        att = jnp.where(masks[level], sc, att)
        level += 1
        w //= 2
    return att


def _hgrn2_kernel(x_ref, mod_ref, g_ref, win_ref, alb_ref, on_ref, wout_ref, o_ref,
                  st_ref, q_ref, k_ref, lf_ref, b_ref, v_ref, gt_ref, oall_ref, *, idx, sub):
    chunk, d = x_ref.shape[1], x_ref.shape[2]
    hw = HEAD_W

    @pl.when(pl.program_id(1) == 0)
    def _():
        st_ref[...] = jnp.zeros_like(st_ref)

    x = x_ref[0]
    mod = mod_ref[0]
    xn = _prenorm(x, g_ref[...], mod).astype(BF16)

    alb = alb_ref[...]
    e = jnp.exp(alb - jnp.max(alb, axis=0, keepdims=True))
    sm = e / jnp.sum(e, axis=0, keepdims=True)
    lb = jnp.sum(sm[0:idx + 1], axis=0, keepdims=True) - sm[0:1]

    q_ref[...] = _silu(_dot(xn, win_ref[:, 0:d]))
    f = lb + (1.0 - lb) * jax.nn.sigmoid(_dot(xn, win_ref[:, d:2 * d]))
    k_ref[...] = 1.0 - f
    v_ref[...] = _dot(xn, win_ref[:, 2 * d:3 * d])
    gt_ref[...] = _silu(_dot(xn, win_ref[:, 3 * d:4 * d]))

    lf = jnp.log(f)
    lf_ref[...] = lf
    hi = lf.astype(BF16)
    r1 = lf - hi.astype(F32)
    mid = r1.astype(BF16)
    lo = (r1 - mid.astype(F32)).astype(BF16)
    row = lax.broadcasted_iota(jnp.int32, (chunk, chunk), 0)
    col = lax.broadcasted_iota(jnp.int32, (chunk, chunk), 1)
    tri = (row >= col).astype(BF16)
    b_ref[...] = _dot(tri, hi) + _dot(tri, mid) + _dot(tri, lo)

    masks = _block_masks(sub)

    for h in range(d // hw):
        hs = slice(h * hw, (h + 1) * hw)
        q, k, lf, b, v = q_ref[:, hs], k_ref[:, hs], lf_ref[:, hs], b_ref[:, hs], v_ref[:, hs]
        vb = v.astype(BF16)
        st = st_ref[h]
        inter = _dot_nt((q * jnp.exp(b)).astype(BF16), st.astype(BF16))

        outs = []
        for i in range(chunk // sub):
            rs = slice(i * sub, (i + 1) * sub)
            att = _block_scores(q[rs], k[rs], lf[rs], b[rs], masks)
            oi = inter[rs] + _dot(att.astype(BF16), vb[rs])
            if i > 0:
                edge = b[i * sub - 1:i * sub, :]
                qc = (q[rs] * jnp.exp(b[rs] - edge)).astype(BF16)
                for j in range(i):
                    js = slice(j * sub, (j + 1) * sub)
                    kc = (k[js] * jnp.exp(edge - b[js])).astype(BF16)
                    oi = oi + _dot(_dot_nt(qc, kc).astype(BF16), vb[js])
            outs.append(oi)
        o = jnp.concatenate(outs, axis=0)

        bl = b[chunk - 1:chunk, :]
        kt = k * jnp.exp(bl - b)
        st_ref[h] = st * jnp.exp(bl) + _dot_tn(vb, kt.astype(BF16))

        ms = jnp.mean(o * o, axis=-1, keepdims=True)
        on = o * lax.rsqrt(ms + EPS) * on_ref[...]
        oall_ref[:, hs] = (on * gt_ref[:, hs]).astype(BF16)

    y = _dot(oall_ref[...], wout_ref[...])
    o_ref[0] = x + (1.0 + mod[2:3]) * y


def _hgrn2(x, mod, g, w_in, a_lb, onorm, w_out, idx):
    bn, s, d = x.shape
    chunk = A_CHUNK
    nh = d // HEAD_W
    act = pltpu.VMEM((chunk, d), F32)
    return pl.pallas_call(
        functools.partial(_hgrn2_kernel, idx=idx, sub=A_SUB),
        grid=(bn, s // chunk),
        in_specs=[
            pl.BlockSpec((1, chunk, d), lambda b, i: (b, i, 0)),
            pl.BlockSpec((1, 3, d), lambda b, i: (b, 0, 0)),
            _resident((1, d)),
            _resident((d, 4 * d)),
            _resident(a_lb.shape),
            _resident((1, HEAD_W)),
            _resident((d, d)),
        ],
        out_specs=pl.BlockSpec((1, chunk, d), lambda b, i: (b, i, 0)),
        out_shape=jax.ShapeDtypeStruct(x.shape, x.dtype),
        scratch_shapes=[pltpu.VMEM((nh, HEAD_W, HEAD_W), F32), act, act, act, act, act, act,
                        pltpu.VMEM((chunk, d), BF16)],
        compiler_params=_params(("arbitrary", "arbitrary"), 48),
        name="hgrn2",
    )(x, mod, g, w_in, a_lb, onorm, w_out)


def _conv_kernel(x_ref, mod_ref, g_ref, win_ref, cw_ref, wout_ref, o_ref, carry_ref):
    tm, d = x_ref.shape[1], x_ref.shape[2]

    @pl.when(pl.program_id(1) == 0)
    def _():
        carry_ref[...] = jnp.zeros_like(carry_ref)

    x = x_ref[0]
    mod = mod_ref[0]
    xn = _prenorm(x, g_ref[...], mod).astype(BF16)
    u = _dot(xn, win_ref[:, d:2 * d]) * _dot(xn, win_ref[:, 2 * d:3 * d])
    carry = carry_ref[...]
    row = lax.broadcasted_iota(jnp.int32, (tm, 1), 0)
    u1 = jnp.where(row == 0, carry[7:8], pltpu.roll(u, 1, 0))
    u2 = jnp.where(row == 0, carry[6:7], jnp.where(row == 1, carry[7:8], pltpu.roll(u, 2, 0)))
    cw = cw_ref[...]
    y = cw[0:1] * u2 + cw[1:2] * u1 + cw[2:3] * u
    carry_ref[...] = u[tm - 8:tm]
    z = (_dot(xn, win_ref[:, 0:d]) * y).astype(BF16)
    o_ref[0] = x + (1.0 + mod[2:3]) * _dot(z, wout_ref[...])


def _short_conv(x, mod, g, w_in, conv_w, w_out):
    bn, s, d = x.shape
    tm = MIX_TM
    return pl.pallas_call(
        _conv_kernel,
        grid=(bn, s // tm),
        in_specs=[
            pl.BlockSpec((1, tm, d), lambda b, i: (b, i, 0)),
            pl.BlockSpec((1, 3, d), lambda b, i: (b, 0, 0)),
            _resident((1, d)),
            _resident((d, 3 * d)),
            _resident(conv_w.shape),
            _resident((d, d)),
        ],
        out_specs=pl.BlockSpec((1, tm, d), lambda b, i: (b, i, 0)),
        out_shape=jax.ShapeDtypeStruct(x.shape, x.dtype),
        scratch_shapes=[pltpu.VMEM((8, d), F32)],
        compiler_params=_params(("arbitrary", "arbitrary"), 48),
        name="short_conv",
    )(x, mod, g, w_in, conv_w, w_out)


def _qkv_kernel(x_ref, mod_ref, g_ref, win_ref, gq_ref, gk_ref, bd_ref, pos_ref, invf_ref,
                q_ref, k_ref, v_ref):
    d = x_ref.shape[2]
    hw = HEAD_W
    xn = _prenorm(x_ref[0], g_ref[...], mod_ref[0]).astype(BF16)

    ang = pos_ref[0].astype(F32) * invf_ref[...]
    lane = lax.broadcasted_iota(jnp.int32, (1, hw), 1) & (B_DH - 1)
    cs, sn = jnp.cos(ang), jnp.sin(ang)
    ca = jnp.where(lane < B_ROT, cs, 1.0)
    cm = jnp.where(lane < B_ROT // 2, -sn, 0.0)
    cp = jnp.where((lane >= B_ROT // 2) & (lane < B_ROT), sn, 0.0)
    half = B_ROT // 2

    gw = 2 * hw
    for off, gain_ref, is_q in ((0, gq_ref, True), (d, gk_ref, False)):
        for j in range(d // gw):
            z = _dot(xn, win_ref[:, off + j * gw:off + (j + 1) * gw])
            ss = _dot((z * z).astype(BF16), bd_ref[...])
            zn = z * lax.rsqrt(ss * (1.0 / B_DH) + EPS) * gain_ref[:, j * gw:(j + 1) * gw]
            for t in range(gw // hw):
                zz = zn[:, t * hw:(t + 1) * hw]
                r = zz * ca + pltpu.roll(zz, hw - half, 1) * cm + pltpu.roll(zz, half, 1) * cp
                head = j * (gw // hw) + t
                if is_q:
                    q_ref[0, :, head * hw:(head + 1) * hw] = (r * B_DH ** -0.5).astype(BF16)
                else:
                    k_ref[0, head, 0] = r.T.astype(BF16)
    v_ref[0] = _dot(xn, win_ref[:, 2 * d:3 * d]).astype(BF16)


def _qkv(x, mod, g, w_in, qk_g, positions):
    bn, s, d = x.shape
    tm = MIX_TM
    reps = d // B_DH
    gq = jnp.tile(qk_g[0].astype(F32), reps)[None, :]
    gk = jnp.tile(qk_g[1].astype(F32), reps)[None, :]
    blk = jnp.arange(2 * HEAD_W) // B_DH
    bd = (blk[:, None] == blk[None, :]).astype(BF16)
    inv_freq = ROPE_THETA ** (-jnp.arange(0, B_ROT, 2, dtype=F32) / B_ROT)
    lane = jnp.arange(HEAD_W) % B_DH
    invf = jnp.where(lane < B_ROT, inv_freq[lane % (B_ROT // 2)], 0.0)[None, :]
    tok = pl.BlockSpec((1, tm, d), lambda b, i: (b, i, 0))
    out = jax.ShapeDtypeStruct((bn, s, d), BF16)
    nh = d // HEAD_W
    kt_spec = pl.BlockSpec((1, nh, 1, HEAD_W, tm), lambda b, i: (b, 0, i, 0, 0))
    kt_out = jax.ShapeDtypeStruct((bn, nh, s // tm, HEAD_W, tm), BF16)
    return pl.pallas_call(
        _qkv_kernel,
        grid=(bn, s // tm),
        in_specs=[
            tok,
            pl.BlockSpec((1, 3, d), lambda b, i: (b, 0, 0)),
            _resident((1, d)),
            _resident((d, 3 * d)),
            _resident((1, d)),
            _resident((1, d)),
            _resident((2 * HEAD_W, 2 * HEAD_W)),
            pl.BlockSpec((1, tm, 1), lambda b, i: (b, i, 0)),
            _resident((1, HEAD_W)),
        ],
        out_specs=[tok, kt_spec, tok],
        out_shape=[out, kt_out, out],
        compiler_params=_params(("arbitrary", "arbitrary"), 48),
        name="attn_qkv",
    )(x, mod, g, w_in, gq, gk, bd, positions.reshape(bn, s, 1), invf)


def _attn_kernel(q_ref, kt_ref, v_ref, lam_ref, sg_ref, o_ref, qs_ref, m_ref, acc_ref, *, lambda_init, rows):
    t = q_ref.shape[1]
    hw = HEAD_W
    i = pl.program_id(2)
    q = q_ref[0]
    lane = lax.broadcasted_iota(jnp.int32, (1, hw), 1)
    qs_ref[0] = jnp.where(lane < B_DH, q, jnp.zeros_like(q))
    qs_ref[1] = jnp.where(lane >= B_DH, q, jnp.zeros_like(q))
    ones = jnp.ones((t, hw), BF16)

    def unit(c, rc, kt, vx, first):
        rs = slice(rc * rows, (rc + 1) * rows)
        nk = kt.shape[1]
        sc = _dot(qs_ref[c, rs, :], kt)
        if first:
            r = rc * rows + lax.broadcasted_iota(jnp.int32, (rows, nk), 0)
            cc = lax.broadcasted_iota(jnp.int32, (rows, nk), 1)
            sc = jnp.where(r >= cc, sc, -jnp.inf)
        cols = [sc[:, a * hw:(a + 1) * hw] for a in range(nk // hw)]
        rmax = jnp.max(functools.reduce(jnp.maximum, cols), axis=-1, keepdims=True)
        if first:
            m_new = jnp.broadcast_to(rmax, (rows, hw))
        else:
            m_old = m_ref[c, rs, :]
            m_new = jnp.maximum(m_old, rmax)
        p = jnp.concatenate([jnp.exp(cl - m_new) for cl in cols], axis=1).astype(BF16)
        pv = _dot(p, vx)
        if first:
            acc_ref[c, rs, :] = pv
        else:
            alpha = jnp.exp(m_old - m_new)
            acc_ref[c, rs, :] = acc_ref[c, rs, :] * jnp.concatenate([alpha, alpha], axis=1) + pv
        m_ref[c, rs, :] = m_new

    ktd = kt_ref[0, 0, i]
    for rc in range(t // rows):
        nk = (rc + 1) * rows
        vx = jnp.concatenate([v_ref[0, pl.ds(i * t, nk), :], ones[:nk]], axis=1)
        for c in range(2):
            unit(c, rc, ktd[:, :nk], vx, True)

    def body(j, carry):
        kt = kt_ref[0, 0, j]
        vx = jnp.concatenate([v_ref[0, pl.ds(j * t, t), :], ones], axis=1)
        for rc in range(t // rows):
            for c in range(2):
                unit(c, rc, kt, vx, False)
        return carry

    lax.fori_loop(0, i, body, 0)

    lp = lam_ref[...]
    lam = (jnp.exp(jnp.sum(lp[0:1] * lp[1:2], axis=-1, keepdims=True))
           - jnp.exp(jnp.sum(lp[2:3] * lp[3:4], axis=-1, keepdims=True)) + lambda_init)
    a0, a1 = acc_ref[0], acc_ref[1]
    o = a0[:, :hw] / a0[:, hw:] - lam * (a1[:, :hw] / a1[:, hw:])
    ms = jnp.mean(o * o, axis=-1, keepdims=True)
    o_ref[0] = (o * lax.rsqrt(ms + EPS) * sg_ref[...] * (1.0 - lambda_init)).astype(o_ref.dtype)


def _attention(q, kt, v, lam_p, subln, lambda_init):
    bn, s, d = q.shape
    t = ATT_T
    nh = d // HEAD_W
    assert kt.shape == (bn, nh, s // t, HEAD_W, t)
    tile = pl.BlockSpec((1, t, HEAD_W), lambda b, h, i: (b, i, h))
    return pl.pallas_call(
        functools.partial(_attn_kernel, lambda_init=lambda_init, rows=ATT_R),
        grid=(bn, nh, s // t),
        in_specs=[
            tile,
            pl.BlockSpec((1, 1, s // t, HEAD_W, t), lambda b, h, i: (b, h, 0, 0, 0)),
            pl.BlockSpec((1, s, HEAD_W), lambda b, h, i: (b, 0, h)),
            _resident(lam_p.shape),
            _resident((1, HEAD_W)),
        ],
        out_specs=tile,
        out_shape=jax.ShapeDtypeStruct((bn, s, d), BF16),
        scratch_shapes=[pltpu.VMEM((2, t, HEAD_W), BF16), pltpu.VMEM((2, t, HEAD_W), F32),
                        pltpu.VMEM((2, t, 2 * HEAD_W), F32)],
        compiler_params=_params(("arbitrary", "arbitrary", "arbitrary"), 48),
        name="attn_core",
    )(q, kt, v, lam_p, subln)


def _outproj_kernel(a_ref, x_ref, mod_ref, w_ref, o_ref):
    o_ref[0] = x_ref[0] + (1.0 + mod_ref[0][2:3]) * _dot(a_ref[0], w_ref[...])


def _outproj(a, x, mod, w_out):
    bn, s, d = x.shape
    tm = MIX_TM
    tok = pl.BlockSpec((1, tm, d), lambda b, i: (b, i, 0))
    return pl.pallas_call(
        _outproj_kernel,
        grid=(bn, s // tm),
        in_specs=[tok, tok, pl.BlockSpec((1, 3, d), lambda b, i: (b, 0, 0)), _resident((d, d))],
        out_specs=tok,
        out_shape=jax.ShapeDtypeStruct(x.shape, x.dtype),
        compiler_params=_params(("arbitrary", "arbitrary"), 32),
        name="attn_out",
    )(a, x, mod, w_out)


def kernel(x, c, positions, ada_w, ada_b, norm_g, ffn_wi, ffn_wo, a_w_in, a_w_out, a_lb, a_onorm,
           b_w_in, b_w_out, b_qk_g, b_lam, b_subln, c_w_in, c_conv, c_w_out):
    bn, s, d = x.shape
    depth = ada_w.shape[0]
    mod = _adaln(c, ada_w, ada_b).reshape(depth, bn, N_SUB, 3, d)

    for l in range(depth):
        def sub(j):
            return mod[l, :, j], norm_g[l, j][None, :]

        m, g = sub(0)
        x = _ffn(x, m, g, *_ffn_weights(ffn_wi[l, 0], ffn_wo[l, 0], FFN_FC))
        m, g = sub(1)
        kind, idx = l % N_MIXERS, l // N_MIXERS
        if kind == 0:
            x = _hgrn2(x, m, g, a_w_in[idx].astype(BF16), a_lb.astype(F32), a_onorm[idx][None, :].astype(F32),
                       a_w_out[idx].astype(BF16), idx)
        elif kind == 1:
            lambda_init = 0.8 - 0.6 * math.exp(-0.3 * l)
            q, k, v = _qkv(x, m, g, b_w_in[idx].astype(BF16), b_qk_g[idx], positions)
            a = _attention(q, k, v, b_lam[idx].astype(F32), b_subln[idx][None, :].astype(F32), lambda_init)
            x = _outproj(a, x, m, b_w_out[idx].astype(BF16))
        else:
            x = _short_conv(x, m, g, c_w_in[idx].astype(BF16), c_conv[idx].astype(F32), c_w_out[idx].astype(BF16))
        m, g = sub(2)
        x = _ffn(x, m, g, *_ffn_weights(ffn_wi[l, 1], ffn_wo[l, 1], FFN_FC))
    return x
```

```python
import functools
import math

import jax
import jax.numpy as jnp
from jax import lax
from jax.experimental import pallas as pl
from jax.experimental.pallas import tpu as pltpu

F32 = jnp.float32
BF16 = jnp.bfloat16

EPS = 1e-6
N_SUB = 3
N_MIXERS = 3
HEAD_W = 128
N_HEADS = 8
B_DH = 64
B_ROT = 16
ROPE_THETA = 500000.0
MIB = 1024 * 1024

FFN_TM = 512
FFN_FC = 256
MIX_TM = 512
A_CHUNK = 256
A_SUB = 128
A_LOCAL = 32
A_SAFE = 60.0
ATT_T = MIX_TM
ATT_R = 256


def _silu(z):
    return z * jax.nn.sigmoid(z)


def _dot(a, b):
    return jnp.dot(a, b, preferred_element_type=F32)


def _dot_nt(a, b):
    return lax.dot_general(a, b, (((1,), (1,)), ((), ())), preferred_element_type=F32)


def _dot_tn(a, b):
    return lax.dot_general(a, b, (((0,), (0,)), ((), ())), preferred_element_type=F32)


def _prenorm(x, g, mod):
    ms = jnp.mean(x * x, axis=-1, keepdims=True)
    y = x * lax.rsqrt(ms + EPS) * g
    return y * (1.0 + mod[1:2]) + mod[0:1]


def _resident(shape):
    nd = len(shape)
    return pl.BlockSpec(shape, lambda *_: (0,) * nd, pipeline_mode=pl.Buffered(1))


def _params(sem, vmem_mib):
    return pltpu.CompilerParams(dimension_semantics=sem, vmem_limit_bytes=vmem_mib * MIB)


def _adaln_kernel(c_ref, w_ref, b_ref, o_ref):
    ca = _silu(c_ref[...]).astype(BF16)
    o_ref[0] = _dot(ca, w_ref[0].astype(BF16)) + b_ref[0]


def _adaln(c, ada_w, ada_b):
    depth, d, n = ada_w.shape
    bn = c.shape[0]
    rows = 8 * pl.cdiv(bn, 8)
    cp = jnp.zeros((rows, d), F32).at[:bn].set(c)
    tn = n // 6
    out = pl.pallas_call(
        _adaln_kernel,
        grid=(depth, n // tn),
        in_specs=[
            pl.BlockSpec((rows, d), lambda l, j: (0, 0)),
            pl.BlockSpec((1, d, tn), lambda l, j: (l, 0, j)),
            pl.BlockSpec((1, 1, tn), lambda l, j: (l, 0, j)),
        ],
        out_specs=pl.BlockSpec((1, rows, tn), lambda l, j: (l, 0, j)),
        out_shape=jax.ShapeDtypeStruct((depth, rows, n), F32),
        compiler_params=_params(("arbitrary", "arbitrary"), 40),
        name="adaln",
    )(cp, ada_w, ada_b.reshape(depth, 1, n))
    return out[:, :bn]


def _ffn_kernel(x_ref, mod_ref, g_ref, wi_ref, wo_ref, o_ref, h_ref, *, fc):
    x = x_ref[0]
    mod = mod_ref[0]
    xn = _prenorm(x, g_ref[...], mod).astype(BF16)
    ff = h_ref.shape[1]
    for c in range(ff // fc):
        gate = _dot(xn, wi_ref[:, c * fc:(c + 1) * fc])
        up = _dot(xn, wi_ref[:, ff + c * fc:ff + (c + 1) * fc])
        h_ref[:, c * fc:(c + 1) * fc] = (_silu(gate) * up).astype(BF16)
    y = _dot(h_ref[...], wo_ref[...])
    o_ref[0] = x + (0.5 * (1.0 + mod[2:3])) * y


def _ffn(x, mod, g, wi, wo):
    bn, s, d = x.shape
    ff = wo.shape[0]
    tm, fc = FFN_TM, FFN_FC
    return pl.pallas_call(
        functools.partial(_ffn_kernel, fc=fc),
        grid=(bn, s // tm),
        in_specs=[
            pl.BlockSpec((1, tm, d), lambda b, i: (b, i, 0)),
            pl.BlockSpec((1, 3, d), lambda b, i: (b, 0, 0)),
            _resident((1, d)),
            _resident((d, 2 * ff)),
            _resident((ff, d)),
        ],
        out_specs=pl.BlockSpec((1, tm, d), lambda b, i: (b, i, 0)),
        out_shape=jax.ShapeDtypeStruct(x.shape, x.dtype),
        scratch_shapes=[pltpu.VMEM((tm, ff), BF16)],
        compiler_params=_params(("arbitrary", "arbitrary"), 48),
        name="ffn",
    )(x, mod, g, wi, wo)


def _block_masks(n, local):
    row = lax.broadcasted_iota(jnp.int32, (n, n), 0)
    col = lax.broadcasted_iota(jnp.int32, (n, n), 1)
    sh = int(math.log2(local))
    masks = {"diag": row == col, "local": ((row >> sh) == (col >> sh)) & (row >= col)}
    w = n // 2
    while w >= 1:
        sh = int(math.log2(w))
        rb, cb = row >> sh, col >> sh
        masks[w] = (rb == cb + 1) & ((cb & 1) == 0)
        w //= 2
    return masks


def _block_scores(q, k, lf, b, masks, local):
    n, hw = q.shape
    if local:
        b3 = b.reshape(n // local, local, hw)
        first = b3[:, 0:1, :]
        qe = (q * jnp.exp(b3 - first).reshape(n, hw)).astype(BF16)
        ke = (k * jnp.exp(first - b3).reshape(n, hw)).astype(BF16)
        att = jnp.where(masks["local"], _dot_nt(qe, ke), 0.0)
    else:
        att = jnp.where(masks["diag"], _dot_nt(q.astype(BF16), k.astype(BF16)), 0.0)
    w = n // 2
    while w >= (local or 8):
        b3 = b.reshape(n // w, w, hw)
        last = b3[:, w - 1:w, :]
        prev = jnp.concatenate([last[:1], last[:-1]], axis=0)
        qa = jnp.minimum(b3 - prev, 0.0).reshape(n, hw)
        ka = (last - b3).reshape(n, hw)
        sc = _dot_nt((q * jnp.exp(qa)).astype(BF16), (k * jnp.exp(ka)).astype(BF16))
        att = jnp.where(masks[w], sc, att)
        w //= 2
    if local:
        return att
    t = lax.broadcasted_iota(jnp.int32, (n, 1), 0)
    back = [None] + [pltpu.roll(lf, i, 0) for i in range(1, 4)]
    fwd = [None] + [pltpu.roll(lf, n - i, 0) for i in range(1, 4)]
    while w >= 1:
        r = t & (w - 1)
        qa = lf
        ke = k
        if w > 1:
            ka = jnp.where(r <= w - 2, fwd[1], 0.0)
            for i in range(1, w):
                qa = qa + jnp.where(r >= i, back[i], 0.0)
            for i in range(2, w):
                ka = ka + jnp.where(r <= w - 1 - i, fwd[i], 0.0)
            ke = k * jnp.exp(ka)
        sc = _dot_nt((q * jnp.exp(qa)).astype(BF16), ke.astype(BF16))
        att = jnp.where(masks[w], sc, att)
        w //= 2
    return att


def _hgrn2_kernel(x_ref, mod_ref, g_ref, win_ref, alb_ref, on_ref, wout_ref, o_ref,
                  st_ref, q_ref, k_ref, lf_ref, b_ref, v_ref, gt_ref, oall_ref, *, idx, sub):
    chunk, d = x_ref.shape[1], x_ref.shape[2]
    hw = HEAD_W

    @pl.when(pl.program_id(1) == 0)
    def _():
        st_ref[...] = jnp.zeros_like(st_ref)

    x = x_ref[0]
    mod = mod_ref[0]
    xn = _prenorm(x, g_ref[...], mod).astype(BF16)

    alb = alb_ref[...]
    e = jnp.exp(alb - jnp.max(alb, axis=0, keepdims=True))
    sm = e / jnp.sum(e, axis=0, keepdims=True)
    lb = jnp.sum(sm[0:idx + 1], axis=0, keepdims=True) - sm[0:1]

    q_ref[...] = _silu(_dot(xn, win_ref[:, 0:d]))
    f = lb + (1.0 - lb) * jax.nn.sigmoid(_dot(xn, win_ref[:, d:2 * d]))
    k_ref[...] = 1.0 - f
    v_ref[...] = _dot(xn, win_ref[:, 2 * d:3 * d])
    gt_ref[...] = _silu(_dot(xn, win_ref[:, 3 * d:4 * d]))

    lf = jnp.log(f)
    lf_ref[...] = lf
    hi = lf.astype(BF16)
    r1 = lf - hi.astype(F32)
    mid = r1.astype(BF16)
    lo = (r1 - mid.astype(F32)).astype(BF16)
    row = lax.broadcasted_iota(jnp.int32, (chunk, chunk), 0)
    col = lax.broadcasted_iota(jnp.int32, (chunk, chunk), 1)
    tri = (row >= col).astype(BF16)
    b_ref[...] = _dot(tri, hi) + _dot(tri, mid) + _dot(tri, lo)

    masks = _block_masks(sub, A_LOCAL)
    nsub = chunk // sub

    def heads(local):
        staged = []
        for h in range(d // hw):
            hs = slice(h * hw, (h + 1) * hw)
            q, k, lf, b = q_ref[:, hs], k_ref[:, hs], lf_ref[:, hs], b_ref[:, hs]
            inter = _dot_nt((q * jnp.exp(b)).astype(BF16), st_ref[h].astype(BF16))
            atts, crosses = [], []
            for i in range(nsub):
                rs = slice(i * sub, (i + 1) * sub)
                atts.append(_block_scores(q[rs], k[rs], lf[rs], b[rs], masks, local).astype(BF16))
                if i > 0:
                    edge = b[i * sub - 1:i * sub, :]
                    qc = (q[rs] * jnp.exp(b[rs] - edge)).astype(BF16)
                    for j in range(i):
                        js = slice(j * sub, (j + 1) * sub)
                        kc = (k[js] * jnp.exp(edge - b[js])).astype(BF16)
                        crosses.append((i, j, _dot_nt(qc, kc).astype(BF16)))
            staged.append((inter, atts, crosses))
        for h in range(d // hw):
            hs = slice(h * hw, (h + 1) * hw)
            inter, atts, crosses = staged[h]
            k, b = k_ref[:, hs], b_ref[:, hs]
            vb = v_ref[:, hs].astype(BF16)
            outs = [inter[i * sub:(i + 1) * sub] + _dot(atts[i], vb[i * sub:(i + 1) * sub]) for i in range(nsub)]
            for i, j, cs in crosses:
                outs[i] = outs[i] + _dot(cs, vb[j * sub:(j + 1) * sub])
            o = jnp.concatenate(outs, axis=0)

            bl = b[chunk - 1:chunk, :]
            kt = k * jnp.exp(bl - b)
            st_ref[h] = st_ref[h] * jnp.exp(bl) + _dot_tn(vb, kt.astype(BF16))

            ms = jnp.mean(o * o, axis=-1, keepdims=True)
            on = o * lax.rsqrt(ms + EPS) * on_ref[...]
            oall_ref[:, hs] = (on * gt_ref[:, hs]).astype(BF16)

    b3 = b_ref[...].reshape(chunk // A_LOCAL, A_LOCAL, d)
    safe = jnp.max(b3[:, 0:1, :] - b3[:, A_LOCAL - 1:A_LOCAL, :]) <= A_SAFE

    @pl.when(safe)
    def _():
        heads(A_LOCAL)

    @pl.when(jnp.logical_not(safe))
    def _():
        heads(None)

    y = _dot(oall_ref[...], wout_ref[...])
    o_ref[0] = x + (1.0 + mod[2:3]) * y


def _hgrn2(x, mod, g, w_in, a_lb, onorm, w_out, idx):
    bn, s, d = x.shape
    chunk = A_CHUNK
    nh = d // HEAD_W
    act = pltpu.VMEM((chunk, d), F32)
    return pl.pallas_call(
        functools.partial(_hgrn2_kernel, idx=idx, sub=A_SUB),
        grid=(bn, s // chunk),
        in_specs=[
            pl.BlockSpec((1, chunk, d), lambda b, i: (b, i, 0)),
            pl.BlockSpec((1, 3, d), lambda b, i: (b, 0, 0)),
            _resident((1, d)),
            _resident((d, 4 * d)),
            _resident(a_lb.shape),
            _resident((1, HEAD_W)),
            _resident((d, d)),
        ],
        out_specs=pl.BlockSpec((1, chunk, d), lambda b, i: (b, i, 0)),
        out_shape=jax.ShapeDtypeStruct(x.shape, x.dtype),
        scratch_shapes=[pltpu.VMEM((nh, HEAD_W, HEAD_W), F32), act, act, act, act, act, act,
                        pltpu.VMEM((chunk, d), BF16)],
        compiler_params=_params(("arbitrary", "arbitrary"), 48),
        name="hgrn2",
    )(x, mod, g, w_in, a_lb, onorm, w_out)


def _conv_kernel(x_ref, mod_ref, g_ref, win_ref, cw_ref, wout_ref, o_ref, carry_ref):
    tm, d = x_ref.shape[1], x_ref.shape[2]

    @pl.when(pl.program_id(1) == 0)
    def _():
        carry_ref[...] = jnp.zeros_like(carry_ref)

    x = x_ref[0]
    mod = mod_ref[0]
    xn = _prenorm(x, g_ref[...], mod).astype(BF16)
    u = _dot(xn, win_ref[:, d:2 * d]) * _dot(xn, win_ref[:, 2 * d:3 * d])
    carry = carry_ref[...]
    row = lax.broadcasted_iota(jnp.int32, (tm, 1), 0)
    u1 = jnp.where(row == 0, carry[7:8], pltpu.roll(u, 1, 0))
    u2 = jnp.where(row == 0, carry[6:7], jnp.where(row == 1, carry[7:8], pltpu.roll(u, 2, 0)))
    cw = cw_ref[...]
    y = cw[0:1] * u2 + cw[1:2] * u1 + cw[2:3] * u
    carry_ref[...] = u[tm - 8:tm]
    z = (_dot(xn, win_ref[:, 0:d]) * y).astype(BF16)
    o_ref[0] = x + (1.0 + mod[2:3]) * _dot(z, wout_ref[...])


def _short_conv(x, mod, g, w_in, conv_w, w_out):
    bn, s, d = x.shape
    tm = MIX_TM
    return pl.pallas_call(
        _conv_kernel,
        grid=(bn, s // tm),
        in_specs=[
            pl.BlockSpec((1, tm, d), lambda b, i: (b, i, 0)),
            pl.BlockSpec((1, 3, d), lambda b, i: (b, 0, 0)),
            _resident((1, d)),
            _resident((d, 3 * d)),
            _resident(conv_w.shape),
            _resident((d, d)),
        ],
        out_specs=pl.BlockSpec((1, tm, d), lambda b, i: (b, i, 0)),
        out_shape=jax.ShapeDtypeStruct(x.shape, x.dtype),
        scratch_shapes=[pltpu.VMEM((8, d), F32)],
        compiler_params=_params(("arbitrary", "arbitrary"), 48),
        name="short_conv",
    )(x, mod, g, w_in, conv_w, w_out)


def _qkv_kernel(x_ref, mod_ref, g_ref, win_ref, gq_ref, gk_ref, bd_ref, pos_ref, invf_ref,
                q_ref, k_ref, v_ref):
    d = x_ref.shape[2]
    hw = HEAD_W
    xn = _prenorm(x_ref[0], g_ref[...], mod_ref[0]).astype(BF16)

    ang = pos_ref[0].astype(F32) * invf_ref[...]
    lane = lax.broadcasted_iota(jnp.int32, (1, hw), 1) & (B_DH - 1)
    cs, sn = jnp.cos(ang), jnp.sin(ang)
    ca = jnp.where(lane < B_ROT, cs, 1.0)
    cm = jnp.where(lane < B_ROT // 2, -sn, 0.0)
    cp = jnp.where((lane >= B_ROT // 2) & (lane < B_ROT), sn, 0.0)
    half = B_ROT // 2

    gw = 2 * hw
    for off, gain_ref, is_q in ((0, gq_ref, True), (d, gk_ref, False)):
        for j in range(d // gw):
            z = _dot(xn, win_ref[:, off + j * gw:off + (j + 1) * gw])
            ss = _dot((z * z).astype(BF16), bd_ref[...])
            zn = z * lax.rsqrt(ss * (1.0 / B_DH) + EPS) * gain_ref[:, j * gw:(j + 1) * gw]
            for t in range(gw // hw):
                zz = zn[:, t * hw:(t + 1) * hw]
                r = zz * ca + pltpu.roll(zz, hw - half, 1) * cm + pltpu.roll(zz, half, 1) * cp
                head = j * (gw // hw) + t
                if is_q:
                    q_ref[0, :, head * hw:(head + 1) * hw] = (r * B_DH ** -0.5).astype(BF16)
                else:
                    k_ref[0, head, 0] = r.T.astype(BF16)
    v_ref[0] = _dot(xn, win_ref[:, 2 * d:3 * d]).astype(BF16)


def _qkv(x, mod, g, w_in, qk_g, positions):
    bn, s, d = x.shape
    tm = MIX_TM
    reps = d // B_DH
    gq = jnp.tile(qk_g[0].astype(F32), reps)[None, :]
    gk = jnp.tile(qk_g[1].astype(F32), reps)[None, :]
    blk = jnp.arange(2 * HEAD_W) // B_DH
    bd = (blk[:, None] == blk[None, :]).astype(BF16)
    inv_freq = ROPE_THETA ** (-jnp.arange(0, B_ROT, 2, dtype=F32) / B_ROT)
    lane = jnp.arange(HEAD_W) % B_DH
    invf = jnp.where(lane < B_ROT, inv_freq[lane % (B_ROT // 2)], 0.0)[None, :]
    tok = pl.BlockSpec((1, tm, d), lambda b, i: (b, i, 0))
    out = jax.ShapeDtypeStruct((bn, s, d), BF16)
    nh = d // HEAD_W
    kt_spec = pl.BlockSpec((1, nh, 1, HEAD_W, tm), lambda b, i: (b, 0, i, 0, 0))
    kt_out = jax.ShapeDtypeStruct((bn, nh, s // tm, HEAD_W, tm), BF16)
    return pl.pallas_call(
        _qkv_kernel,
        grid=(bn, s // tm),
        in_specs=[
            tok,
            pl.BlockSpec((1, 3, d), lambda b, i: (b, 0, 0)),
            _resident((1, d)),
            _resident((d, 3 * d)),
            _resident((1, d)),
            _resident((1, d)),
            _resident((2 * HEAD_W, 2 * HEAD_W)),
            pl.BlockSpec((1, tm, 1), lambda b, i: (b, i, 0)),
            _resident((1, HEAD_W)),
        ],
        out_specs=[tok, kt_spec, tok],
        out_shape=[out, kt_out, out],
        compiler_params=_params(("arbitrary", "arbitrary"), 48),
        name="attn_qkv",
    )(x, mod, g, w_in, gq, gk, bd, positions.reshape(bn, s, 1), invf)


def _attn_kernel(q_ref, kt_ref, v_ref, lam_ref, sg_ref, o_ref, qs_ref, s_ref, m_ref, acc_ref, *,
                 lambda_init, rows):
    t = q_ref.shape[1]
    hw = HEAD_W
    i = pl.program_id(2)
    q = q_ref[0]
    lane = lax.broadcasted_iota(jnp.int32, (1, hw), 1)
    qs_ref[0] = jnp.where(lane < B_DH, q, jnp.zeros_like(q))
    qs_ref[1] = jnp.where(lane >= B_DH, q, jnp.zeros_like(q))
    m_ref[...] = jnp.full_like(m_ref, -jnp.inf)
    acc_ref[...] = jnp.zeros_like(acc_ref)
    ones = jnp.ones((t, hw), BF16)
    groups = [(c, rc) for rc in range(t // rows) for c in range(2)]

    def scores_to(slot, j):
        kt = kt_ref[0, 0, j]
        for c, rc in groups:
            rs = slice(rc * rows, (rc + 1) * rows)
            s_ref[slot, c, rs, :] = _dot(qs_ref[c, rs, :], kt)

    def softmax_pv(slot, j, causal):
        for c, rc in groups:
            rs = slice(rc * rows, (rc + 1) * rows)
            nk = hw * pl.cdiv((rc + 1) * rows, hw) if causal else t
            sc = s_ref[slot, c, rs, :nk]
            if causal:
                r = rc * rows + lax.broadcasted_iota(jnp.int32, (rows, nk), 0)
                cc = lax.broadcasted_iota(jnp.int32, (rows, nk), 1)
                sc = jnp.where(r >= cc, sc, -jnp.inf)
            cols = [sc[:, a * hw:(a + 1) * hw] for a in range(nk // hw)]
            rmax = jnp.max(functools.reduce(jnp.maximum, cols), axis=-1, keepdims=True)
            m_old = m_ref[c, rs, :]
            m_new = jnp.maximum(m_old, rmax)
            p = jnp.concatenate([jnp.exp(cl - m_new) for cl in cols], axis=1).astype(BF16)
            vx = jnp.concatenate([v_ref[0, pl.ds(j * t, nk), :], ones[:nk]], axis=1)
            alpha = jnp.exp(m_old - m_new)
            acc_ref[c, rs, :] = acc_ref[c, rs, :] * jnp.concatenate([alpha, alpha], axis=1) + _dot(p, vx)
            m_ref[c, rs, :] = m_new

    scores_to(0, 0)

    def pair(p, carry):
        scores_to(1, 2 * p + 1)
        softmax_pv(0, 2 * p, False)
        scores_to(0, 2 * p + 2)
        softmax_pv(1, 2 * p + 1, False)
        return carry

    lax.fori_loop(0, i // 2, pair, 0)

    @pl.when(i % 2 == 1)
    def _():
        scores_to(1, i)
        softmax_pv(0, i - 1, False)

    softmax_pv(i % 2, i, True)

    lp = lam_ref[...]
    lam = (jnp.exp(jnp.sum(lp[0:1] * lp[1:2], axis=-1, keepdims=True))
           - jnp.exp(jnp.sum(lp[2:3] * lp[3:4], axis=-1, keepdims=True)) + lambda_init)
    a0, a1 = acc_ref[0], acc_ref[1]
    o = a0[:, :hw] / a0[:, hw:] - lam * (a1[:, :hw] / a1[:, hw:])
    ms = jnp.mean(o * o, axis=-1, keepdims=True)
    o_ref[0] = (o * lax.rsqrt(ms + EPS) * sg_ref[...] * (1.0 - lambda_init)).astype(o_ref.dtype)


def _attention(q, kt, v, lam_p, subln, lambda_init):
    bn, s, d = q.shape
    t = ATT_T
    nh = d // HEAD_W
    assert kt.shape == (bn, nh, s // t, HEAD_W, t)
    tile = pl.BlockSpec((1, t, HEAD_W), lambda b, h, i: (b, i, h))
    return pl.pallas_call(
        functools.partial(_attn_kernel, lambda_init=lambda_init, rows=ATT_R),
        grid=(bn, nh, s // t),
        in_specs=[
            tile,
            pl.BlockSpec((1, 1, s // t, HEAD_W, t), lambda b, h, i: (b, h, 0, 0, 0)),
            pl.BlockSpec((1, s, HEAD_W), lambda b, h, i: (b, 0, h)),
            _resident(lam_p.shape),
            _resident((1, HEAD_W)),
        ],
        out_specs=tile,
        out_shape=jax.ShapeDtypeStruct((bn, s, d), BF16),
        scratch_shapes=[pltpu.VMEM((2, t, HEAD_W), BF16), pltpu.VMEM((2, 2, t, t), F32),
                        pltpu.VMEM((2, t, HEAD_W), F32),
                        pltpu.VMEM((2, t, 2 * HEAD_W), F32)],
        compiler_params=_params(("arbitrary", "arbitrary", "arbitrary"), 48),
        name="attn_core",
    )(q, kt, v, lam_p, subln)


def _outproj_kernel(a_ref, x_ref, mod_ref, w_ref, o_ref):
    o_ref[0] = x_ref[0] + (1.0 + mod_ref[0][2:3]) * _dot(a_ref[0], w_ref[...])


def _outproj(a, x, mod, w_out):
    bn, s, d = x.shape
    tm = MIX_TM
    tok = pl.BlockSpec((1, tm, d), lambda b, i: (b, i, 0))
    return pl.pallas_call(
        _outproj_kernel,
        grid=(bn, s // tm),
        in_specs=[tok, tok, pl.BlockSpec((1, 3, d), lambda b, i: (b, 0, 0)), _resident((d, d))],
        out_specs=tok,
        out_shape=jax.ShapeDtypeStruct(x.shape, x.dtype),
        compiler_params=_params(("arbitrary", "arbitrary"), 32),
        name="attn_out",
    )(a, x, mod, w_out)


def kernel(x, c, positions, ada_w, ada_b, norm_g, ffn_wi, ffn_wo, a_w_in, a_w_out, a_lb, a_onorm,
           b_w_in, b_w_out, b_qk_g, b_lam, b_subln, c_w_in, c_conv, c_w_out):
    bn, s, d = x.shape
    depth = ada_w.shape[0]
    mod = _adaln(c, ada_w, ada_b).reshape(depth, bn, N_SUB, 3, d)

    for l in range(depth):
        def sub(j):
            return mod[l, :, j], norm_g[l, j][None, :]

        m, g = sub(0)
        x = _ffn(x, m, g, ffn_wi[l, 0].astype(BF16), ffn_wo[l, 0].astype(BF16))
        m, g = sub(1)
        kind, idx = l % N_MIXERS, l // N_MIXERS
        if kind == 0:
            x = _hgrn2(x, m, g, a_w_in[idx].astype(BF16), a_lb.astype(F32), a_onorm[idx][None, :].astype(F32),
                       a_w_out[idx].astype(BF16), idx)
        elif kind == 1:
            lambda_init = 0.8 - 0.6 * math.exp(-0.3 * l)
            q, k, v = _qkv(x, m, g, b_w_in[idx].astype(BF16), b_qk_g[idx], positions)
            a = _attention(q, k, v, b_lam[idx].astype(F32), b_subln[idx][None, :].astype(F32), lambda_init)
            x = _outproj(a, x, m, b_w_out[idx].astype(BF16))
        else:
            x = _short_conv(x, m, g, c_w_in[idx].astype(BF16), c_conv[idx].astype(F32), c_w_out[idx].astype(BF16))
        m, g = sub(2)
        x = _ffn(x, m, g, ffn_wi[l, 1].astype(BF16), ffn_wo[l, 1].astype(BF16))
    return x
```

```python
import functools
import math

import jax
import jax.numpy as jnp
from jax import lax
from jax.experimental import pallas as pl
from jax.experimental.pallas import tpu as pltpu

F32 = jnp.float32
BF16 = jnp.bfloat16

EPS = 1e-6
N_SUB = 3
N_MIXERS = 3
HEAD_W = 128
N_HEADS = 8
B_DH = 64
B_ROT = 16
ROPE_THETA = 500000.0
MIB = 1024 * 1024

FFN_TM = 512
FFN_FC = 256
MIX_TM = 512
A_CHUNK = 256
A_SUB = 128
A_LOCAL = 32
A_SAFE = 60.0
ATT_T = 1024
ATT_R = 256


def _silu(z):
    return z * jax.nn.sigmoid(z)


def _dot(a, b):
    return jnp.dot(a, b, preferred_element_type=F32)


def _dot_nt(a, b):
    return lax.dot_general(a, b, (((1,), (1,)), ((), ())), preferred_element_type=F32)


def _dot_tn(a, b):
    return lax.dot_general(a, b, (((0,), (0,)), ((), ())), preferred_element_type=F32)


def _prenorm(x, g, mod):
    ms = jnp.mean(x * x, axis=-1, keepdims=True)
    y = x * lax.rsqrt(ms + EPS) * g
    return y * (1.0 + mod[1:2]) + mod[0:1]


def _resident(shape):
    nd = len(shape)
    return pl.BlockSpec(shape, lambda *_: (0,) * nd, pipeline_mode=pl.Buffered(1))


def _params(sem, vmem_mib):
    return pltpu.CompilerParams(dimension_semantics=sem, vmem_limit_bytes=vmem_mib * MIB)


def _adaln_kernel(c_ref, w_ref, b_ref, o_ref):
    ca = _silu(c_ref[...]).astype(BF16)
    o_ref[0] = _dot(ca, w_ref[0].astype(BF16)) + b_ref[0]


def _adaln(c, ada_w, ada_b):
    depth, d, n = ada_w.shape
    bn = c.shape[0]
    rows = 8 * pl.cdiv(bn, 8)
    cp = jnp.zeros((rows, d), F32).at[:bn].set(c)
    tn = n // 6
    out = pl.pallas_call(
        _adaln_kernel,
        grid=(depth, n // tn),
        in_specs=[
            pl.BlockSpec((rows, d), lambda l, j: (0, 0)),
            pl.BlockSpec((1, d, tn), lambda l, j: (l, 0, j)),
            pl.BlockSpec((1, 1, tn), lambda l, j: (l, 0, j)),
        ],
        out_specs=pl.BlockSpec((1, rows, tn), lambda l, j: (l, 0, j)),
        out_shape=jax.ShapeDtypeStruct((depth, rows, n), F32),
        compiler_params=_params(("arbitrary", "arbitrary"), 40),
        name="adaln",
    )(cp, ada_w, ada_b.reshape(depth, 1, n))
    return out[:, :bn]


def _ffn_kernel(x_ref, mod_ref, g_ref, wi_ref, wo_ref, o_ref, h_ref, *, fc):
    x = x_ref[0]
    mod = mod_ref[0]
    xn = _prenorm(x, g_ref[...], mod).astype(BF16)
    ff = h_ref.shape[1]
    for c in range(ff // fc):
        gate = _dot(xn, wi_ref[:, c * fc:(c + 1) * fc])
        up = _dot(xn, wi_ref[:, ff + c * fc:ff + (c + 1) * fc])
        h_ref[:, c * fc:(c + 1) * fc] = (_silu(gate) * up).astype(BF16)
    y = _dot(h_ref[...], wo_ref[...])
    o_ref[0] = x + (0.5 * (1.0 + mod[2:3])) * y


def _ffn(x, mod, g, wi, wo):
    bn, s, d = x.shape
    ff = wo.shape[0]
    tm, fc = FFN_TM, FFN_FC
    return pl.pallas_call(
        functools.partial(_ffn_kernel, fc=fc),
        grid=(bn, s // tm),
        in_specs=[
            pl.BlockSpec((1, tm, d), lambda b, i: (b, i, 0)),
            pl.BlockSpec((1, 3, d), lambda b, i: (b, 0, 0)),
            _resident((1, d)),
            _resident((d, 2 * ff)),
            _resident((ff, d)),
        ],
        out_specs=pl.BlockSpec((1, tm, d), lambda b, i: (b, i, 0)),
        out_shape=jax.ShapeDtypeStruct(x.shape, x.dtype),
        scratch_shapes=[pltpu.VMEM((tm, ff), BF16)],
        compiler_params=_params(("arbitrary", "arbitrary"), 48),
        name="ffn",
    )(x, mod, g, wi, wo)


def _block_masks(n, local):
    row = lax.broadcasted_iota(jnp.int32, (n, n), 0)
    col = lax.broadcasted_iota(jnp.int32, (n, n), 1)
    sh = int(math.log2(local))
    masks = {"diag": row == col, "local": ((row >> sh) == (col >> sh)) & (row >= col)}
    w = n // 2
    while w >= 1:
        sh = int(math.log2(w))
        rb, cb = row >> sh, col >> sh
        masks[w] = (rb == cb + 1) & ((cb & 1) == 0)
        w //= 2
    return masks


def _block_scores(q, k, lf, b, masks, local):
    n, hw = q.shape
    if local:
        b3 = b.reshape(n // local, local, hw)
        first = b3[:, 0:1, :]
        qe = (q * jnp.exp(b3 - first).reshape(n, hw)).astype(BF16)
        ke = (k * jnp.exp(first - b3).reshape(n, hw)).astype(BF16)
        att = jnp.where(masks["local"], _dot_nt(qe, ke), 0.0)
    else:
        att = jnp.where(masks["diag"], _dot_nt(q.astype(BF16), k.astype(BF16)), 0.0)
    w = n // 2
    while w >= (local or 8):
        b3 = b.reshape(n // w, w, hw)
        last = b3[:, w - 1:w, :]
        prev = jnp.concatenate([last[:1], last[:-1]], axis=0)
        qa = jnp.minimum(b3 - prev, 0.0).reshape(n, hw)
        ka = (last - b3).reshape(n, hw)
        sc = _dot_nt((q * jnp.exp(qa)).astype(BF16), (k * jnp.exp(ka)).astype(BF16))
        att = jnp.where(masks[w], sc, att)
        w //= 2
    if local:
        return att
    t = lax.broadcasted_iota(jnp.int32, (n, 1), 0)
    back = [None] + [pltpu.roll(lf, i, 0) for i in range(1, 4)]
    fwd = [None] + [pltpu.roll(lf, n - i, 0) for i in range(1, 4)]
    while w >= 1:
        r = t & (w - 1)
        qa = lf
        ke = k
        if w > 1:
            ka = jnp.where(r <= w - 2, fwd[1], 0.0)
            for i in range(1, w):
                qa = qa + jnp.where(r >= i, back[i], 0.0)
            for i in range(2, w):
                ka = ka + jnp.where(r <= w - 1 - i, fwd[i], 0.0)
            ke = k * jnp.exp(ka)
        sc = _dot_nt((q * jnp.exp(qa)).astype(BF16), ke.astype(BF16))
        att = jnp.where(masks[w], sc, att)
        w //= 2
    return att


def _hgrn2_kernel(x_ref, mod_ref, g_ref, win_ref, alb_ref, on_ref, wout_ref, o_ref,
                  st_ref, q_ref, k_ref, lf_ref, b_ref, v_ref, gt_ref, oall_ref, *, idx, sub):
    chunk, d = x_ref.shape[1], x_ref.shape[2]
    hw = HEAD_W

    @pl.when(pl.program_id(1) == 0)
    def _():
        st_ref[...] = jnp.zeros_like(st_ref)

    x = x_ref[0]
    mod = mod_ref[0]
    xn = _prenorm(x, g_ref[...], mod).astype(BF16)

    alb = alb_ref[...]
    e = jnp.exp(alb - jnp.max(alb, axis=0, keepdims=True))
    sm = e / jnp.sum(e, axis=0, keepdims=True)
    lb = jnp.sum(sm[0:idx + 1], axis=0, keepdims=True) - sm[0:1]

    q_ref[...] = _silu(_dot(xn, win_ref[:, 0:d]))
    f = lb + (1.0 - lb) * jax.nn.sigmoid(_dot(xn, win_ref[:, d:2 * d]))
    k_ref[...] = 1.0 - f
    v_ref[...] = _dot(xn, win_ref[:, 2 * d:3 * d])
    gt_ref[...] = _silu(_dot(xn, win_ref[:, 3 * d:4 * d]))

    lf = jnp.log(f)
    lf_ref[...] = lf
    hi = lf.astype(BF16)
    r1 = lf - hi.astype(F32)
    mid = r1.astype(BF16)
    lo = (r1 - mid.astype(F32)).astype(BF16)
    row = lax.broadcasted_iota(jnp.int32, (chunk, chunk), 0)
    col = lax.broadcasted_iota(jnp.int32, (chunk, chunk), 1)
    tri = (row >= col).astype(BF16)
    b_ref[...] = _dot(tri, hi) + _dot(tri, mid) + _dot(tri, lo)

    masks = _block_masks(sub, A_LOCAL)
    nsub = chunk // sub

    def heads(local):
        staged = []
        for h in range(d // hw):
            hs = slice(h * hw, (h + 1) * hw)
            q, k, lf, b = q_ref[:, hs], k_ref[:, hs], lf_ref[:, hs], b_ref[:, hs]
            inter = _dot_nt((q * jnp.exp(b)).astype(BF16), st_ref[h].astype(BF16))
            atts, crosses = [], []
            for i in range(nsub):
                rs = slice(i * sub, (i + 1) * sub)
                atts.append(_block_scores(q[rs], k[rs], lf[rs], b[rs], masks, local).astype(BF16))
                if i > 0:
                    edge = b[i * sub - 1:i * sub, :]
                    qc = (q[rs] * jnp.exp(b[rs] - edge)).astype(BF16)
                    for j in range(i):
                        js = slice(j * sub, (j + 1) * sub)
                        kc = (k[js] * jnp.exp(edge - b[js])).astype(BF16)
                        crosses.append((i, j, _dot_nt(qc, kc).astype(BF16)))
            staged.append((inter, atts, crosses))
        for h in range(d // hw):
            hs = slice(h * hw, (h + 1) * hw)
            inter, atts, crosses = staged[h]
            k, b = k_ref[:, hs], b_ref[:, hs]
            vb = v_ref[:, hs].astype(BF16)
            outs = [inter[i * sub:(i + 1) * sub] + _dot(atts[i], vb[i * sub:(i + 1) * sub]) for i in range(nsub)]
            for i, j, cs in crosses:
                outs[i] = outs[i] + _dot(cs, vb[j * sub:(j + 1) * sub])
            o = jnp.concatenate(outs, axis=0)

            bl = b[chunk - 1:chunk, :]
            kt = k * jnp.exp(bl - b)
            st_ref[h] = st_ref[h] * jnp.exp(bl) + _dot_tn(vb, kt.astype(BF16))

            ms = jnp.mean(o * o, axis=-1, keepdims=True)
            on = o * lax.rsqrt(ms + EPS) * on_ref[...]
            oall_ref[:, hs] = (on * gt_ref[:, hs]).astype(BF16)

    b3 = b_ref[...].reshape(chunk // A_LOCAL, A_LOCAL, d)
    safe = jnp.max(b3[:, 0:1, :] - b3[:, A_LOCAL - 1:A_LOCAL, :]) <= A_SAFE

    @pl.when(safe)
    def _():
        heads(A_LOCAL)

    @pl.when(jnp.logical_not(safe))
    def _():
        heads(None)

    y = _dot(oall_ref[...], wout_ref[...])
    o_ref[0] = x + (1.0 + mod[2:3]) * y


def _hgrn2(x, mod, g, w_in, a_lb, onorm, w_out, idx):
    bn, s, d = x.shape
    chunk = A_CHUNK
    nh = d // HEAD_W
    act = pltpu.VMEM((chunk, d), F32)
    return pl.pallas_call(
        functools.partial(_hgrn2_kernel, idx=idx, sub=A_SUB),
        grid=(bn, s // chunk),
        in_specs=[
            pl.BlockSpec((1, chunk, d), lambda b, i: (b, i, 0)),
            pl.BlockSpec((1, 3, d), lambda b, i: (b, 0, 0)),
            _resident((1, d)),
            _resident((d, 4 * d)),
            _resident(a_lb.shape),
            _resident((1, HEAD_W)),
            _resident((d, d)),
        ],
        out_specs=pl.BlockSpec((1, chunk, d), lambda b, i: (b, i, 0)),
        out_shape=jax.ShapeDtypeStruct(x.shape, x.dtype),
        scratch_shapes=[pltpu.VMEM((nh, HEAD_W, HEAD_W), F32), act, act, act, act, act, act,
                        pltpu.VMEM((chunk, d), BF16)],
        compiler_params=_params(("arbitrary", "arbitrary"), 48),
        name="hgrn2",
    )(x, mod, g, w_in, a_lb, onorm, w_out)


def _conv_kernel(x_ref, mod_ref, g_ref, win_ref, cw_ref, wout_ref, o_ref, carry_ref):
    tm, d = x_ref.shape[1], x_ref.shape[2]

    @pl.when(pl.program_id(1) == 0)
    def _():
        carry_ref[...] = jnp.zeros_like(carry_ref)

    x = x_ref[0]
    mod = mod_ref[0]
    xn = _prenorm(x, g_ref[...], mod).astype(BF16)
    u = _dot(xn, win_ref[:, d:2 * d]) * _dot(xn, win_ref[:, 2 * d:3 * d])
    carry = carry_ref[...]
    row = lax.broadcasted_iota(jnp.int32, (tm, 1), 0)
    u1 = jnp.where(row == 0, carry[7:8], pltpu.roll(u, 1, 0))
    u2 = jnp.where(row == 0, carry[6:7], jnp.where(row == 1, carry[7:8], pltpu.roll(u, 2, 0)))
    cw = cw_ref[...]
    y = cw[0:1] * u2 + cw[1:2] * u1 + cw[2:3] * u
    carry_ref[...] = u[tm - 8:tm]
    z = (_dot(xn, win_ref[:, 0:d]) * y).astype(BF16)
    o_ref[0] = x + (1.0 + mod[2:3]) * _dot(z, wout_ref[...])


def _short_conv(x, mod, g, w_in, conv_w, w_out):
    bn, s, d = x.shape
    tm = MIX_TM
    return pl.pallas_call(
        _conv_kernel,
        grid=(bn, s // tm),
        in_specs=[
            pl.BlockSpec((1, tm, d), lambda b, i: (b, i, 0)),
            pl.BlockSpec((1, 3, d), lambda b, i: (b, 0, 0)),
            _resident((1, d)),
            _resident((d, 3 * d)),
            _resident(conv_w.shape),
            _resident((d, d)),
        ],
        out_specs=pl.BlockSpec((1, tm, d), lambda b, i: (b, i, 0)),
        out_shape=jax.ShapeDtypeStruct(x.shape, x.dtype),
        scratch_shapes=[pltpu.VMEM((8, d), F32)],
        compiler_params=_params(("arbitrary", "arbitrary"), 48),
        name="short_conv",
    )(x, mod, g, w_in, conv_w, w_out)


def _qkv_kernel(x_ref, mod_ref, g_ref, win_ref, gq_ref, gk_ref, bd_ref, pos_ref, invf_ref,
                q_ref, k_ref, v_ref):
    d = x_ref.shape[2]
    hw = HEAD_W
    xn = _prenorm(x_ref[0], g_ref[...], mod_ref[0]).astype(BF16)

    ang = pos_ref[0].astype(F32) * invf_ref[...]
    lane = lax.broadcasted_iota(jnp.int32, (1, hw), 1) & (B_DH - 1)
    cs, sn = jnp.cos(ang), jnp.sin(ang)
    ca = jnp.where(lane < B_ROT, cs, 1.0)
    cm = jnp.where(lane < B_ROT // 2, -sn, 0.0)
    cp = jnp.where((lane >= B_ROT // 2) & (lane < B_ROT), sn, 0.0)
    half = B_ROT // 2

    gw = 2 * hw
    for off, gain_ref, is_q in ((0, gq_ref, True), (d, gk_ref, False)):
        for j in range(d // gw):
            z = _dot(xn, win_ref[:, off + j * gw:off + (j + 1) * gw])
            ss = _dot((z * z).astype(BF16), bd_ref[...])
            zn = z * lax.rsqrt(ss * (1.0 / B_DH) + EPS) * gain_ref[:, j * gw:(j + 1) * gw]
            for t in range(gw // hw):
                zz = zn[:, t * hw:(t + 1) * hw]
                r = zz * ca + pltpu.roll(zz, hw - half, 1) * cm + pltpu.roll(zz, half, 1) * cp
                head = j * (gw // hw) + t
                if is_q:
                    q_ref[0, :, head * hw:(head + 1) * hw] = (r * B_DH ** -0.5).astype(BF16)
                else:
                    k_ref[0, head, 0] = r.T.astype(BF16)
    v_ref[0] = _dot(xn, win_ref[:, 2 * d:3 * d]).astype(BF16)


def _qkv(x, mod, g, w_in, qk_g, positions):
    bn, s, d = x.shape
    tm = MIX_TM
    reps = d // B_DH
    gq = jnp.tile(qk_g[0].astype(F32), reps)[None, :]
    gk = jnp.tile(qk_g[1].astype(F32), reps)[None, :]
    blk = jnp.arange(2 * HEAD_W) // B_DH
    bd = (blk[:, None] == blk[None, :]).astype(BF16)
    inv_freq = ROPE_THETA ** (-jnp.arange(0, B_ROT, 2, dtype=F32) / B_ROT)
    lane = jnp.arange(HEAD_W) % B_DH
    invf = jnp.where(lane < B_ROT, inv_freq[lane % (B_ROT // 2)], 0.0)[None, :]
    tok = pl.BlockSpec((1, tm, d), lambda b, i: (b, i, 0))
    out = jax.ShapeDtypeStruct((bn, s, d), BF16)
    nh = d // HEAD_W
    kt_spec = pl.BlockSpec((1, nh, 1, HEAD_W, tm), lambda b, i: (b, 0, i, 0, 0))
    kt_out = jax.ShapeDtypeStruct((bn, nh, s // tm, HEAD_W, tm), BF16)
    return pl.pallas_call(
        _qkv_kernel,
        grid=(bn, s // tm),
        in_specs=[
            tok,
            pl.BlockSpec((1, 3, d), lambda b, i: (b, 0, 0)),
            _resident((1, d)),
            _resident((d, 3 * d)),
            _resident((1, d)),
            _resident((1, d)),
            _resident((2 * HEAD_W, 2 * HEAD_W)),
            pl.BlockSpec((1, tm, 1), lambda b, i: (b, i, 0)),
            _resident((1, HEAD_W)),
        ],
        out_specs=[tok, kt_spec, tok],
        out_shape=[out, kt_out, out],
        compiler_params=_params(("arbitrary", "arbitrary"), 48),
        name="attn_qkv",
    )(x, mod, g, w_in, gq, gk, bd, positions.reshape(bn, s, 1), invf)


def _attn_kernel(q_ref, kt_ref, v_ref, lam_ref, sg_ref, o_ref, qs_ref, s_ref, m_ref, acc_ref, *,
                 lambda_init, rows):
    t = q_ref.shape[1]
    hw = HEAD_W
    i = pl.program_id(2)
    q = q_ref[0]
    lane = lax.broadcasted_iota(jnp.int32, (1, hw), 1)
    qs_ref[0] = jnp.where(lane < B_DH, q, jnp.zeros_like(q))
    qs_ref[1] = jnp.where(lane >= B_DH, q, jnp.zeros_like(q))
    m_ref[...] = jnp.full_like(m_ref, -jnp.inf)
    acc_ref[...] = jnp.zeros_like(acc_ref)
    ones = jnp.ones((t, hw), BF16)
    groups = [(c, rc) for rc in range(t // rows) for c in range(2)]

    def scores_to(slot, j):
        per = t // kt_ref.shape[-1]
        kt = jnp.concatenate([kt_ref[0, 0, j * per + a] for a in range(per)], axis=1)
        for c, rc in groups:
            rs = slice(rc * rows, (rc + 1) * rows)
            s_ref[slot, c, rs, :] = _dot(qs_ref[c, rs, :], kt)

    def softmax_pv(slot, j, causal):
        for c, rc in groups:
            rs = slice(rc * rows, (rc + 1) * rows)
            nk = hw * pl.cdiv((rc + 1) * rows, hw) if causal else t
            sc = s_ref[slot, c, rs, :nk]
            if causal:
                r = rc * rows + lax.broadcasted_iota(jnp.int32, (rows, nk), 0)
                cc = lax.broadcasted_iota(jnp.int32, (rows, nk), 1)
                sc = jnp.where(r >= cc, sc, -jnp.inf)
            cols = [sc[:, a * hw:(a + 1) * hw] for a in range(nk // hw)]
            rmax = jnp.max(functools.reduce(jnp.maximum, cols), axis=-1, keepdims=True)
            m_old = m_ref[c, rs, :]
            m_new = jnp.maximum(m_old, rmax)
            p = jnp.concatenate([jnp.exp(cl - m_new) for cl in cols], axis=1).astype(BF16)
            vx = jnp.concatenate([v_ref[0, pl.ds(j * t, nk), :], ones[:nk]], axis=1)
            alpha = jnp.exp(m_old - m_new)
            acc_ref[c, rs, :] = acc_ref[c, rs, :] * jnp.concatenate([alpha, alpha], axis=1) + _dot(p, vx)
            m_ref[c, rs, :] = m_new

    scores_to(0, 0)

    def pair(p, carry):
        scores_to(1, 2 * p + 1)
        softmax_pv(0, 2 * p, False)
        scores_to(0, 2 * p + 2)
        softmax_pv(1, 2 * p + 1, False)
        return carry

    lax.fori_loop(0, i // 2, pair, 0)

    @pl.when(i % 2 == 1)
    def _():
        scores_to(1, i)
        softmax_pv(0, i - 1, False)

    softmax_pv(i % 2, i, True)

    lp = lam_ref[...]
    lam = (jnp.exp(jnp.sum(lp[0:1] * lp[1:2], axis=-1, keepdims=True))
           - jnp.exp(jnp.sum(lp[2:3] * lp[3:4], axis=-1, keepdims=True)) + lambda_init)
    a0, a1 = acc_ref[0], acc_ref[1]
    o = a0[:, :hw] / a0[:, hw:] - lam * (a1[:, :hw] / a1[:, hw:])
    ms = jnp.mean(o * o, axis=-1, keepdims=True)
    o_ref[0] = (o * lax.rsqrt(ms + EPS) * sg_ref[...] * (1.0 - lambda_init)).astype(o_ref.dtype)


def _attention(q, kt, v, lam_p, subln, lambda_init):
    bn, s, d = q.shape
    t = ATT_T
    nh = d // HEAD_W
    kb = kt.shape[-1]
    assert kt.shape == (bn, nh, s // kb, HEAD_W, kb) and t % kb == 0
    tile = pl.BlockSpec((1, t, HEAD_W), lambda b, h, i: (b, i, h))
    return pl.pallas_call(
        functools.partial(_attn_kernel, lambda_init=lambda_init, rows=ATT_R),
        grid=(bn, nh, s // t),
        in_specs=[
            tile,
            pl.BlockSpec((1, 1, s // kb, HEAD_W, kb), lambda b, h, i: (b, h, 0, 0, 0)),
            pl.BlockSpec((1, s, HEAD_W), lambda b, h, i: (b, 0, h)),
            _resident(lam_p.shape),
            _resident((1, HEAD_W)),
        ],
        out_specs=tile,
        out_shape=jax.ShapeDtypeStruct((bn, s, d), BF16),
        scratch_shapes=[pltpu.VMEM((2, t, HEAD_W), BF16), pltpu.VMEM((2, 2, t, t), F32),
                        pltpu.VMEM((2, t, HEAD_W), F32),
                        pltpu.VMEM((2, t, 2 * HEAD_W), F32)],
        compiler_params=_params(("arbitrary", "arbitrary", "arbitrary"), 56),
        name="attn_core",
    )(q, kt, v, lam_p, subln)


def _outproj_kernel(a_ref, x_ref, mod_ref, w_ref, o_ref):
    o_ref[0] = x_ref[0] + (1.0 + mod_ref[0][2:3]) * _dot(a_ref[0], w_ref[...])


def _outproj(a, x, mod, w_out):
    bn, s, d = x.shape
    tm = MIX_TM
    tok = pl.BlockSpec((1, tm, d), lambda b, i: (b, i, 0))
    return pl.pallas_call(
        _outproj_kernel,
        grid=(bn, s // tm),
        in_specs=[tok, tok, pl.BlockSpec((1, 3, d), lambda b, i: (b, 0, 0)), _resident((d, d))],
        out_specs=tok,
        out_shape=jax.ShapeDtypeStruct(x.shape, x.dtype),
        compiler_params=_params(("arbitrary", "arbitrary"), 32),
        name="attn_out",
    )(a, x, mod, w_out)


def kernel(x, c, positions, ada_w, ada_b, norm_g, ffn_wi, ffn_wo, a_w_in, a_w_out, a_lb, a_onorm,
           b_w_in, b_w_out, b_qk_g, b_lam, b_subln, c_w_in, c_conv, c_w_out):
    bn, s, d = x.shape
    depth = ada_w.shape[0]
    mod = _adaln(c, ada_w, ada_b).reshape(depth, bn, N_SUB, 3, d)

    for l in range(depth):
        def sub(j):
            return mod[l, :, j], norm_g[l, j][None, :]

        m, g = sub(0)
        x = _ffn(x, m, g, ffn_wi[l, 0].astype(BF16), ffn_wo[l, 0].astype(BF16))
        m, g = sub(1)
        kind, idx = l % N_MIXERS, l // N_MIXERS
        if kind == 0:
            x = _hgrn2(x, m, g, a_w_in[idx].astype(BF16), a_lb.astype(F32), a_onorm[idx][None, :].astype(F32),
                       a_w_out[idx].astype(BF16), idx)
        elif kind == 1:
            lambda_init = 0.8 - 0.6 * math.exp(-0.3 * l)
            q, k, v = _qkv(x, m, g, b_w_in[idx].astype(BF16), b_qk_g[idx], positions)
            a = _attention(q, k, v, b_lam[idx].astype(F32), b_subln[idx][None, :].astype(F32), lambda_init)
            x = _outproj(a, x, m, b_w_out[idx].astype(BF16))
        else:
            x = _short_conv(x, m, g, c_w_in[idx].astype(BF16), c_conv[idx].astype(F32), c_w_out[idx].astype(BF16))
        m, g = sub(2)
        x = _ffn(x, m, g, ffn_wi[l, 1].astype(BF16), ffn_wo[l, 1].astype(BF16))
    return x
```

```python
import functools
import math

import jax
import jax.numpy as jnp
from jax import lax
from jax.experimental import pallas as pl
from jax.experimental.pallas import tpu as pltpu

F32 = jnp.float32
BF16 = jnp.bfloat16

EPS = 1e-6
N_SUB = 3
N_MIXERS = 3
HEAD_W = 128
N_HEADS = 8
B_DH = 64
B_ROT = 16
ROPE_THETA = 500000.0
MIB = 1024 * 1024

FFN_TM = 512
FFN_FC = 256
MIX_TM = 512
A_CHUNK = 256
A_SUB = 128
A_LOCAL = 32
A_SAFE = 60.0
ATT_T = 1024
ATT_R = 256


def _silu(z):
    return z * jax.nn.sigmoid(z)


def _dot(a, b):
    return jnp.dot(a, b, preferred_element_type=F32)


def _dot_nt(a, b):
    return lax.dot_general(a, b, (((1,), (1,)), ((), ())), preferred_element_type=F32)


def _dot_tn(a, b):
    return lax.dot_general(a, b, (((0,), (0,)), ((), ())), preferred_element_type=F32)


def _prenorm(x, g, mod):
    ms = jnp.mean(x * x, axis=-1, keepdims=True)
    y = x * lax.rsqrt(ms + EPS) * g
    return y * (1.0 + mod[1:2]) + mod[0:1]


def _resident(shape):
    nd = len(shape)
    return pl.BlockSpec(shape, lambda *_: (0,) * nd, pipeline_mode=pl.Buffered(1))


def _resident_at(shape, lead):
    nd = len(shape)
    return pl.BlockSpec((None,) * len(lead) + tuple(shape), lambda *_: tuple(lead) + (0,) * nd,
                        pipeline_mode=pl.Buffered(1))


def _params(sem, vmem_mib):
    return pltpu.CompilerParams(dimension_semantics=sem, vmem_limit_bytes=vmem_mib * MIB)


def _adaln_kernel(c_ref, w_ref, b_ref, o_ref):
    ca = _silu(c_ref[...]).astype(BF16)
    o_ref[0] = _dot(ca, w_ref[0].astype(BF16)) + b_ref[0]


def _adaln(c, ada_w, ada_b):
    depth, d, n = ada_w.shape
    bn = c.shape[0]
    rows = 8 * pl.cdiv(bn, 8)
    cp = jnp.zeros((rows, d), F32).at[:bn].set(c)
    tn = n // 6
    out = pl.pallas_call(
        _adaln_kernel,
        grid=(depth, n // tn),
        in_specs=[
            pl.BlockSpec((rows, d), lambda l, j: (0, 0)),
            pl.BlockSpec((1, d, tn), lambda l, j: (l, 0, j)),
            pl.BlockSpec((1, 1, tn), lambda l, j: (l, 0, j)),
        ],
        out_specs=pl.BlockSpec((1, rows, tn), lambda l, j: (l, 0, j)),
        out_shape=jax.ShapeDtypeStruct((depth, rows, n), F32),
        compiler_params=_params(("arbitrary", "arbitrary"), 40),
        name="adaln",
    )(cp, ada_w, ada_b.reshape(depth, 1, n))
    return out[:, :bn]


def _ffn_kernel(x_ref, mod_ref, g_ref, wi_ref, wo_ref, o_ref, h_ref, *, fc):
    x = x_ref[0]
    mod = mod_ref[0]
    xn = _prenorm(x, g_ref[...], mod).astype(BF16)
    ff = h_ref.shape[1]
    for c in range(ff // fc):
        gate = _dot(xn, wi_ref[:, c * fc:(c + 1) * fc])
        up = _dot(xn, wi_ref[:, ff + c * fc:ff + (c + 1) * fc])
        h_ref[:, c * fc:(c + 1) * fc] = (_silu(gate) * up).astype(BF16)
    y = _dot(h_ref[...], wo_ref[...])
    o_ref[0] = x + (0.5 * (1.0 + mod[2:3])) * y


def _ffn(x, mod, g, wi, wo, which):
    bn, s, d = x.shape
    ff = wo.shape[-2]
    tm, fc = FFN_TM, FFN_FC
    return pl.pallas_call(
        functools.partial(_ffn_kernel, fc=fc),
        grid=(bn, s // tm),
        in_specs=[
            pl.BlockSpec((1, tm, d), lambda b, i: (b, i, 0)),
            pl.BlockSpec((1, 3, d), lambda b, i: (b, 0, 0)),
            _resident((1, d)),
            _resident_at((d, 2 * ff), which),
            _resident_at((ff, d), which),
        ],
        out_specs=pl.BlockSpec((1, tm, d), lambda b, i: (b, i, 0)),
        out_shape=jax.ShapeDtypeStruct(x.shape, x.dtype),
        scratch_shapes=[pltpu.VMEM((tm, ff), BF16)],
        compiler_params=_params(("arbitrary", "arbitrary"), 48),
        name="ffn",
    )(x, mod, g, wi, wo)


def _block_masks(n, local):
    row = lax.broadcasted_iota(jnp.int32, (n, n), 0)
    col = lax.broadcasted_iota(jnp.int32, (n, n), 1)
    sh = int(math.log2(local))
    masks = {"diag": row == col, "local": ((row >> sh) == (col >> sh)) & (row >= col)}
    w = n // 2
    while w >= 1:
        sh = int(math.log2(w))
        rb, cb = row >> sh, col >> sh
        masks[w] = (rb == cb + 1) & ((cb & 1) == 0)
        w //= 2
    return masks


def _block_scores(q, k, lf, b, masks, local):
    n, hw = q.shape
    if local:
        b3 = b.reshape(n // local, local, hw)
        first = b3[:, 0:1, :]
        qe = (q * jnp.exp(b3 - first).reshape(n, hw)).astype(BF16)
        ke = (k * jnp.exp(first - b3).reshape(n, hw)).astype(BF16)
        att = jnp.where(masks["local"], _dot_nt(qe, ke), 0.0)
    else:
        att = jnp.where(masks["diag"], _dot_nt(q.astype(BF16), k.astype(BF16)), 0.0)
    w = n // 2
    while w >= (local or 8):
        b3 = b.reshape(n // w, w, hw)
        last = b3[:, w - 1:w, :]
        prev = jnp.concatenate([last[:1], last[:-1]], axis=0)
        qa = jnp.minimum(b3 - prev, 0.0).reshape(n, hw)
        ka = (last - b3).reshape(n, hw)
        sc = _dot_nt((q * jnp.exp(qa)).astype(BF16), (k * jnp.exp(ka)).astype(BF16))
        att = jnp.where(masks[w], sc, att)
        w //= 2
    if local:
        return att
    t = lax.broadcasted_iota(jnp.int32, (n, 1), 0)
    back = [None] + [pltpu.roll(lf, i, 0) for i in range(1, 4)]
    fwd = [None] + [pltpu.roll(lf, n - i, 0) for i in range(1, 4)]
    while w >= 1:
        r = t & (w - 1)
        qa = lf
        ke = k
        if w > 1:
            ka = jnp.where(r <= w - 2, fwd[1], 0.0)
            for i in range(1, w):
                qa = qa + jnp.where(r >= i, back[i], 0.0)
            for i in range(2, w):
                ka = ka + jnp.where(r <= w - 1 - i, fwd[i], 0.0)
            ke = k * jnp.exp(ka)
        sc = _dot_nt((q * jnp.exp(qa)).astype(BF16), ke.astype(BF16))
        att = jnp.where(masks[w], sc, att)
        w //= 2
    return att


def _hgrn2_kernel(x_ref, mod_ref, g_ref, win_ref, alb_ref, on_ref, wout_ref, o_ref,
                  st_ref, q_ref, k_ref, lf_ref, b_ref, v_ref, gt_ref, oall_ref, *, idx, sub):
    chunk, d = x_ref.shape[1], x_ref.shape[2]
    hw = HEAD_W

    @pl.when(pl.program_id(1) == 0)
    def _():
        st_ref[...] = jnp.zeros_like(st_ref)

    x = x_ref[0]
    mod = mod_ref[0]
    xn = _prenorm(x, g_ref[...], mod).astype(BF16)

    alb = alb_ref[...]
    e = jnp.exp(alb - jnp.max(alb, axis=0, keepdims=True))
    sm = e / jnp.sum(e, axis=0, keepdims=True)
    lb = jnp.sum(sm[0:idx + 1], axis=0, keepdims=True) - sm[0:1]

    q_ref[...] = _silu(_dot(xn, win_ref[:, 0:d]))
    f = lb + (1.0 - lb) * jax.nn.sigmoid(_dot(xn, win_ref[:, d:2 * d]))
    k_ref[...] = 1.0 - f
    v_ref[...] = _dot(xn, win_ref[:, 2 * d:3 * d])
    gt_ref[...] = _silu(_dot(xn, win_ref[:, 3 * d:4 * d]))

    lf = jnp.log(f)
    lf_ref[...] = lf
    hi = lf.astype(BF16)
    r1 = lf - hi.astype(F32)
    mid = r1.astype(BF16)
    lo = (r1 - mid.astype(F32)).astype(BF16)
    row = lax.broadcasted_iota(jnp.int32, (chunk, chunk), 0)
    col = lax.broadcasted_iota(jnp.int32, (chunk, chunk), 1)
    tri = (row >= col).astype(BF16)
    b_ref[...] = _dot(tri, hi) + _dot(tri, mid) + _dot(tri, lo)

    masks = _block_masks(sub, A_LOCAL)
    nsub = chunk // sub

    def heads(local):
        staged = []
        for h in range(d // hw):
            hs = slice(h * hw, (h + 1) * hw)
            q, k, lf, b = q_ref[:, hs], k_ref[:, hs], lf_ref[:, hs], b_ref[:, hs]
            inter = _dot_nt((q * jnp.exp(b)).astype(BF16), st_ref[h].astype(BF16))
            atts, crosses = [], []
            for i in range(nsub):
                rs = slice(i * sub, (i + 1) * sub)
                atts.append(_block_scores(q[rs], k[rs], lf[rs], b[rs], masks, local).astype(BF16))
                if i > 0:
                    edge = b[i * sub - 1:i * sub, :]
                    qc = (q[rs] * jnp.exp(b[rs] - edge)).astype(BF16)
                    for j in range(i):
                        js = slice(j * sub, (j + 1) * sub)
                        kc = (k[js] * jnp.exp(edge - b[js])).astype(BF16)
                        crosses.append((i, j, _dot_nt(qc, kc).astype(BF16)))
            staged.append((inter, atts, crosses))
        for h in range(d // hw):
            hs = slice(h * hw, (h + 1) * hw)
            inter, atts, crosses = staged[h]
            k, b = k_ref[:, hs], b_ref[:, hs]
            vb = v_ref[:, hs].astype(BF16)
            outs = [inter[i * sub:(i + 1) * sub] + _dot(atts[i], vb[i * sub:(i + 1) * sub]) for i in range(nsub)]
            for i, j, cs in crosses:
                outs[i] = outs[i] + _dot(cs, vb[j * sub:(j + 1) * sub])
            o = jnp.concatenate(outs, axis=0)

            bl = b[chunk - 1:chunk, :]
            kt = k * jnp.exp(bl - b)
            st_ref[h] = st_ref[h] * jnp.exp(bl) + _dot_tn(vb, kt.astype(BF16))

            ms = jnp.mean(o * o, axis=-1, keepdims=True)
            on = o * lax.rsqrt(ms + EPS) * on_ref[...]
            oall_ref[:, hs] = (on * gt_ref[:, hs]).astype(BF16)

    b3 = b_ref[...].reshape(chunk // A_LOCAL, A_LOCAL, d)
    safe = jnp.max(b3[:, 0:1, :] - b3[:, A_LOCAL - 1:A_LOCAL, :]) <= A_SAFE

    @pl.when(safe)
    def _():
        heads(A_LOCAL)

    @pl.when(jnp.logical_not(safe))
    def _():
        heads(None)

    y = _dot(oall_ref[...], wout_ref[...])
    o_ref[0] = x + (1.0 + mod[2:3]) * y


def _hgrn2(x, mod, g, w_in, a_lb, onorm, w_out, idx):
    bn, s, d = x.shape
    chunk = A_CHUNK
    nh = d // HEAD_W
    act = pltpu.VMEM((chunk, d), F32)
    return pl.pallas_call(
        functools.partial(_hgrn2_kernel, idx=idx, sub=A_SUB),
        grid=(bn, s // chunk),
        in_specs=[
            pl.BlockSpec((1, chunk, d), lambda b, i: (b, i, 0)),
            pl.BlockSpec((1, 3, d), lambda b, i: (b, 0, 0)),
            _resident((1, d)),
            _resident_at((d, 4 * d), (idx,)),
            _resident(a_lb.shape),
            _resident((1, HEAD_W)),
            _resident_at((d, d), (idx,)),
        ],
        out_specs=pl.BlockSpec((1, chunk, d), lambda b, i: (b, i, 0)),
        out_shape=jax.ShapeDtypeStruct(x.shape, x.dtype),
        scratch_shapes=[pltpu.VMEM((nh, HEAD_W, HEAD_W), F32), act, act, act, act, act, act,
                        pltpu.VMEM((chunk, d), BF16)],
        compiler_params=_params(("arbitrary", "arbitrary"), 48),
        name="hgrn2",
    )(x, mod, g, w_in, a_lb, onorm, w_out)


def _conv_kernel(x_ref, mod_ref, g_ref, win_ref, cw_ref, wout_ref, o_ref, carry_ref):
    tm, d = x_ref.shape[1], x_ref.shape[2]

    @pl.when(pl.program_id(1) == 0)
    def _():
        carry_ref[...] = jnp.zeros_like(carry_ref)

    x = x_ref[0]
    mod = mod_ref[0]
    xn = _prenorm(x, g_ref[...], mod).astype(BF16)
    u = _dot(xn, win_ref[:, d:2 * d]) * _dot(xn, win_ref[:, 2 * d:3 * d])
    carry = carry_ref[...]
    row = lax.broadcasted_iota(jnp.int32, (tm, 1), 0)
    u1 = jnp.where(row == 0, carry[7:8], pltpu.roll(u, 1, 0))
    u2 = jnp.where(row == 0, carry[6:7], jnp.where(row == 1, carry[7:8], pltpu.roll(u, 2, 0)))
    cw = cw_ref[...]
    y = cw[0:1] * u2 + cw[1:2] * u1 + cw[2:3] * u
    carry_ref[...] = u[tm - 8:tm]
    z = (_dot(xn, win_ref[:, 0:d]) * y).astype(BF16)
    o_ref[0] = x + (1.0 + mod[2:3]) * _dot(z, wout_ref[...])


def _short_conv(x, mod, g, w_in, conv_w, w_out):
    bn, s, d = x.shape
    tm = MIX_TM
    return pl.pallas_call(
        _conv_kernel,
        grid=(bn, s // tm),
        in_specs=[
            pl.BlockSpec((1, tm, d), lambda b, i: (b, i, 0)),
            pl.BlockSpec((1, 3, d), lambda b, i: (b, 0, 0)),
            _resident((1, d)),
            _resident((d, 3 * d)),
            _resident(conv_w.shape),
            _resident((d, d)),
        ],
        out_specs=pl.BlockSpec((1, tm, d), lambda b, i: (b, i, 0)),
        out_shape=jax.ShapeDtypeStruct(x.shape, x.dtype),
        scratch_shapes=[pltpu.VMEM((8, d), F32)],
        compiler_params=_params(("arbitrary", "arbitrary"), 48),
        name="short_conv",
    )(x, mod, g, w_in, conv_w, w_out)


def _qkv_kernel(x_ref, mod_ref, g_ref, win_ref, gq_ref, gk_ref, bd_ref, pos_ref, invf_ref,
                q_ref, k_ref, v_ref):
    d = x_ref.shape[2]
    hw = HEAD_W
    xn = _prenorm(x_ref[0], g_ref[...], mod_ref[0]).astype(BF16)

    ang = pos_ref[0].astype(F32) * invf_ref[...]
    lane = lax.broadcasted_iota(jnp.int32, (1, hw), 1) & (B_DH - 1)
    cs, sn = jnp.cos(ang), jnp.sin(ang)
    ca = jnp.where(lane < B_ROT, cs, 1.0)
    cm = jnp.where(lane < B_ROT // 2, -sn, 0.0)
    cp = jnp.where((lane >= B_ROT // 2) & (lane < B_ROT), sn, 0.0)
    half = B_ROT // 2

    gw = 2 * hw
    for off, gain_ref, is_q in ((0, gq_ref, True), (d, gk_ref, False)):
        for j in range(d // gw):
            z = _dot(xn, win_ref[:, off + j * gw:off + (j + 1) * gw])
            ss = _dot((z * z).astype(BF16), bd_ref[...])
            zn = z * lax.rsqrt(ss * (1.0 / B_DH) + EPS) * gain_ref[:, j * gw:(j + 1) * gw]
            for t in range(gw // hw):
                zz = zn[:, t * hw:(t + 1) * hw]
                r = zz * ca + pltpu.roll(zz, hw - half, 1) * cm + pltpu.roll(zz, half, 1) * cp
                head = j * (gw // hw) + t
                if is_q:
                    q_ref[0, :, head * hw:(head + 1) * hw] = (r * B_DH ** -0.5).astype(BF16)
                else:
                    k_ref[0, head, 0] = r.T.astype(BF16)
    v_ref[0] = _dot(xn, win_ref[:, 2 * d:3 * d]).astype(BF16)


def _qkv(x, mod, g, w_in, qk_g, positions):
    bn, s, d = x.shape
    tm = MIX_TM
    reps = d // B_DH
    gq = jnp.tile(qk_g[0].astype(F32), reps)[None, :]
    gk = jnp.tile(qk_g[1].astype(F32), reps)[None, :]
    blk = jnp.arange(2 * HEAD_W) // B_DH
    bd = (blk[:, None] == blk[None, :]).astype(BF16)
    inv_freq = ROPE_THETA ** (-jnp.arange(0, B_ROT, 2, dtype=F32) / B_ROT)
    lane = jnp.arange(HEAD_W) % B_DH
    invf = jnp.where(lane < B_ROT, inv_freq[lane % (B_ROT // 2)], 0.0)[None, :]
    tok = pl.BlockSpec((1, tm, d), lambda b, i: (b, i, 0))
    out = jax.ShapeDtypeStruct((bn, s, d), BF16)
    nh = d // HEAD_W
    kt_spec = pl.BlockSpec((1, nh, 1, HEAD_W, tm), lambda b, i: (b, 0, i, 0, 0))
    kt_out = jax.ShapeDtypeStruct((bn, nh, s // tm, HEAD_W, tm), BF16)
    return pl.pallas_call(
        _qkv_kernel,
        grid=(bn, s // tm),
        in_specs=[
            tok,
            pl.BlockSpec((1, 3, d), lambda b, i: (b, 0, 0)),
            _resident((1, d)),
            _resident((d, 3 * d)),
            _resident((1, d)),
            _resident((1, d)),
            _resident((2 * HEAD_W, 2 * HEAD_W)),
            pl.BlockSpec((1, tm, 1), lambda b, i: (b, i, 0)),
            _resident((1, HEAD_W)),
        ],
        out_specs=[tok, kt_spec, tok],
        out_shape=[out, kt_out, out],
        compiler_params=_params(("arbitrary", "arbitrary"), 48),
        name="attn_qkv",
    )(x, mod, g, w_in, gq, gk, bd, positions.reshape(bn, s, 1), invf)


def _attn_kernel(q_ref, kt_ref, v_ref, lam_ref, sg_ref, o_ref, qs_ref, s_ref, m_ref, acc_ref, *,
                 lambda_init, rows):
    t = q_ref.shape[1]
    hw = HEAD_W
    i = pl.program_id(2)
    q = q_ref[0]
    lane = lax.broadcasted_iota(jnp.int32, (1, hw), 1)
    qs_ref[0] = jnp.where(lane < B_DH, q, jnp.zeros_like(q))
    qs_ref[1] = jnp.where(lane >= B_DH, q, jnp.zeros_like(q))
    m_ref[...] = jnp.full_like(m_ref, -jnp.inf)
    acc_ref[...] = jnp.zeros_like(acc_ref)
    ones = jnp.ones((t, hw), BF16)
    groups = [(c, rc) for rc in range(t // rows) for c in range(2)]

    def scores_to(slot, j):
        per = t // kt_ref.shape[-1]
        kt = jnp.concatenate([kt_ref[0, 0, j * per + a] for a in range(per)], axis=1)
        for c, rc in groups:
            rs = slice(rc * rows, (rc + 1) * rows)
            s_ref[slot, c, rs, :] = _dot(qs_ref[c, rs, :], kt)

    def softmax_pv(slot, j, causal):
        for c, rc in groups:
            rs = slice(rc * rows, (rc + 1) * rows)
            nk = hw * pl.cdiv((rc + 1) * rows, hw) if causal else t
            sc = s_ref[slot, c, rs, :nk]
            if causal:
                r = rc * rows + lax.broadcasted_iota(jnp.int32, (rows, nk), 0)
                cc = lax.broadcasted_iota(jnp.int32, (rows, nk), 1)
                sc = jnp.where(r >= cc, sc, -jnp.inf)
            cols = [sc[:, a * hw:(a + 1) * hw] for a in range(nk // hw)]
            rmax = jnp.max(functools.reduce(jnp.maximum, cols), axis=-1, keepdims=True)
            m_old = m_ref[c, rs, :]
            m_new = jnp.maximum(m_old, rmax)
            p = jnp.concatenate([jnp.exp(cl - m_new) for cl in cols], axis=1).astype(BF16)
            vx = jnp.concatenate([v_ref[0, pl.ds(j * t, nk), :], ones[:nk]], axis=1)
            alpha = jnp.exp(m_old - m_new)
            acc_ref[c, rs, :] = acc_ref[c, rs, :] * jnp.concatenate([alpha, alpha], axis=1) + _dot(p, vx)
            m_ref[c, rs, :] = m_new

    scores_to(0, 0)

    def pair(p, carry):
        scores_to(1, 2 * p + 1)
        softmax_pv(0, 2 * p, False)
        scores_to(0, 2 * p + 2)
        softmax_pv(1, 2 * p + 1, False)
        return carry

    lax.fori_loop(0, i // 2, pair, 0)

    @pl.when(i % 2 == 1)
    def _():
        scores_to(1, i)
        softmax_pv(0, i - 1, False)

    softmax_pv(i % 2, i, True)

    lp = lam_ref[...]
    lam = (jnp.exp(jnp.sum(lp[0:1] * lp[1:2], axis=-1, keepdims=True))
           - jnp.exp(jnp.sum(lp[2:3] * lp[3:4], axis=-1, keepdims=True)) + lambda_init)
    a0, a1 = acc_ref[0], acc_ref[1]
    o = a0[:, :hw] / a0[:, hw:] - lam * (a1[:, :hw] / a1[:, hw:])
    ms = jnp.mean(o * o, axis=-1, keepdims=True)
    o_ref[0] = (o * lax.rsqrt(ms + EPS) * sg_ref[...] * (1.0 - lambda_init)).astype(o_ref.dtype)


def _attention(q, kt, v, lam_p, subln, lambda_init):
    bn, s, d = q.shape
    t = ATT_T
    nh = d // HEAD_W
    kb = kt.shape[-1]
    assert kt.shape == (bn, nh, s // kb, HEAD_W, kb) and t % kb == 0
    tile = pl.BlockSpec((1, t, HEAD_W), lambda b, h, i: (b, i, h))
    return pl.pallas_call(
        functools.partial(_attn_kernel, lambda_init=lambda_init, rows=ATT_R),
        grid=(bn, nh, s // t),
        in_specs=[
            tile,
            pl.BlockSpec((1, 1, s // kb, HEAD_W, kb), lambda b, h, i: (b, h, 0, 0, 0)),
            pl.BlockSpec((1, s, HEAD_W), lambda b, h, i: (b, 0, h)),
            _resident(lam_p.shape),
            _resident((1, HEAD_W)),
        ],
        out_specs=tile,
        out_shape=jax.ShapeDtypeStruct((bn, s, d), BF16),
        scratch_shapes=[pltpu.VMEM((2, t, HEAD_W), BF16), pltpu.VMEM((2, 2, t, t), F32),
                        pltpu.VMEM((2, t, HEAD_W), F32),
                        pltpu.VMEM((2, t, 2 * HEAD_W), F32)],
        compiler_params=_params(("arbitrary", "arbitrary", "arbitrary"), 56),
        name="attn_core",
    )(q, kt, v, lam_p, subln)


def _outproj_kernel(a_ref, x_ref, mod_ref, w_ref, o_ref):
    o_ref[0] = x_ref[0] + (1.0 + mod_ref[0][2:3]) * _dot(a_ref[0], w_ref[...])


def _outproj(a, x, mod, w_out):
    bn, s, d = x.shape
    tm = MIX_TM
    tok = pl.BlockSpec((1, tm, d), lambda b, i: (b, i, 0))
    return pl.pallas_call(
        _outproj_kernel,
        grid=(bn, s // tm),
        in_specs=[tok, tok, pl.BlockSpec((1, 3, d), lambda b, i: (b, 0, 0)), _resident((d, d))],
        out_specs=tok,
        out_shape=jax.ShapeDtypeStruct(x.shape, x.dtype),
        compiler_params=_params(("arbitrary", "arbitrary"), 32),
        name="attn_out",
    )(a, x, mod, w_out)


def kernel(x, c, positions, ada_w, ada_b, norm_g, ffn_wi, ffn_wo, a_w_in, a_w_out, a_lb, a_onorm,
           b_w_in, b_w_out, b_qk_g, b_lam, b_subln, c_w_in, c_conv, c_w_out):
    bn, s, d = x.shape
    depth = ada_w.shape[0]
    mod = _adaln(c, ada_w, ada_b).reshape(depth, bn, N_SUB, 3, d)
    wi, wo = ffn_wi.astype(BF16), ffn_wo.astype(BF16)
    a_in, a_out = a_w_in.astype(BF16), a_w_out.astype(BF16)

    for l in range(depth):
        def sub(j):
            return mod[l, :, j], norm_g[l, j][None, :]

        m, g = sub(0)
        x = _ffn(x, m, g, wi, wo, (l, 0))
        m, g = sub(1)
        kind, idx = l % N_MIXERS, l // N_MIXERS
        if kind == 0:
            x = _hgrn2(x, m, g, a_in, a_lb.astype(F32), a_onorm[idx][None, :].astype(F32), a_out, idx)
        elif kind == 1:
            lambda_init = 0.8 - 0.6 * math.exp(-0.3 * l)
            q, k, v = _qkv(x, m, g, b_w_in[idx].astype(BF16), b_qk_g[idx], positions)
            a = _attention(q, k, v, b_lam[idx].astype(F32), b_subln[idx][None, :].astype(F32), lambda_init)
            x = _outproj(a, x, m, b_w_out[idx].astype(BF16))
        else:
            x = _short_conv(x, m, g, c_w_in[idx].astype(BF16), c_conv[idx].astype(F32), c_w_out[idx].astype(BF16))
        m, g = sub(2)
        x = _ffn(x, m, g, wi, wo, (l, 1))
    return x
```

```python
import functools
import math

import jax
import jax.numpy as jnp
from jax import lax
from jax.experimental import pallas as pl
from jax.experimental.pallas import tpu as pltpu

F32 = jnp.float32
BF16 = jnp.bfloat16

EPS = 1e-6
N_SUB = 3
N_MIXERS = 3
HEAD_W = 128
N_HEADS = 8
B_DH = 64
B_ROT = 16
ROPE_THETA = 500000.0
MIB = 1024 * 1024

FFN_TM = 512
FFN_FC = 256
MIX_TM = 512
A_CHUNK = 256
A_SUB = 128
A_LOCAL = 32
A_SAFE = 60.0
ATT_T = 1024
ATT_R = 256


def _silu(z):
    return z * jax.nn.sigmoid(z)


def _dot(a, b):
    return jnp.dot(a, b, preferred_element_type=F32)


def _dot_nt(a, b):
    return lax.dot_general(a, b, (((1,), (1,)), ((), ())), preferred_element_type=F32)


def _dot_tn(a, b):
    return lax.dot_general(a, b, (((0,), (0,)), ((), ())), preferred_element_type=F32)


def _prenorm(x, g, mod):
    ms = jnp.mean(x * x, axis=-1, keepdims=True)
    y = x * lax.rsqrt(ms + EPS) * g
    return y * (1.0 + mod[1:2]) + mod[0:1]


def _resident(shape):
    nd = len(shape)
    return pl.BlockSpec(shape, lambda *_: (0,) * nd, pipeline_mode=pl.Buffered(1))


def _resident_at(shape, lead):
    nd = len(shape)
    return pl.BlockSpec((None,) * len(lead) + tuple(shape), lambda *_: tuple(lead) + (0,) * nd,
                        pipeline_mode=pl.Buffered(1))


def _params(sem, vmem_mib):
    return pltpu.CompilerParams(dimension_semantics=sem, vmem_limit_bytes=vmem_mib * MIB)


def _adaln_kernel(c_ref, w_ref, b_ref, o_ref):
    ca = _silu(c_ref[...]).astype(BF16)
    o_ref[0] = _dot(ca, w_ref[0].astype(BF16)) + b_ref[0]


def _adaln(c, ada_w, ada_b):
    depth, d, n = ada_w.shape
    bn = c.shape[0]
    rows = 8 * pl.cdiv(bn, 8)
    cp = jnp.zeros((rows, d), F32).at[:bn].set(c)
    tn = n // 6
    out = pl.pallas_call(
        _adaln_kernel,
        grid=(depth, n // tn),
        in_specs=[
            pl.BlockSpec((rows, d), lambda l, j: (0, 0)),
            pl.BlockSpec((1, d, tn), lambda l, j: (l, 0, j)),
            pl.BlockSpec((1, 1, tn), lambda l, j: (l, 0, j)),
        ],
        out_specs=pl.BlockSpec((1, rows, tn), lambda l, j: (l, 0, j)),
        out_shape=jax.ShapeDtypeStruct((depth, rows, n), F32),
        compiler_params=_params(("arbitrary", "arbitrary"), 40),
        name="adaln",
    )(cp, ada_w, ada_b.reshape(depth, 1, n))
    return out[:, :bn]


def _ffn_kernel(x_ref, mod_ref, g_ref, wi_ref, wo_ref, o_ref, h_ref, *, fc):
    x = x_ref[0]
    mod = mod_ref[0]
    xn = _prenorm(x, g_ref[...], mod).astype(BF16)
    ff = h_ref.shape[1]
    for c in range(ff // fc):
        gate = _dot(xn, wi_ref[:, c * fc:(c + 1) * fc])
        up = _dot(xn, wi_ref[:, ff + c * fc:ff + (c + 1) * fc])
        h_ref[:, c * fc:(c + 1) * fc] = (_silu(gate) * up).astype(BF16)
    y = _dot(h_ref[...], wo_ref[...])
    o_ref[0] = x + (0.5 * (1.0 + mod[2:3])) * y


def _ffn(x, mod, g, wi, wo, which):
    bn, s, d = x.shape
    ff = wo.shape[-2]
    tm, fc = FFN_TM, FFN_FC
    return pl.pallas_call(
        functools.partial(_ffn_kernel, fc=fc),
        grid=(bn, s // tm),
        in_specs=[
            pl.BlockSpec((1, tm, d), lambda b, i: (b, i, 0)),
            pl.BlockSpec((1, 3, d), lambda b, i: (b, 0, 0)),
            _resident((1, d)),
            _resident_at((d, 2 * ff), which),
            _resident_at((ff, d), which),
        ],
        out_specs=pl.BlockSpec((1, tm, d), lambda b, i: (b, i, 0)),
        out_shape=jax.ShapeDtypeStruct(x.shape, x.dtype),
        scratch_shapes=[pltpu.VMEM((tm, ff), BF16)],
        compiler_params=_params(("arbitrary", "arbitrary"), 48),
        name="ffn",
    )(x, mod, g, wi, wo)


def _block_masks(n, local):
    row = lax.broadcasted_iota(jnp.int32, (n, n), 0)
    col = lax.broadcasted_iota(jnp.int32, (n, n), 1)
    sh = int(math.log2(local))
    masks = {"diag": row == col, "local": ((row >> sh) == (col >> sh)) & (row >= col)}
    w = n // 2
    while w >= 1:
        sh = int(math.log2(w))
        rb, cb = row >> sh, col >> sh
        masks[w] = (rb == cb + 1) & ((cb & 1) == 0)
        w //= 2
    return masks


def _block_scores(q, k, lf, b, masks, local):
    n, hw = q.shape
    if local:
        b3 = b.reshape(n // local, local, hw)
        first = b3[:, 0:1, :]
        qe = (q * jnp.exp(b3 - first).reshape(n, hw)).astype(BF16)
        ke = (k * jnp.exp(first - b3).reshape(n, hw)).astype(BF16)
        att = jnp.where(masks["local"], _dot_nt(qe, ke), 0.0)
    else:
        att = jnp.where(masks["diag"], _dot_nt(q.astype(BF16), k.astype(BF16)), 0.0)
    w = n // 2
    while w >= (local or 8):
        b3 = b.reshape(n // w, w, hw)
        last = b3[:, w - 1:w, :]
        prev = jnp.concatenate([last[:1], last[:-1]], axis=0)
        qa = jnp.minimum(b3 - prev, 0.0).reshape(n, hw)
        ka = (last - b3).reshape(n, hw)
        sc = _dot_nt((q * jnp.exp(qa)).astype(BF16), (k * jnp.exp(ka)).astype(BF16))
        att = jnp.where(masks[w], sc, att)
        w //= 2
    if local:
        return att
    t = lax.broadcasted_iota(jnp.int32, (n, 1), 0)
    back = [None] + [pltpu.roll(lf, i, 0) for i in range(1, 4)]
    fwd = [None] + [pltpu.roll(lf, n - i, 0) for i in range(1, 4)]
    while w >= 1:
        r = t & (w - 1)
        qa = lf
        ke = k
        if w > 1:
            ka = jnp.where(r <= w - 2, fwd[1], 0.0)
            for i in range(1, w):
                qa = qa + jnp.where(r >= i, back[i], 0.0)
            for i in range(2, w):
                ka = ka + jnp.where(r <= w - 1 - i, fwd[i], 0.0)
            ke = k * jnp.exp(ka)
        sc = _dot_nt((q * jnp.exp(qa)).astype(BF16), ke.astype(BF16))
        att = jnp.where(masks[w], sc, att)
        w //= 2
    return att


def _hgrn2_kernel(x_ref, mod_ref, g_ref, win_ref, alb_ref, on_ref, wout_ref, o_ref,
                  st_ref, q_ref, k_ref, lf_ref, b_ref, v_ref, gt_ref, oall_ref, *, idx, sub):
    chunk, d = x_ref.shape[1], x_ref.shape[2]
    hw = HEAD_W

    @pl.when(pl.program_id(1) == 0)
    def _():
        st_ref[...] = jnp.zeros_like(st_ref)

    x = x_ref[0]
    mod = mod_ref[0]
    xn = _prenorm(x, g_ref[...], mod).astype(BF16)

    alb = alb_ref[...]
    e = jnp.exp(alb - jnp.max(alb, axis=0, keepdims=True))
    sm = e / jnp.sum(e, axis=0, keepdims=True)
    lb = jnp.sum(sm[0:idx + 1], axis=0, keepdims=True) - sm[0:1]

    q_ref[...] = _silu(_dot(xn, win_ref[:, 0:d]))
    f = lb + (1.0 - lb) * jax.nn.sigmoid(_dot(xn, win_ref[:, d:2 * d]))
    k_ref[...] = 1.0 - f
    v_ref[...] = _dot(xn, win_ref[:, 2 * d:3 * d])
    gt_ref[...] = _silu(_dot(xn, win_ref[:, 3 * d:4 * d]))

    lf = jnp.log(f)
    lf_ref[...] = lf
    hi = lf.astype(BF16)
    r1 = lf - hi.astype(F32)
    mid = r1.astype(BF16)
    lo = (r1 - mid.astype(F32)).astype(BF16)
    row = lax.broadcasted_iota(jnp.int32, (chunk, chunk), 0)
    col = lax.broadcasted_iota(jnp.int32, (chunk, chunk), 1)
    tri = (row >= col).astype(BF16)
    b_ref[...] = _dot(tri, hi) + _dot(tri, mid) + _dot(tri, lo)

    masks = _block_masks(sub, A_LOCAL)
    nsub = chunk // sub

    def heads(local):
        staged = []
        for h in range(d // hw):
            hs = slice(h * hw, (h + 1) * hw)
            q, k, lf, b = q_ref[:, hs], k_ref[:, hs], lf_ref[:, hs], b_ref[:, hs]
            inter = _dot_nt((q * jnp.exp(b)).astype(BF16), st_ref[h].astype(BF16))
            atts, crosses = [], []
            for i in range(nsub):
                rs = slice(i * sub, (i + 1) * sub)
                atts.append(_block_scores(q[rs], k[rs], lf[rs], b[rs], masks, local).astype(BF16))
                if i > 0:
                    edge = b[i * sub - 1:i * sub, :]
                    qc = (q[rs] * jnp.exp(b[rs] - edge)).astype(BF16)
                    for j in range(i):
                        js = slice(j * sub, (j + 1) * sub)
                        kc = (k[js] * jnp.exp(edge - b[js])).astype(BF16)
                        crosses.append((i, j, _dot_nt(qc, kc).astype(BF16)))
            staged.append((inter, atts, crosses))
        for h in range(d // hw):
            hs = slice(h * hw, (h + 1) * hw)
            inter, atts, crosses = staged[h]
            k, b = k_ref[:, hs], b_ref[:, hs]
            vb = v_ref[:, hs].astype(BF16)
            outs = [inter[i * sub:(i + 1) * sub] + _dot(atts[i], vb[i * sub:(i + 1) * sub]) for i in range(nsub)]
            for i, j, cs in crosses:
                outs[i] = outs[i] + _dot(cs, vb[j * sub:(j + 1) * sub])
            o = jnp.concatenate(outs, axis=0)

            bl = b[chunk - 1:chunk, :]
            kt = k * jnp.exp(bl - b)
            st_ref[h] = st_ref[h] * jnp.exp(bl) + _dot_tn(vb, kt.astype(BF16))

            ms = jnp.mean(o * o, axis=-1, keepdims=True)
            on = o * lax.rsqrt(ms + EPS) * on_ref[...]
            oall_ref[:, hs] = (on * gt_ref[:, hs]).astype(BF16)

    b3 = b_ref[...].reshape(chunk // A_LOCAL, A_LOCAL, d)
    safe = jnp.max(b3[:, 0:1, :] - b3[:, A_LOCAL - 1:A_LOCAL, :]) <= A_SAFE

    @pl.when(safe)
    def _():
        heads(A_LOCAL)

    @pl.when(jnp.logical_not(safe))
    def _():
        heads(None)

    y = _dot(oall_ref[...], wout_ref[...])
    o_ref[0] = x + (1.0 + mod[2:3]) * y


def _hgrn2(x, mod, g, w_in, a_lb, onorm, w_out, idx):
    bn, s, d = x.shape
    chunk = A_CHUNK
    nh = d // HEAD_W
    act = pltpu.VMEM((chunk, d), F32)
    return pl.pallas_call(
        functools.partial(_hgrn2_kernel, idx=idx, sub=A_SUB),
        grid=(bn, s // chunk),
        in_specs=[
            pl.BlockSpec((1, chunk, d), lambda b, i: (b, i, 0)),
            pl.BlockSpec((1, 3, d), lambda b, i: (b, 0, 0)),
            _resident((1, d)),
            _resident_at((d, 4 * d), (idx,)),
            _resident(a_lb.shape),
            _resident((1, HEAD_W)),
            _resident_at((d, d), (idx,)),
        ],
        out_specs=pl.BlockSpec((1, chunk, d), lambda b, i: (b, i, 0)),
        out_shape=jax.ShapeDtypeStruct(x.shape, x.dtype),
        scratch_shapes=[pltpu.VMEM((nh, HEAD_W, HEAD_W), F32), act, act, act, act, act, act,
                        pltpu.VMEM((chunk, d), BF16)],
        compiler_params=_params(("arbitrary", "arbitrary"), 48),
        name="hgrn2",
    )(x, mod, g, w_in, a_lb, onorm, w_out)


def _conv_kernel(x_ref, mod_ref, g_ref, win_ref, cw_ref, wout_ref, o_ref, carry_ref):
    tm, d = x_ref.shape[1], x_ref.shape[2]

    @pl.when(pl.program_id(1) == 0)
    def _():
        carry_ref[...] = jnp.zeros_like(carry_ref)

    x = x_ref[0]
    mod = mod_ref[0]
    xn = _prenorm(x, g_ref[...], mod).astype(BF16)
    u = _dot(xn, win_ref[:, d:2 * d]) * _dot(xn, win_ref[:, 2 * d:3 * d])
    carry = carry_ref[...]
    row = lax.broadcasted_iota(jnp.int32, (tm, 1), 0)
    u1 = jnp.where(row == 0, carry[7:8], pltpu.roll(u, 1, 0))
    u2 = jnp.where(row == 0, carry[6:7], jnp.where(row == 1, carry[7:8], pltpu.roll(u, 2, 0)))
    cw = cw_ref[...]
    y = cw[0:1] * u2 + cw[1:2] * u1 + cw[2:3] * u
    carry_ref[...] = u[tm - 8:tm]
    z = (_dot(xn, win_ref[:, 0:d]) * y).astype(BF16)
    o_ref[0] = x + (1.0 + mod[2:3]) * _dot(z, wout_ref[...])


def _short_conv(x, mod, g, w_in, conv_w, w_out):
    bn, s, d = x.shape
    tm = MIX_TM
    return pl.pallas_call(
        _conv_kernel,
        grid=(bn, s // tm),
        in_specs=[
            pl.BlockSpec((1, tm, d), lambda b, i: (b, i, 0)),
            pl.BlockSpec((1, 3, d), lambda b, i: (b, 0, 0)),
            _resident((1, d)),
            _resident((d, 3 * d)),
            _resident(conv_w.shape),
            _resident((d, d)),
        ],
        out_specs=pl.BlockSpec((1, tm, d), lambda b, i: (b, i, 0)),
        out_shape=jax.ShapeDtypeStruct(x.shape, x.dtype),
        scratch_shapes=[pltpu.VMEM((8, d), F32)],
        compiler_params=_params(("arbitrary", "arbitrary"), 48),
        name="short_conv",
    )(x, mod, g, w_in, conv_w, w_out)


def _qkv_kernel(x_ref, mod_ref, g_ref, win_ref, gq_ref, gk_ref, bd_ref, pos_ref, invf_ref,
                q_ref, k_ref, v_ref):
    d = x_ref.shape[2]
    hw = HEAD_W
    xn = _prenorm(x_ref[0], g_ref[...], mod_ref[0]).astype(BF16)

    ang = pos_ref[0].astype(F32) * invf_ref[...]
    lane = lax.broadcasted_iota(jnp.int32, (1, hw), 1) & (B_DH - 1)
    cs, sn = jnp.cos(ang), jnp.sin(ang)
    ca = jnp.where(lane < B_ROT, cs, 1.0)
    cm = jnp.where(lane < B_ROT // 2, -sn, 0.0)
    cp = jnp.where((lane >= B_ROT // 2) & (lane < B_ROT), sn, 0.0)
    half = B_ROT // 2

    gw = 2 * hw
    for off, gain_ref, is_q in ((0, gq_ref, True), (d, gk_ref, False)):
        for j in range(d // gw):
            z = _dot(xn, win_ref[:, off + j * gw:off + (j + 1) * gw])
            ss = _dot((z * z).astype(BF16), bd_ref[...])
            zn = z * lax.rsqrt(ss * (1.0 / B_DH) + EPS) * gain_ref[:, j * gw:(j + 1) * gw]
            for t in range(gw // hw):
                zz = zn[:, t * hw:(t + 1) * hw]
                r = zz * ca + pltpu.roll(zz, hw - half, 1) * cm + pltpu.roll(zz, half, 1) * cp
                head = j * (gw // hw) + t
                if is_q:
                    q_ref[0, head] = (r * B_DH ** -0.5).astype(BF16)
                else:
                    k_ref[0, head, 0] = r.T.astype(BF16)
    v = _dot(xn, win_ref[:, 2 * d:3 * d]).astype(BF16)
    for head in range(d // hw):
        v_ref[0, head] = v[:, head * hw:(head + 1) * hw]


def _qkv(x, mod, g, w_in, qk_g, positions):
    bn, s, d = x.shape
    tm = MIX_TM
    reps = d // B_DH
    gq = jnp.tile(qk_g[0].astype(F32), reps)[None, :]
    gk = jnp.tile(qk_g[1].astype(F32), reps)[None, :]
    blk = jnp.arange(2 * HEAD_W) // B_DH
    bd = (blk[:, None] == blk[None, :]).astype(BF16)
    inv_freq = ROPE_THETA ** (-jnp.arange(0, B_ROT, 2, dtype=F32) / B_ROT)
    lane = jnp.arange(HEAD_W) % B_DH
    invf = jnp.where(lane < B_ROT, inv_freq[lane % (B_ROT // 2)], 0.0)[None, :]
    tok = pl.BlockSpec((1, tm, d), lambda b, i: (b, i, 0))
    nh = d // HEAD_W
    hm_spec = pl.BlockSpec((1, nh, tm, HEAD_W), lambda b, i: (b, 0, i, 0))
    hm_out = jax.ShapeDtypeStruct((bn, nh, s, HEAD_W), BF16)
    kt_spec = pl.BlockSpec((1, nh, 1, HEAD_W, tm), lambda b, i: (b, 0, i, 0, 0))
    kt_out = jax.ShapeDtypeStruct((bn, nh, s // tm, HEAD_W, tm), BF16)
    return pl.pallas_call(
        _qkv_kernel,
        grid=(bn, s // tm),
        in_specs=[
            tok,
            pl.BlockSpec((1, 3, d), lambda b, i: (b, 0, 0)),
            _resident((1, d)),
            _resident((d, 3 * d)),
            _resident((1, d)),
            _resident((1, d)),
            _resident((2 * HEAD_W, 2 * HEAD_W)),
            pl.BlockSpec((1, tm, 1), lambda b, i: (b, i, 0)),
            _resident((1, HEAD_W)),
        ],
        out_specs=[hm_spec, kt_spec, hm_spec],
        out_shape=[hm_out, kt_out, hm_out],
        compiler_params=_params(("arbitrary", "arbitrary"), 48),
        name="attn_qkv",
    )(x, mod, g, w_in, gq, gk, bd, positions.reshape(bn, s, 1), invf)


def _attn_kernel(q_ref, kt_ref, v_ref, lam_ref, sg_ref, o_ref, qs_ref, s_ref, m_ref, acc_ref, *,
                 lambda_init, rows):
    t = q_ref.shape[2]
    hw = HEAD_W
    i = pl.program_id(2)
    q = q_ref[0, 0]
    lane = lax.broadcasted_iota(jnp.int32, (1, hw), 1)
    qs_ref[0] = jnp.where(lane < B_DH, q, jnp.zeros_like(q))
    qs_ref[1] = jnp.where(lane >= B_DH, q, jnp.zeros_like(q))
    m_ref[...] = jnp.full_like(m_ref, -jnp.inf)
    acc_ref[...] = jnp.zeros_like(acc_ref)
    ones = jnp.ones((t, hw), BF16)
    groups = [(c, rc) for rc in range(t // rows) for c in range(2)]

    def scores_to(slot, j):
        per = t // kt_ref.shape[-1]
        kt = jnp.concatenate([kt_ref[0, 0, j * per + a] for a in range(per)], axis=1)
        for c, rc in groups:
            rs = slice(rc * rows, (rc + 1) * rows)
            s_ref[slot, c, rs, :] = _dot(qs_ref[c, rs, :], kt)

    def softmax_pv(slot, j, causal):
        for c, rc in groups:
            rs = slice(rc * rows, (rc + 1) * rows)
            nk = hw * pl.cdiv((rc + 1) * rows, hw) if causal else t
            sc = s_ref[slot, c, rs, :nk]
            if causal:
                r = rc * rows + lax.broadcasted_iota(jnp.int32, (rows, nk), 0)
                cc = lax.broadcasted_iota(jnp.int32, (rows, nk), 1)
                sc = jnp.where(r >= cc, sc, -jnp.inf)
            cols = [sc[:, a * hw:(a + 1) * hw] for a in range(nk // hw)]
            rmax = jnp.max(functools.reduce(jnp.maximum, cols), axis=-1, keepdims=True)
            m_old = m_ref[c, rs, :]
            m_new = jnp.maximum(m_old, rmax)
            p = jnp.concatenate([jnp.exp(cl - m_new) for cl in cols], axis=1).astype(BF16)
            vx = jnp.concatenate([v_ref[0, 0, pl.ds(j * t, nk), :], ones[:nk]], axis=1)
            alpha = jnp.exp(m_old - m_new)
            acc_ref[c, rs, :] = acc_ref[c, rs, :] * jnp.concatenate([alpha, alpha], axis=1) + _dot(p, vx)
            m_ref[c, rs, :] = m_new

    scores_to(0, 0)

    def pair(p, carry):
        scores_to(1, 2 * p + 1)
        softmax_pv(0, 2 * p, False)
        scores_to(0, 2 * p + 2)
        softmax_pv(1, 2 * p + 1, False)
        return carry

    lax.fori_loop(0, i // 2, pair, 0)

    @pl.when(i % 2 == 1)
    def _():
        scores_to(1, i)
        softmax_pv(0, i - 1, False)

    softmax_pv(i % 2, i, True)

    lp = lam_ref[...]
    lam = (jnp.exp(jnp.sum(lp[0:1] * lp[1:2], axis=-1, keepdims=True))
           - jnp.exp(jnp.sum(lp[2:3] * lp[3:4], axis=-1, keepdims=True)) + lambda_init)
    a0, a1 = acc_ref[0], acc_ref[1]
    o = a0[:, :hw] / a0[:, hw:] - lam * (a1[:, :hw] / a1[:, hw:])
    ms = jnp.mean(o * o, axis=-1, keepdims=True)
    o_ref[0, 0] = (o * lax.rsqrt(ms + EPS) * sg_ref[...] * (1.0 - lambda_init)).astype(o_ref.dtype)


def _attention(q, kt, v, lam_p, subln, lambda_init):
    bn, nh, s, _ = q.shape
    t = ATT_T
    kb = kt.shape[-1]
    assert kt.shape == (bn, nh, s // kb, HEAD_W, kb) and t % kb == 0
    tile = pl.BlockSpec((1, 1, t, HEAD_W), lambda b, h, i: (b, h, i, 0))
    return pl.pallas_call(
        functools.partial(_attn_kernel, lambda_init=lambda_init, rows=ATT_R),
        grid=(bn, nh, s // t),
        in_specs=[
            tile,
            pl.BlockSpec((1, 1, s // kb, HEAD_W, kb), lambda b, h, i: (b, h, 0, 0, 0)),
            pl.BlockSpec((1, 1, s, HEAD_W), lambda b, h, i: (b, h, 0, 0)),
            _resident(lam_p.shape),
            _resident((1, HEAD_W)),
        ],
        out_specs=tile,
        out_shape=jax.ShapeDtypeStruct((bn, nh, s, HEAD_W), BF16),
        scratch_shapes=[pltpu.VMEM((2, t, HEAD_W), BF16), pltpu.VMEM((2, 2, t, t), F32),
                        pltpu.VMEM((2, t, HEAD_W), F32),
                        pltpu.VMEM((2, t, 2 * HEAD_W), F32)],
        compiler_params=_params(("arbitrary", "arbitrary", "arbitrary"), 56),
        name="attn_core",
    )(q, kt, v, lam_p, subln)


def _outproj_kernel(a_ref, x_ref, mod_ref, w_ref, o_ref):
    a = jnp.concatenate([a_ref[0, h] for h in range(a_ref.shape[1])], axis=1)
    o_ref[0] = x_ref[0] + (1.0 + mod_ref[0][2:3]) * _dot(a, w_ref[...])


def _outproj(a, x, mod, w_out):
    bn, s, d = x.shape
    tm = MIX_TM
    tok = pl.BlockSpec((1, tm, d), lambda b, i: (b, i, 0))
    heads = pl.BlockSpec((1, d // HEAD_W, tm, HEAD_W), lambda b, i: (b, 0, i, 0))
    return pl.pallas_call(
        _outproj_kernel,
        grid=(bn, s // tm),
        in_specs=[heads, tok, pl.BlockSpec((1, 3, d), lambda b, i: (b, 0, 0)), _resident((d, d))],
        out_specs=tok,
        out_shape=jax.ShapeDtypeStruct(x.shape, x.dtype),
        compiler_params=_params(("arbitrary", "arbitrary"), 32),
        name="attn_out",
    )(a, x, mod, w_out)


def kernel(x, c, positions, ada_w, ada_b, norm_g, ffn_wi, ffn_wo, a_w_in, a_w_out, a_lb, a_onorm,
           b_w_in, b_w_out, b_qk_g, b_lam, b_subln, c_w_in, c_conv, c_w_out):
    bn, s, d = x.shape
    depth = ada_w.shape[0]
    mod = _adaln(c, ada_w, ada_b).reshape(depth, bn, N_SUB, 3, d)
    wi, wo = ffn_wi.astype(BF16), ffn_wo.astype(BF16)
    a_in, a_out = a_w_in.astype(BF16), a_w_out.astype(BF16)

    for l in range(depth):
        def sub(j):
            return mod[l, :, j], norm_g[l, j][None, :]

        m, g = sub(0)
        x = _ffn(x, m, g, wi, wo, (l, 0))
        m, g = sub(1)
        kind, idx = l % N_MIXERS, l // N_MIXERS
        if kind == 0:
            x = _hgrn2(x, m, g, a_in, a_lb.astype(F32), a_onorm[idx][None, :].astype(F32), a_out, idx)
        elif kind == 1:
            lambda_init = 0.8 - 0.6 * math.exp(-0.3 * l)
            q, k, v = _qkv(x, m, g, b_w_in[idx].astype(BF16), b_qk_g[idx], positions)
            a = _attention(q, k, v, b_lam[idx].astype(F32), b_subln[idx][None, :].astype(F32), lambda_init)
            x = _outproj(a, x, m, b_w_out[idx].astype(BF16))
        else:
            x = _short_conv(x, m, g, c_w_in[idx].astype(BF16), c_conv[idx].astype(F32), c_w_out[idx].astype(BF16))
        m, g = sub(2)
        x = _ffn(x, m, g, wi, wo, (l, 1))
    return x
```

```python
import functools
import math

import jax
import jax.numpy as jnp
from jax import lax
from jax.experimental import pallas as pl
from jax.experimental.pallas import tpu as pltpu

F32 = jnp.float32
BF16 = jnp.bfloat16

EPS = 1e-6
N_SUB = 3
N_MIXERS = 3
HEAD_W = 128
N_HEADS = 8
B_DH = 64
B_ROT = 16
ROPE_THETA = 500000.0
MIB = 1024 * 1024

FFN_TM = 1024
FFN_PARTS = 4
FFN_FC = 256
MIX_TM = 512
A_CHUNK = 256
A_SUB = 128
A_LOCAL = 32
A_SAFE = 60.0
ATT_T = 1024
ATT_R = 256


def _silu(z):
    return z * jax.nn.sigmoid(z)


def _dot(a, b):
    return jnp.dot(a, b, preferred_element_type=F32)


def _dot_nt(a, b):
    return lax.dot_general(a, b, (((1,), (1,)), ((), ())), preferred_element_type=F32)


def _dot_tn(a, b):
    return lax.dot_general(a, b, (((0,), (0,)), ((), ())), preferred_element_type=F32)


def _prenorm(x, g, mod):
    ms = jnp.mean(x * x, axis=-1, keepdims=True)
    y = x * lax.rsqrt(ms + EPS) * g
    return y * (1.0 + mod[1:2]) + mod[0:1]


def _resident(shape):
    nd = len(shape)
    return pl.BlockSpec(shape, lambda *_: (0,) * nd, pipeline_mode=pl.Buffered(1))


def _resident_at(shape, lead):
    nd = len(shape)
    return pl.BlockSpec((None,) * len(lead) + tuple(shape), lambda *_: tuple(lead) + (0,) * nd,
                        pipeline_mode=pl.Buffered(1))


def _params(sem, vmem_mib):
    return pltpu.CompilerParams(dimension_semantics=sem, vmem_limit_bytes=vmem_mib * MIB)


def _adaln_kernel(c_ref, w_ref, b_ref, o_ref):
    ca = _silu(c_ref[...]).astype(BF16)
    o_ref[0] = _dot(ca, w_ref[0].astype(BF16)) + b_ref[0]


def _adaln(c, ada_w, ada_b):
    depth, d, n = ada_w.shape
    bn = c.shape[0]
    rows = 8 * pl.cdiv(bn, 8)
    cp = jnp.zeros((rows, d), F32).at[:bn].set(c)
    tn = n // 6
    out = pl.pallas_call(
        _adaln_kernel,
        grid=(depth, n // tn),
        in_specs=[
            pl.BlockSpec((rows, d), lambda l, j: (0, 0)),
            pl.BlockSpec((1, d, tn), lambda l, j: (l, 0, j)),
            pl.BlockSpec((1, 1, tn), lambda l, j: (l, 0, j)),
        ],
        out_specs=pl.BlockSpec((1, rows, tn), lambda l, j: (l, 0, j)),
        out_shape=jax.ShapeDtypeStruct((depth, rows, n), F32),
        compiler_params=_params(("arbitrary", "arbitrary"), 40),
        name="adaln",
    )(cp, ada_w, ada_b.reshape(depth, 1, n))
    return out[:, :bn]


def _ffn_kernel(x_ref, mod_ref, g_ref, wi_ref, wo_ref, o_ref, h_ref, *, fc, parts):
    mod = mod_ref[0]
    ff = h_ref.shape[1]
    rows = x_ref.shape[1] // parts

    def hidden(p):
        rs = slice(p * rows, (p + 1) * rows)
        xn = _prenorm(x_ref[0, rs, :], g_ref[...], mod).astype(BF16)
        for c in range(ff // fc):
            gate = _dot(xn, wi_ref[:, c * fc:(c + 1) * fc])
            up = _dot(xn, wi_ref[:, ff + c * fc:ff + (c + 1) * fc])
            h_ref[rs, c * fc:(c + 1) * fc] = (_silu(gate) * up).astype(BF16)

    def project(p):
        rs = slice(p * rows, (p + 1) * rows)
        y = _dot(h_ref[rs, :], wo_ref[...])
        o_ref[0, rs, :] = x_ref[0, rs, :] + (0.5 * (1.0 + mod[2:3])) * y

    hidden(0)
    for p in range(1, parts):
        hidden(p)
        project(p - 1)
    project(parts - 1)


def _ffn(x, mod, g, wi, wo, which):
    bn, s, d = x.shape
    ff = wo.shape[-2]
    tm, fc = FFN_TM, FFN_FC
    return pl.pallas_call(
        functools.partial(_ffn_kernel, fc=fc, parts=FFN_PARTS),
        grid=(bn, s // tm),
        in_specs=[
            pl.BlockSpec((1, tm, d), lambda b, i: (b, i, 0)),
            pl.BlockSpec((1, 3, d), lambda b, i: (b, 0, 0)),
            _resident((1, d)),
            _resident_at((d, 2 * ff), which),
            _resident_at((ff, d), which),
        ],
        out_specs=pl.BlockSpec((1, tm, d), lambda b, i: (b, i, 0)),
        out_shape=jax.ShapeDtypeStruct(x.shape, x.dtype),
        scratch_shapes=[pltpu.VMEM((tm, ff), BF16)],
        compiler_params=_params(("arbitrary", "arbitrary"), 56),
        name="ffn",
    )(x, mod, g, wi, wo)


def _block_masks(n, local):
    row = lax.broadcasted_iota(jnp.int32, (n, n), 0)
    col = lax.broadcasted_iota(jnp.int32, (n, n), 1)
    sh = int(math.log2(local))
    masks = {"diag": row == col, "local": ((row >> sh) == (col >> sh)) & (row >= col)}
    w = n // 2
    while w >= 1:
        sh = int(math.log2(w))
        rb, cb = row >> sh, col >> sh
        masks[w] = (rb == cb + 1) & ((cb & 1) == 0)
        w //= 2
    return masks


def _block_scores(q, k, lf, b, masks, local):
    n, hw = q.shape
    if local:
        b3 = b.reshape(n // local, local, hw)
        first = b3[:, 0:1, :]
        qe = (q * jnp.exp(b3 - first).reshape(n, hw)).astype(BF16)
        ke = (k * jnp.exp(first - b3).reshape(n, hw)).astype(BF16)
        att = jnp.where(masks["local"], _dot_nt(qe, ke), 0.0)
    else:
        att = jnp.where(masks["diag"], _dot_nt(q.astype(BF16), k.astype(BF16)), 0.0)
    w = n // 2
    while w >= (local or 8):
        b3 = b.reshape(n // w, w, hw)
        last = b3[:, w - 1:w, :]
        prev = jnp.concatenate([last[:1], last[:-1]], axis=0)
        qa = jnp.minimum(b3 - prev, 0.0).reshape(n, hw)
        ka = (last - b3).reshape(n, hw)
        sc = _dot_nt((q * jnp.exp(qa)).astype(BF16), (k * jnp.exp(ka)).astype(BF16))
        att = jnp.where(masks[w], sc, att)
        w //= 2
    if local:
        return att
    t = lax.broadcasted_iota(jnp.int32, (n, 1), 0)
    back = [None] + [pltpu.roll(lf, i, 0) for i in range(1, 4)]
    fwd = [None] + [pltpu.roll(lf, n - i, 0) for i in range(1, 4)]
    while w >= 1:
        r = t & (w - 1)
        qa = lf
        ke = k
        if w > 1:
            ka = jnp.where(r <= w - 2, fwd[1], 0.0)
            for i in range(1, w):
                qa = qa + jnp.where(r >= i, back[i], 0.0)
            for i in range(2, w):
                ka = ka + jnp.where(r <= w - 1 - i, fwd[i], 0.0)
            ke = k * jnp.exp(ka)
        sc = _dot_nt((q * jnp.exp(qa)).astype(BF16), ke.astype(BF16))
        att = jnp.where(masks[w], sc, att)
        w //= 2
    return att


def _hgrn2_kernel(x_ref, mod_ref, g_ref, win_ref, alb_ref, on_ref, wout_ref, o_ref,
                  st_ref, q_ref, k_ref, lf_ref, b_ref, v_ref, gt_ref, oall_ref, *, idx, sub):
    chunk, d = x_ref.shape[1], x_ref.shape[2]
    hw = HEAD_W

    @pl.when(pl.program_id(1) == 0)
    def _():
        st_ref[...] = jnp.zeros_like(st_ref)

    x = x_ref[0]
    mod = mod_ref[0]
    xn = _prenorm(x, g_ref[...], mod).astype(BF16)

    alb = alb_ref[...]
    e = jnp.exp(alb - jnp.max(alb, axis=0, keepdims=True))
    sm = e / jnp.sum(e, axis=0, keepdims=True)
    lb = jnp.sum(sm[0:idx + 1], axis=0, keepdims=True) - sm[0:1]

    q_ref[...] = _silu(_dot(xn, win_ref[:, 0:d]))
    f = lb + (1.0 - lb) * jax.nn.sigmoid(_dot(xn, win_ref[:, d:2 * d]))
    k_ref[...] = 1.0 - f
    v_ref[...] = _dot(xn, win_ref[:, 2 * d:3 * d])
    gt_ref[...] = _silu(_dot(xn, win_ref[:, 3 * d:4 * d]))

    lf = jnp.log(f)
    lf_ref[...] = lf
    hi = lf.astype(BF16)
    r1 = lf - hi.astype(F32)
    mid = r1.astype(BF16)
    lo = (r1 - mid.astype(F32)).astype(BF16)
    row = lax.broadcasted_iota(jnp.int32, (chunk, chunk), 0)
    col = lax.broadcasted_iota(jnp.int32, (chunk, chunk), 1)
    tri = (row >= col).astype(BF16)
    b_ref[...] = _dot(tri, hi) + _dot(tri, mid) + _dot(tri, lo)

    masks = _block_masks(sub, A_LOCAL)
    nsub = chunk // sub

    def heads(local):
        staged = []
        for h in range(d // hw):
            hs = slice(h * hw, (h + 1) * hw)
            q, k, lf, b = q_ref[:, hs], k_ref[:, hs], lf_ref[:, hs], b_ref[:, hs]
            inter = _dot_nt((q * jnp.exp(b)).astype(BF16), st_ref[h].astype(BF16))
            atts, crosses = [], []
            for i in range(nsub):
                rs = slice(i * sub, (i + 1) * sub)
                atts.append(_block_scores(q[rs], k[rs], lf[rs], b[rs], masks, local).astype(BF16))
                if i > 0:
                    edge = b[i * sub - 1:i * sub, :]
                    qc = (q[rs] * jnp.exp(b[rs] - edge)).astype(BF16)
                    for j in range(i):
                        js = slice(j * sub, (j + 1) * sub)
                        kc = (k[js] * jnp.exp(edge - b[js])).astype(BF16)
                        crosses.append((i, j, _dot_nt(qc, kc).astype(BF16)))
            staged.append((inter, atts, crosses))
        for h in range(d // hw):
            hs = slice(h * hw, (h + 1) * hw)
            inter, atts, crosses = staged[h]
            k, b = k_ref[:, hs], b_ref[:, hs]
            vb = v_ref[:, hs].astype(BF16)
            outs = [inter[i * sub:(i + 1) * sub] + _dot(atts[i], vb[i * sub:(i + 1) * sub]) for i in range(nsub)]
            for i, j, cs in crosses:
                outs[i] = outs[i] + _dot(cs, vb[j * sub:(j + 1) * sub])
            o = jnp.concatenate(outs, axis=0)

            bl = b[chunk - 1:chunk, :]
            kt = k * jnp.exp(bl - b)
            st_ref[h] = st_ref[h] * jnp.exp(bl) + _dot_tn(vb, kt.astype(BF16))

            ms = jnp.mean(o * o, axis=-1, keepdims=True)
            on = o * lax.rsqrt(ms + EPS) * on_ref[...]
            oall_ref[:, hs] = (on * gt_ref[:, hs]).astype(BF16)

    b3 = b_ref[...].reshape(chunk // A_LOCAL, A_LOCAL, d)
    safe = jnp.max(b3[:, 0:1, :] - b3[:, A_LOCAL - 1:A_LOCAL, :]) <= A_SAFE

    @pl.when(safe)
    def _():
        heads(A_LOCAL)

    @pl.when(jnp.logical_not(safe))
    def _():
        heads(None)

    y = _dot(oall_ref[...], wout_ref[...])
    o_ref[0] = x + (1.0 + mod[2:3]) * y


def _hgrn2(x, mod, g, w_in, a_lb, onorm, w_out, idx):
    bn, s, d = x.shape
    chunk = A_CHUNK
    nh = d // HEAD_W
    act = pltpu.VMEM((chunk, d), F32)
    return pl.pallas_call(
        functools.partial(_hgrn2_kernel, idx=idx, sub=A_SUB),
        grid=(bn, s // chunk),
        in_specs=[
            pl.BlockSpec((1, chunk, d), lambda b, i: (b, i, 0)),
            pl.BlockSpec((1, 3, d), lambda b, i: (b, 0, 0)),
            _resident((1, d)),
            _resident_at((d, 4 * d), (idx,)),
            _resident(a_lb.shape),
            _resident((1, HEAD_W)),
            _resident_at((d, d), (idx,)),
        ],
        out_specs=pl.BlockSpec((1, chunk, d), lambda b, i: (b, i, 0)),
        out_shape=jax.ShapeDtypeStruct(x.shape, x.dtype),
        scratch_shapes=[pltpu.VMEM((nh, HEAD_W, HEAD_W), F32), act, act, act, act, act, act,
                        pltpu.VMEM((chunk, d), BF16)],
        compiler_params=_params(("arbitrary", "arbitrary"), 48),
        name="hgrn2",
    )(x, mod, g, w_in, a_lb, onorm, w_out)


def _conv_kernel(x_ref, mod_ref, g_ref, win_ref, cw_ref, wout_ref, o_ref, carry_ref):
    tm, d = x_ref.shape[1], x_ref.shape[2]

    @pl.when(pl.program_id(1) == 0)
    def _():
        carry_ref[...] = jnp.zeros_like(carry_ref)

    x = x_ref[0]
    mod = mod_ref[0]
    xn = _prenorm(x, g_ref[...], mod).astype(BF16)
    u = _dot(xn, win_ref[:, d:2 * d]) * _dot(xn, win_ref[:, 2 * d:3 * d])
    carry = carry_ref[...]
    row = lax.broadcasted_iota(jnp.int32, (tm, 1), 0)
    u1 = jnp.where(row == 0, carry[7:8], pltpu.roll(u, 1, 0))
    u2 = jnp.where(row == 0, carry[6:7], jnp.where(row == 1, carry[7:8], pltpu.roll(u, 2, 0)))
    cw = cw_ref[...]
    y = cw[0:1] * u2 + cw[1:2] * u1 + cw[2:3] * u
    carry_ref[...] = u[tm - 8:tm]
    z = (_dot(xn, win_ref[:, 0:d]) * y).astype(BF16)
    o_ref[0] = x + (1.0 + mod[2:3]) * _dot(z, wout_ref[...])


def _short_conv(x, mod, g, w_in, conv_w, w_out):
    bn, s, d = x.shape
    tm = MIX_TM
    return pl.pallas_call(
        _conv_kernel,
        grid=(bn, s // tm),
        in_specs=[
            pl.BlockSpec((1, tm, d), lambda b, i: (b, i, 0)),
            pl.BlockSpec((1, 3, d), lambda b, i: (b, 0, 0)),
            _resident((1, d)),
            _resident((d, 3 * d)),
            _resident(conv_w.shape),
            _resident((d, d)),
        ],
        out_specs=pl.BlockSpec((1, tm, d), lambda b, i: (b, i, 0)),
        out_shape=jax.ShapeDtypeStruct(x.shape, x.dtype),
        scratch_shapes=[pltpu.VMEM((8, d), F32)],
        compiler_params=_params(("arbitrary", "arbitrary"), 48),
        name="short_conv",
    )(x, mod, g, w_in, conv_w, w_out)


def _qkv_kernel(x_ref, mod_ref, g_ref, win_ref, gq_ref, gk_ref, bd_ref, pos_ref, invf_ref,
                q_ref, k_ref, v_ref):
    d = x_ref.shape[2]
    hw = HEAD_W
    xn = _prenorm(x_ref[0], g_ref[...], mod_ref[0]).astype(BF16)

    ang = pos_ref[0].astype(F32) * invf_ref[...]
    lane = lax.broadcasted_iota(jnp.int32, (1, hw), 1) & (B_DH - 1)
    cs, sn = jnp.cos(ang), jnp.sin(ang)
    ca = jnp.where(lane < B_ROT, cs, 1.0)
    cm = jnp.where(lane < B_ROT // 2, -sn, 0.0)
    cp = jnp.where((lane >= B_ROT // 2) & (lane < B_ROT), sn, 0.0)
    half = B_ROT // 2

    gw = 2 * hw
    for off, gain_ref, is_q in ((0, gq_ref, True), (d, gk_ref, False)):
        for j in range(d // gw):
            z = _dot(xn, win_ref[:, off + j * gw:off + (j + 1) * gw])
            ss = _dot((z * z).astype(BF16), bd_ref[...])
            zn = z * lax.rsqrt(ss * (1.0 / B_DH) + EPS) * gain_ref[:, j * gw:(j + 1) * gw]
            for t in range(gw // hw):
                zz = zn[:, t * hw:(t + 1) * hw]
                r = zz * ca + pltpu.roll(zz, hw - half, 1) * cm + pltpu.roll(zz, half, 1) * cp
                head = j * (gw // hw) + t
                if is_q:
                    q_ref[0, head] = (r * B_DH ** -0.5).astype(BF16)
                else:
                    k_ref[0, head, 0] = r.T.astype(BF16)
    v = _dot(xn, win_ref[:, 2 * d:3 * d]).astype(BF16)
    for head in range(d // hw):
        v_ref[0, head] = v[:, head * hw:(head + 1) * hw]


def _qkv(x, mod, g, w_in, qk_g, positions):
    bn, s, d = x.shape
    tm = MIX_TM
    reps = d // B_DH
    gq = jnp.tile(qk_g[0].astype(F32), reps)[None, :]
    gk = jnp.tile(qk_g[1].astype(F32), reps)[None, :]
    blk = jnp.arange(2 * HEAD_W) // B_DH
    bd = (blk[:, None] == blk[None, :]).astype(BF16)
    inv_freq = ROPE_THETA ** (-jnp.arange(0, B_ROT, 2, dtype=F32) / B_ROT)
    lane = jnp.arange(HEAD_W) % B_DH
    invf = jnp.where(lane < B_ROT, inv_freq[lane % (B_ROT // 2)], 0.0)[None, :]
    tok = pl.BlockSpec((1, tm, d), lambda b, i: (b, i, 0))
    nh = d // HEAD_W
    hm_spec = pl.BlockSpec((1, nh, tm, HEAD_W), lambda b, i: (b, 0, i, 0))
    hm_out = jax.ShapeDtypeStruct((bn, nh, s, HEAD_W), BF16)
    kt_spec = pl.BlockSpec((1, nh, 1, HEAD_W, tm), lambda b, i: (b, 0, i, 0, 0))
    kt_out = jax.ShapeDtypeStruct((bn, nh, s // tm, HEAD_W, tm), BF16)
    return pl.pallas_call(
        _qkv_kernel,
        grid=(bn, s // tm),
        in_specs=[
            tok,
            pl.BlockSpec((1, 3, d), lambda b, i: (b, 0, 0)),
            _resident((1, d)),
            _resident((d, 3 * d)),
            _resident((1, d)),
            _resident((1, d)),
            _resident((2 * HEAD_W, 2 * HEAD_W)),
            pl.BlockSpec((1, tm, 1), lambda b, i: (b, i, 0)),
            _resident((1, HEAD_W)),
        ],
        out_specs=[hm_spec, kt_spec, hm_spec],
        out_shape=[hm_out, kt_out, hm_out],
        compiler_params=_params(("arbitrary", "arbitrary"), 48),
        name="attn_qkv",
    )(x, mod, g, w_in, gq, gk, bd, positions.reshape(bn, s, 1), invf)


def _attn_kernel(q_ref, kt_ref, v_ref, lam_ref, sg_ref, o_ref, qs_ref, s_ref, m_ref, acc_ref, *,
                 lambda_init, rows):
    t = q_ref.shape[2]
    hw = HEAD_W
    i = pl.program_id(2)
    q = q_ref[0, 0]
    lane = lax.broadcasted_iota(jnp.int32, (1, hw), 1)
    qs_ref[0] = jnp.where(lane < B_DH, q, jnp.zeros_like(q))
    qs_ref[1] = jnp.where(lane >= B_DH, q, jnp.zeros_like(q))
    m_ref[...] = jnp.full_like(m_ref, -jnp.inf)
    acc_ref[...] = jnp.zeros_like(acc_ref)
    ones = jnp.ones((t, hw), BF16)
    groups = [(c, rc) for rc in range(t // rows) for c in range(2)]

    def scores_to(slot, j):
        per = t // kt_ref.shape[-1]
        kt = jnp.concatenate([kt_ref[0, 0, j * per + a] for a in range(per)], axis=1)
        for c, rc in groups:
            rs = slice(rc * rows, (rc + 1) * rows)
            s_ref[slot, c, rs, :] = _dot(qs_ref[c, rs, :], kt)

    def softmax_pv(slot, j, causal):
        for c, rc in groups:
            rs = slice(rc * rows, (rc + 1) * rows)
            nk = hw * pl.cdiv((rc + 1) * rows, hw) if causal else t
            sc = s_ref[slot, c, rs, :nk]
            if causal:
                r = rc * rows + lax.broadcasted_iota(jnp.int32, (rows, nk), 0)
                cc = lax.broadcasted_iota(jnp.int32, (rows, nk), 1)
                sc = jnp.where(r >= cc, sc, -jnp.inf)
            cols = [sc[:, a * hw:(a + 1) * hw] for a in range(nk // hw)]
            rmax = jnp.max(functools.reduce(jnp.maximum, cols), axis=-1, keepdims=True)
            m_old = m_ref[c, rs, :]
            m_new = jnp.maximum(m_old, rmax)
            p = jnp.concatenate([jnp.exp(cl - m_new) for cl in cols], axis=1).astype(BF16)
            vx = jnp.concatenate([v_ref[0, 0, pl.ds(j * t, nk), :], ones[:nk]], axis=1)
            alpha = jnp.exp(m_old - m_new)
            acc_ref[c, rs, :] = acc_ref[c, rs, :] * jnp.concatenate([alpha, alpha], axis=1) + _dot(p, vx)
            m_ref[c, rs, :] = m_new

    scores_to(0, 0)

    def pair(p, carry):
        scores_to(1, 2 * p + 1)
        softmax_pv(0, 2 * p, False)
        scores_to(0, 2 * p + 2)
        softmax_pv(1, 2 * p + 1, False)
        return carry

    lax.fori_loop(0, i // 2, pair, 0)

    @pl.when(i % 2 == 1)
    def _():
        scores_to(1, i)
        softmax_pv(0, i - 1, False)

    softmax_pv(i % 2, i, True)

    lp = lam_ref[...]
    lam = (jnp.exp(jnp.sum(lp[0:1] * lp[1:2], axis=-1, keepdims=True))
           - jnp.exp(jnp.sum(lp[2:3] * lp[3:4], axis=-1, keepdims=True)) + lambda_init)
    a0, a1 = acc_ref[0], acc_ref[1]
    o = a0[:, :hw] / a0[:, hw:] - lam * (a1[:, :hw] / a1[:, hw:])
    ms = jnp.mean(o * o, axis=-1, keepdims=True)
    o_ref[0, 0] = (o * lax.rsqrt(ms + EPS) * sg_ref[...] * (1.0 - lambda_init)).astype(o_ref.dtype)


def _attention(q, kt, v, lam_p, subln, lambda_init):
    bn, nh, s, _ = q.shape
    t = ATT_T
    kb = kt.shape[-1]
    assert kt.shape == (bn, nh, s // kb, HEAD_W, kb) and t % kb == 0
    tile = pl.BlockSpec((1, 1, t, HEAD_W), lambda b, h, i: (b, h, i, 0))
    return pl.pallas_call(
        functools.partial(_attn_kernel, lambda_init=lambda_init, rows=ATT_R),
        grid=(bn, nh, s // t),
        in_specs=[
            tile,
            pl.BlockSpec((1, 1, s // kb, HEAD_W, kb), lambda b, h, i: (b, h, 0, 0, 0)),
            pl.BlockSpec((1, 1, s, HEAD_W), lambda b, h, i: (b, h, 0, 0)),
            _resident(lam_p.shape),
            _resident((1, HEAD_W)),
        ],
        out_specs=tile,
        out_shape=jax.ShapeDtypeStruct((bn, nh, s, HEAD_W), BF16),
        scratch_shapes=[pltpu.VMEM((2, t, HEAD_W), BF16), pltpu.VMEM((2, 2, t, t), F32),
                        pltpu.VMEM((2, t, HEAD_W), F32),
                        pltpu.VMEM((2, t, 2 * HEAD_W), F32)],
        compiler_params=_params(("arbitrary", "arbitrary", "arbitrary"), 56),
        name="attn_core",
    )(q, kt, v, lam_p, subln)


def _outproj_kernel(a_ref, x_ref, mod_ref, w_ref, o_ref):
    a = jnp.concatenate([a_ref[0, h] for h in range(a_ref.shape[1])], axis=1)
    o_ref[0] = x_ref[0] + (1.0 + mod_ref[0][2:3]) * _dot(a, w_ref[...])


def _outproj(a, x, mod, w_out):
    bn, s, d = x.shape
    tm = MIX_TM
    tok = pl.BlockSpec((1, tm, d), lambda b, i: (b, i, 0))
    heads = pl.BlockSpec((1, d // HEAD_W, tm, HEAD_W), lambda b, i: (b, 0, i, 0))
    return pl.pallas_call(
        _outproj_kernel,
        grid=(bn, s // tm),
        in_specs=[heads, tok, pl.BlockSpec((1, 3, d), lambda b, i: (b, 0, 0)), _resident((d, d))],
        out_specs=tok,
        out_shape=jax.ShapeDtypeStruct(x.shape, x.dtype),
        compiler_params=_params(("arbitrary", "arbitrary"), 32),
        name="attn_out",
    )(a, x, mod, w_out)


def kernel(x, c, positions, ada_w, ada_b, norm_g, ffn_wi, ffn_wo, a_w_in, a_w_out, a_lb, a_onorm,
           b_w_in, b_w_out, b_qk_g, b_lam, b_subln, c_w_in, c_conv, c_w_out):
    bn, s, d = x.shape
    depth = ada_w.shape[0]
    mod = _adaln(c, ada_w, ada_b).reshape(depth, bn, N_SUB, 3, d)
    wi, wo = ffn_wi.astype(BF16), ffn_wo.astype(BF16)
    a_in, a_out = a_w_in.astype(BF16), a_w_out.astype(BF16)

    for l in range(depth):
        def sub(j):
            return mod[l, :, j], norm_g[l, j][None, :]

        m, g = sub(0)
        x = _ffn(x, m, g, wi, wo, (l, 0))
        m, g = sub(1)
        kind, idx = l % N_MIXERS, l // N_MIXERS
        if kind == 0:
            x = _hgrn2(x, m, g, a_in, a_lb.astype(F32), a_onorm[idx][None, :].astype(F32), a_out, idx)
        elif kind == 1:
            lambda_init = 0.8 - 0.6 * math.exp(-0.3 * l)
            q, k, v = _qkv(x, m, g, b_w_in[idx].astype(BF16), b_qk_g[idx], positions)
            a = _attention(q, k, v, b_lam[idx].astype(F32), b_subln[idx][None, :].astype(F32), lambda_init)
            x = _outproj(a, x, m, b_w_out[idx].astype(BF16))
        else:
            x = _short_conv(x, m, g, c_w_in[idx].astype(BF16), c_conv[idx].astype(F32), c_w_out[idx].astype(BF16))
        m, g = sub(2)
        x = _ffn(x, m, g, wi, wo, (l, 1))
    return x
```

```python
import functools
import math

import jax
import jax.numpy as jnp
from jax import lax
from jax.experimental import pallas as pl
from jax.experimental.pallas import tpu as pltpu

F32 = jnp.float32
BF16 = jnp.bfloat16

EPS = 1e-6
N_SUB = 3
N_MIXERS = 3
HEAD_W = 128
N_HEADS = 8
B_DH = 64
B_ROT = 16
ROPE_THETA = 500000.0
MIB = 1024 * 1024

FFN_TM = 1024
FFN_PARTS = 4
FFN_FC = 256
MIX_TM = 512
A_CHUNK = 256
A_SUB = 128
A_LOCAL = 32
A_SAFE = 60.0
ATT_T = 1024
ATT_R = 512
ATT_CAUSAL_R = 256


def _silu(z):
    return z * jax.nn.sigmoid(z)


def _dot(a, b):
    return jnp.dot(a, b, preferred_element_type=F32)


def _dot_nt(a, b):
    return lax.dot_general(a, b, (((1,), (1,)), ((), ())), preferred_element_type=F32)


def _dot_tn(a, b):
    return lax.dot_general(a, b, (((0,), (0,)), ((), ())), preferred_element_type=F32)


def _prenorm(x, g, mod):
    ms = jnp.mean(x * x, axis=-1, keepdims=True)
    y = x * lax.rsqrt(ms + EPS) * g
    return y * (1.0 + mod[1:2]) + mod[0:1]


def _resident(shape):
    nd = len(shape)
    return pl.BlockSpec(shape, lambda *_: (0,) * nd, pipeline_mode=pl.Buffered(1))


def _resident_at(shape, lead):
    nd = len(shape)
    return pl.BlockSpec((None,) * len(lead) + tuple(shape), lambda *_: tuple(lead) + (0,) * nd,
                        pipeline_mode=pl.Buffered(1))


def _params(sem, vmem_mib):
    return pltpu.CompilerParams(dimension_semantics=sem, vmem_limit_bytes=vmem_mib * MIB)


def _adaln_kernel(c_ref, w_ref, b_ref, o_ref):
    ca = _silu(c_ref[...]).astype(BF16)
    o_ref[0] = _dot(ca, w_ref[0].astype(BF16)) + b_ref[0]


def _adaln(c, ada_w, ada_b):
    depth, d, n = ada_w.shape
    bn = c.shape[0]
    rows = 8 * pl.cdiv(bn, 8)
    cp = jnp.zeros((rows, d), F32).at[:bn].set(c)
    tn = n // 6
    out = pl.pallas_call(
        _adaln_kernel,
        grid=(depth, n // tn),
        in_specs=[
            pl.BlockSpec((rows, d), lambda l, j: (0, 0)),
            pl.BlockSpec((1, d, tn), lambda l, j: (l, 0, j)),
            pl.BlockSpec((1, 1, tn), lambda l, j: (l, 0, j)),
        ],
        out_specs=pl.BlockSpec((1, rows, tn), lambda l, j: (l, 0, j)),
        out_shape=jax.ShapeDtypeStruct((depth, rows, n), F32),
        compiler_params=_params(("arbitrary", "arbitrary"), 40),
        name="adaln",
    )(cp, ada_w, ada_b.reshape(depth, 1, n))
    return out[:, :bn]


def _ffn_kernel(x_ref, mod_ref, g_ref, wi_ref, wo_ref, o_ref, h_ref, *, fc, parts):
    mod = mod_ref[0]
    ff = h_ref.shape[1]
    rows = x_ref.shape[1] // parts

    def hidden(p):
        rs = slice(p * rows, (p + 1) * rows)
        xn = _prenorm(x_ref[0, rs, :], g_ref[...], mod).astype(BF16)
        for c in range(ff // fc):
            gate = _dot(xn, wi_ref[:, c * fc:(c + 1) * fc])
            up = _dot(xn, wi_ref[:, ff + c * fc:ff + (c + 1) * fc])
            h_ref[rs, c * fc:(c + 1) * fc] = (_silu(gate) * up).astype(BF16)

    def project(p):
        rs = slice(p * rows, (p + 1) * rows)
        y = _dot(h_ref[rs, :], wo_ref[...])
        o_ref[0, rs, :] = x_ref[0, rs, :] + (0.5 * (1.0 + mod[2:3])) * y

    hidden(0)
    for p in range(1, parts):
        hidden(p)
        project(p - 1)
    project(parts - 1)


def _ffn(x, mod, g, wi, wo, which):
    bn, s, d = x.shape
    ff = wo.shape[-2]
    tm, fc = FFN_TM, FFN_FC
    return pl.pallas_call(
        functools.partial(_ffn_kernel, fc=fc, parts=FFN_PARTS),
        grid=(bn, s // tm),
        in_specs=[
            pl.BlockSpec((1, tm, d), lambda b, i: (b, i, 0)),
            pl.BlockSpec((1, 3, d), lambda b, i: (b, 0, 0)),
            _resident((1, d)),
            _resident_at((d, 2 * ff), which),
            _resident_at((ff, d), which),
        ],
        out_specs=pl.BlockSpec((1, tm, d), lambda b, i: (b, i, 0)),
        out_shape=jax.ShapeDtypeStruct(x.shape, x.dtype),
        scratch_shapes=[pltpu.VMEM((tm, ff), BF16)],
        compiler_params=_params(("arbitrary", "arbitrary"), 56),
        name="ffn",
    )(x, mod, g, wi, wo)


def _block_masks(n, local):
    row = lax.broadcasted_iota(jnp.int32, (n, n), 0)
    col = lax.broadcasted_iota(jnp.int32, (n, n), 1)
    sh = int(math.log2(local))
    masks = {"diag": row == col, "local": ((row >> sh) == (col >> sh)) & (row >= col)}
    w = n // 2
    while w >= 1:
        sh = int(math.log2(w))
        rb, cb = row >> sh, col >> sh
        masks[w] = (rb == cb + 1) & ((cb & 1) == 0)
        w //= 2
    return masks


def _block_scores(q, k, lf, b, masks, local):
    n, hw = q.shape
    if local:
        b3 = b.reshape(n // local, local, hw)
        first = b3[:, 0:1, :]
        qe = (q * jnp.exp(b3 - first).reshape(n, hw)).astype(BF16)
        ke = (k * jnp.exp(first - b3).reshape(n, hw)).astype(BF16)
        att = jnp.where(masks["local"], _dot_nt(qe, ke), 0.0)
    else:
        att = jnp.where(masks["diag"], _dot_nt(q.astype(BF16), k.astype(BF16)), 0.0)
    w = n // 2
    while w >= (local or 8):
        b3 = b.reshape(n // w, w, hw)
        last = b3[:, w - 1:w, :]
        prev = jnp.concatenate([last[:1], last[:-1]], axis=0)
        qa = jnp.minimum(b3 - prev, 0.0).reshape(n, hw)
        ka = (last - b3).reshape(n, hw)
        sc = _dot_nt((q * jnp.exp(qa)).astype(BF16), (k * jnp.exp(ka)).astype(BF16))
        att = jnp.where(masks[w], sc, att)
        w //= 2
    if local:
        return att
    t = lax.broadcasted_iota(jnp.int32, (n, 1), 0)
    back = [None] + [pltpu.roll(lf, i, 0) for i in range(1, 4)]
    fwd = [None] + [pltpu.roll(lf, n - i, 0) for i in range(1, 4)]
    while w >= 1:
        r = t & (w - 1)
        qa = lf
        ke = k
        if w > 1:
            ka = jnp.where(r <= w - 2, fwd[1], 0.0)
            for i in range(1, w):
                qa = qa + jnp.where(r >= i, back[i], 0.0)
            for i in range(2, w):
                ka = ka + jnp.where(r <= w - 1 - i, fwd[i], 0.0)
            ke = k * jnp.exp(ka)
        sc = _dot_nt((q * jnp.exp(qa)).astype(BF16), ke.astype(BF16))
        att = jnp.where(masks[w], sc, att)
        w //= 2
    return att


def _hgrn2_kernel(x_ref, mod_ref, g_ref, win_ref, alb_ref, on_ref, wout_ref, o_ref,
                  st_ref, q_ref, k_ref, lf_ref, b_ref, v_ref, gt_ref, oall_ref, *, idx, sub):
    chunk, d = x_ref.shape[1], x_ref.shape[2]
    hw = HEAD_W

    @pl.when(pl.program_id(1) == 0)
    def _():
        st_ref[...] = jnp.zeros_like(st_ref)

    x = x_ref[0]
    mod = mod_ref[0]
    xn = _prenorm(x, g_ref[...], mod).astype(BF16)

    alb = alb_ref[...]
    e = jnp.exp(alb - jnp.max(alb, axis=0, keepdims=True))
    sm = e / jnp.sum(e, axis=0, keepdims=True)
    lb = jnp.sum(sm[0:idx + 1], axis=0, keepdims=True) - sm[0:1]

    q_ref[...] = _silu(_dot(xn, win_ref[:, 0:d]))
    f = lb + (1.0 - lb) * jax.nn.sigmoid(_dot(xn, win_ref[:, d:2 * d]))
    k_ref[...] = 1.0 - f
    v_ref[...] = _dot(xn, win_ref[:, 2 * d:3 * d])
    gt_ref[...] = _silu(_dot(xn, win_ref[:, 3 * d:4 * d]))

    lf = jnp.log(f)
    lf_ref[...] = lf
    hi = lf.astype(BF16)
    r1 = lf - hi.astype(F32)
    mid = r1.astype(BF16)
    lo = (r1 - mid.astype(F32)).astype(BF16)
    row = lax.broadcasted_iota(jnp.int32, (chunk, chunk), 0)
    col = lax.broadcasted_iota(jnp.int32, (chunk, chunk), 1)
    tri = (row >= col).astype(BF16)
    b_ref[...] = _dot(tri, hi) + _dot(tri, mid) + _dot(tri, lo)

    masks = _block_masks(sub, A_LOCAL)
    nsub = chunk // sub

    def heads(local):
        staged = []
        for h in range(d // hw):
            hs = slice(h * hw, (h + 1) * hw)
            q, k, lf, b = q_ref[:, hs], k_ref[:, hs], lf_ref[:, hs], b_ref[:, hs]
            inter = _dot_nt((q * jnp.exp(b)).astype(BF16), st_ref[h].astype(BF16))
            atts, crosses = [], []
            for i in range(nsub):
                rs = slice(i * sub, (i + 1) * sub)
                atts.append(_block_scores(q[rs], k[rs], lf[rs], b[rs], masks, local).astype(BF16))
                if i > 0:
                    edge = b[i * sub - 1:i * sub, :]
                    qc = (q[rs] * jnp.exp(b[rs] - edge)).astype(BF16)
                    for j in range(i):
                        js = slice(j * sub, (j + 1) * sub)
                        kc = (k[js] * jnp.exp(edge - b[js])).astype(BF16)
                        crosses.append((i, j, _dot_nt(qc, kc).astype(BF16)))
            staged.append((inter, atts, crosses))
        for h in range(d // hw):
            hs = slice(h * hw, (h + 1) * hw)
            inter, atts, crosses = staged[h]
            k, b = k_ref[:, hs], b_ref[:, hs]
            vb = v_ref[:, hs].astype(BF16)
            outs = [inter[i * sub:(i + 1) * sub] + _dot(atts[i], vb[i * sub:(i + 1) * sub]) for i in range(nsub)]
            for i, j, cs in crosses:
                outs[i] = outs[i] + _dot(cs, vb[j * sub:(j + 1) * sub])
            o = jnp.concatenate(outs, axis=0)

            bl = b[chunk - 1:chunk, :]
            kt = k * jnp.exp(bl - b)
            st_ref[h] = st_ref[h] * jnp.exp(bl) + _dot_tn(vb, kt.astype(BF16))

            ms = jnp.mean(o * o, axis=-1, keepdims=True)
            on = o * lax.rsqrt(ms + EPS) * on_ref[...]
            oall_ref[:, hs] = (on * gt_ref[:, hs]).astype(BF16)

    b3 = b_ref[...].reshape(chunk // A_LOCAL, A_LOCAL, d)
    safe = jnp.max(b3[:, 0:1, :] - b3[:, A_LOCAL - 1:A_LOCAL, :]) <= A_SAFE

    @pl.when(safe)
    def _():
        heads(A_LOCAL)

    @pl.when(jnp.logical_not(safe))
    def _():
        heads(None)

    y = _dot(oall_ref[...], wout_ref[...])
    o_ref[0] = x + (1.0 + mod[2:3]) * y


def _hgrn2(x, mod, g, w_in, a_lb, onorm, w_out, idx):
    bn, s, d = x.shape
    chunk = A_CHUNK
    nh = d // HEAD_W
    act = pltpu.VMEM((chunk, d), F32)
    return pl.pallas_call(
        functools.partial(_hgrn2_kernel, idx=idx, sub=A_SUB),
        grid=(bn, s // chunk),
        in_specs=[
            pl.BlockSpec((1, chunk, d), lambda b, i: (b, i, 0)),
            pl.BlockSpec((1, 3, d), lambda b, i: (b, 0, 0)),
            _resident((1, d)),
            _resident_at((d, 4 * d), (idx,)),
            _resident(a_lb.shape),
            _resident((1, HEAD_W)),
            _resident_at((d, d), (idx,)),
        ],
        out_specs=pl.BlockSpec((1, chunk, d), lambda b, i: (b, i, 0)),
        out_shape=jax.ShapeDtypeStruct(x.shape, x.dtype),
        scratch_shapes=[pltpu.VMEM((nh, HEAD_W, HEAD_W), F32), act, act, act, act, act, act,
                        pltpu.VMEM((chunk, d), BF16)],
        compiler_params=_params(("arbitrary", "arbitrary"), 48),
        name="hgrn2",
    )(x, mod, g, w_in, a_lb, onorm, w_out)


def _conv_kernel(x_ref, mod_ref, g_ref, win_ref, cw_ref, wout_ref, o_ref, carry_ref):
    tm, d = x_ref.shape[1], x_ref.shape[2]

    @pl.when(pl.program_id(1) == 0)
    def _():
        carry_ref[...] = jnp.zeros_like(carry_ref)

    x = x_ref[0]
    mod = mod_ref[0]
    xn = _prenorm(x, g_ref[...], mod).astype(BF16)
    u = _dot(xn, win_ref[:, d:2 * d]) * _dot(xn, win_ref[:, 2 * d:3 * d])
    carry = carry_ref[...]
    row = lax.broadcasted_iota(jnp.int32, (tm, 1), 0)
    u1 = jnp.where(row == 0, carry[7:8], pltpu.roll(u, 1, 0))
    u2 = jnp.where(row == 0, carry[6:7], jnp.where(row == 1, carry[7:8], pltpu.roll(u, 2, 0)))
    cw = cw_ref[...]
    y = cw[0:1] * u2 + cw[1:2] * u1 + cw[2:3] * u
    carry_ref[...] = u[tm - 8:tm]
    z = (_dot(xn, win_ref[:, 0:d]) * y).astype(BF16)
    o_ref[0] = x + (1.0 + mod[2:3]) * _dot(z, wout_ref[...])


def _short_conv(x, mod, g, w_in, conv_w, w_out):
    bn, s, d = x.shape
    tm = MIX_TM
    return pl.pallas_call(
        _conv_kernel,
        grid=(bn, s // tm),
        in_specs=[
            pl.BlockSpec((1, tm, d), lambda b, i: (b, i, 0)),
            pl.BlockSpec((1, 3, d), lambda b, i: (b, 0, 0)),
            _resident((1, d)),
            _resident((d, 3 * d)),
            _resident(conv_w.shape),
            _resident((d, d)),
        ],
        out_specs=pl.BlockSpec((1, tm, d), lambda b, i: (b, i, 0)),
        out_shape=jax.ShapeDtypeStruct(x.shape, x.dtype),
        scratch_shapes=[pltpu.VMEM((8, d), F32)],
        compiler_params=_params(("arbitrary", "arbitrary"), 48),
        name="short_conv",
    )(x, mod, g, w_in, conv_w, w_out)


def _qkv_kernel(x_ref, mod_ref, g_ref, win_ref, gq_ref, gk_ref, bd_ref, pos_ref, invf_ref,
                q_ref, k_ref, v_ref):
    d = x_ref.shape[2]
    hw = HEAD_W
    xn = _prenorm(x_ref[0], g_ref[...], mod_ref[0]).astype(BF16)

    ang = pos_ref[0].astype(F32) * invf_ref[...]
    lane = lax.broadcasted_iota(jnp.int32, (1, hw), 1) & (B_DH - 1)
    cs, sn = jnp.cos(ang), jnp.sin(ang)
    ca = jnp.where(lane < B_ROT, cs, 1.0)
    cm = jnp.where(lane < B_ROT // 2, -sn, 0.0)
    cp = jnp.where((lane >= B_ROT // 2) & (lane < B_ROT), sn, 0.0)
    half = B_ROT // 2

    gw = 2 * hw
    for off, gain_ref, is_q in ((0, gq_ref, True), (d, gk_ref, False)):
        for j in range(d // gw):
            z = _dot(xn, win_ref[:, off + j * gw:off + (j + 1) * gw])
            ss = _dot((z * z).astype(BF16), bd_ref[...])
            zn = z * lax.rsqrt(ss * (1.0 / B_DH) + EPS) * gain_ref[:, j * gw:(j + 1) * gw]
            for t in range(gw // hw):
                zz = zn[:, t * hw:(t + 1) * hw]
                r = zz * ca + pltpu.roll(zz, hw - half, 1) * cm + pltpu.roll(zz, half, 1) * cp
                head = j * (gw // hw) + t
                if is_q:
                    q_ref[0, head] = (r * B_DH ** -0.5).astype(BF16)
                else:
                    k_ref[0, head, 0] = r.T.astype(BF16)
    v = _dot(xn, win_ref[:, 2 * d:3 * d]).astype(BF16)
    for head in range(d // hw):
        v_ref[0, head] = v[:, head * hw:(head + 1) * hw]


def _qkv(x, mod, g, w_in, qk_g, positions):
    bn, s, d = x.shape
    tm = MIX_TM
    reps = d // B_DH
    gq = jnp.tile(qk_g[0].astype(F32), reps)[None, :]
    gk = jnp.tile(qk_g[1].astype(F32), reps)[None, :]
    blk = jnp.arange(2 * HEAD_W) // B_DH
    bd = (blk[:, None] == blk[None, :]).astype(BF16)
    inv_freq = ROPE_THETA ** (-jnp.arange(0, B_ROT, 2, dtype=F32) / B_ROT)
    lane = jnp.arange(HEAD_W) % B_DH
    invf = jnp.where(lane < B_ROT, inv_freq[lane % (B_ROT // 2)], 0.0)[None, :]
    tok = pl.BlockSpec((1, tm, d), lambda b, i: (b, i, 0))
    nh = d // HEAD_W
    hm_spec = pl.BlockSpec((1, nh, tm, HEAD_W), lambda b, i: (b, 0, i, 0))
    hm_out = jax.ShapeDtypeStruct((bn, nh, s, HEAD_W), BF16)
    kt_spec = pl.BlockSpec((1, nh, 1, HEAD_W, tm), lambda b, i: (b, 0, i, 0, 0))
    kt_out = jax.ShapeDtypeStruct((bn, nh, s // tm, HEAD_W, tm), BF16)
    return pl.pallas_call(
        _qkv_kernel,
        grid=(bn, s // tm),
        in_specs=[
            tok,
            pl.BlockSpec((1, 3, d), lambda b, i: (b, 0, 0)),
            _resident((1, d)),
            _resident((d, 3 * d)),
            _resident((1, d)),
            _resident((1, d)),
            _resident((2 * HEAD_W, 2 * HEAD_W)),
            pl.BlockSpec((1, tm, 1), lambda b, i: (b, i, 0)),
            _resident((1, HEAD_W)),
        ],
        out_specs=[hm_spec, kt_spec, hm_spec],
        out_shape=[hm_out, kt_out, hm_out],
        compiler_params=_params(("arbitrary", "arbitrary"), 48),
        name="attn_qkv",
    )(x, mod, g, w_in, gq, gk, bd, positions.reshape(bn, s, 1), invf)


def _attn_kernel(q_ref, kt_ref, v_ref, lam_ref, sg_ref, o_ref, qs_ref, s_ref, m_ref, acc_ref, *,
                 lambda_init, full_rows, causal_rows):
    t = q_ref.shape[2]
    hw = HEAD_W
    i = pl.program_id(2)
    q = q_ref[0, 0]
    lane = lax.broadcasted_iota(jnp.int32, (1, hw), 1)
    qs_ref[0] = jnp.where(lane < B_DH, q, jnp.zeros_like(q))
    qs_ref[1] = jnp.where(lane >= B_DH, q, jnp.zeros_like(q))
    m_ref[...] = jnp.full_like(m_ref, -jnp.inf)
    acc_ref[...] = jnp.zeros_like(acc_ref)
    ones = jnp.ones((t, hw), BF16)

    def scores_to(slot, j):
        per = t // kt_ref.shape[-1]
        kt = jnp.concatenate([kt_ref[0, 0, j * per + a] for a in range(per)], axis=1)
        for c, rc in [(c, rc) for rc in range(t // full_rows) for c in range(2)]:
            rs = slice(rc * full_rows, (rc + 1) * full_rows)
            s_ref[slot, c, rs, :] = _dot(qs_ref[c, rs, :], kt)

    def softmax_pv(slot, j, causal):
        rows = causal_rows if causal else full_rows
        for c, rc in [(c, rc) for rc in range(t // rows) for c in range(2)]:
            rs = slice(rc * rows, (rc + 1) * rows)
            nk = hw * pl.cdiv((rc + 1) * rows, hw) if causal else t
            sc = s_ref[slot, c, rs, :nk]
            if causal:
                r = rc * rows + lax.broadcasted_iota(jnp.int32, (rows, nk), 0)
                cc = lax.broadcasted_iota(jnp.int32, (rows, nk), 1)
                sc = jnp.where(r >= cc, sc, -jnp.inf)
            cols = [sc[:, a * hw:(a + 1) * hw] for a in range(nk // hw)]
            rmax = jnp.max(functools.reduce(jnp.maximum, cols), axis=-1, keepdims=True)
            m_old = m_ref[c, rs, :]
            m_new = jnp.maximum(m_old, rmax)
            p = jnp.concatenate([jnp.exp(cl - m_new) for cl in cols], axis=1).astype(BF16)
            vx = jnp.concatenate([v_ref[0, 0, pl.ds(j * t, nk), :], ones[:nk]], axis=1)
            alpha = jnp.exp(m_old - m_new)
            acc_ref[c, rs, :] = acc_ref[c, rs, :] * jnp.concatenate([alpha, alpha], axis=1) + _dot(p, vx)
            m_ref[c, rs, :] = m_new

    scores_to(0, 0)

    def pair(p, carry):
        scores_to(1, 2 * p + 1)
        softmax_pv(0, 2 * p, False)
        scores_to(0, 2 * p + 2)
        softmax_pv(1, 2 * p + 1, False)
        return carry

    lax.fori_loop(0, i // 2, pair, 0)

    @pl.when(i % 2 == 1)
    def _():
        scores_to(1, i)
        softmax_pv(0, i - 1, False)

    softmax_pv(i % 2, i, True)

    lp = lam_ref[...]
    lam = (jnp.exp(jnp.sum(lp[0:1] * lp[1:2], axis=-1, keepdims=True))
           - jnp.exp(jnp.sum(lp[2:3] * lp[3:4], axis=-1, keepdims=True)) + lambda_init)
    a0, a1 = acc_ref[0], acc_ref[1]
    o = a0[:, :hw] / a0[:, hw:] - lam * (a1[:, :hw] / a1[:, hw:])
    ms = jnp.mean(o * o, axis=-1, keepdims=True)
    o_ref[0, 0] = (o * lax.rsqrt(ms + EPS) * sg_ref[...] * (1.0 - lambda_init)).astype(o_ref.dtype)


def _attention(q, kt, v, lam_p, subln, lambda_init):
    bn, nh, s, _ = q.shape
    t = ATT_T
    kb = kt.shape[-1]
    assert kt.shape == (bn, nh, s // kb, HEAD_W, kb) and t % kb == 0
    tile = pl.BlockSpec((1, 1, t, HEAD_W), lambda b, h, i: (b, h, i, 0))
    return pl.pallas_call(
        functools.partial(_attn_kernel, lambda_init=lambda_init, full_rows=ATT_R, causal_rows=ATT_CAUSAL_R),
        grid=(bn, nh, s // t),
        in_specs=[
            tile,
            pl.BlockSpec((1, 1, s // kb, HEAD_W, kb), lambda b, h, i: (b, h, 0, 0, 0)),
            pl.BlockSpec((1, 1, s, HEAD_W), lambda b, h, i: (b, h, 0, 0)),
            _resident(lam_p.shape),
            _resident((1, HEAD_W)),
        ],
        out_specs=tile,
        out_shape=jax.ShapeDtypeStruct((bn, nh, s, HEAD_W), BF16),
        scratch_shapes=[pltpu.VMEM((2, t, HEAD_W), BF16), pltpu.VMEM((2, 2, t, t), F32),
                        pltpu.VMEM((2, t, HEAD_W), F32),
                        pltpu.VMEM((2, t, 2 * HEAD_W), F32)],
        compiler_params=_params(("arbitrary", "arbitrary", "arbitrary"), 56),
        name="attn_core",
    )(q, kt, v, lam_p, subln)


def _outproj_kernel(a_ref, x_ref, mod_ref, w_ref, o_ref):
    a = jnp.concatenate([a_ref[0, h] for h in range(a_ref.shape[1])], axis=1)
    o_ref[0] = x_ref[0] + (1.0 + mod_ref[0][2:3]) * _dot(a, w_ref[...])


def _outproj(a, x, mod, w_out):
    bn, s, d = x.shape
    tm = MIX_TM
    tok = pl.BlockSpec((1, tm, d), lambda b, i: (b, i, 0))
    heads = pl.BlockSpec((1, d // HEAD_W, tm, HEAD_W), lambda b, i: (b, 0, i, 0))
    return pl.pallas_call(
        _outproj_kernel,
        grid=(bn, s // tm),
        in_specs=[heads, tok, pl.BlockSpec((1, 3, d), lambda b, i: (b, 0, 0)), _resident((d, d))],
        out_specs=tok,
        out_shape=jax.ShapeDtypeStruct(x.shape, x.dtype),
        compiler_params=_params(("arbitrary", "arbitrary"), 32),
        name="attn_out",
    )(a, x, mod, w_out)


def kernel(x, c, positions, ada_w, ada_b, norm_g, ffn_wi, ffn_wo, a_w_in, a_w_out, a_lb, a_onorm,
           b_w_in, b_w_out, b_qk_g, b_lam, b_subln, c_w_in, c_conv, c_w_out):
    bn, s, d = x.shape
    depth = ada_w.shape[0]
    mod = _adaln(c, ada_w, ada_b).reshape(depth, bn, N_SUB, 3, d)
    wi, wo = ffn_wi.astype(BF16), ffn_wo.astype(BF16)
    a_in, a_out = a_w_in.astype(BF16), a_w_out.astype(BF16)

    for l in range(depth):
        def sub(j):
            return mod[l, :, j], norm_g[l, j][None, :]

        m, g = sub(0)
        x = _ffn(x, m, g, wi, wo, (l, 0))
        m, g = sub(1)
        kind, idx = l % N_MIXERS, l // N_MIXERS
        if kind == 0:
            x = _hgrn2(x, m, g, a_in, a_lb.astype(F32), a_onorm[idx][None, :].astype(F32), a_out, idx)
        elif kind == 1:
            lambda_init = 0.8 - 0.6 * math.exp(-0.3 * l)
            q, k, v = _qkv(x, m, g, b_w_in[idx].astype(BF16), b_qk_g[idx], positions)
            a = _attention(q, k, v, b_lam[idx].astype(F32), b_subln[idx][None, :].astype(F32), lambda_init)
            x = _outproj(a, x, m, b_w_out[idx].astype(BF16))
        else:
            x = _short_conv(x, m, g, c_w_in[idx].astype(BF16), c_conv[idx].astype(F32), c_w_out[idx].astype(BF16))
        m, g = sub(2)
        x = _ffn(x, m, g, wi, wo, (l, 1))
    return x
```

```python
import functools
import math

import jax
import jax.numpy as jnp
from jax import lax
from jax.experimental import pallas as pl
from jax.experimental.pallas import tpu as pltpu

F32 = jnp.float32
BF16 = jnp.bfloat16

EPS = 1e-6
N_SUB = 3
N_MIXERS = 3
HEAD_W = 128
N_HEADS = 8
B_DH = 64
B_ROT = 16
ROPE_THETA = 500000.0
MIB = 1024 * 1024

FFN_TM = 1024
FFN_PARTS = 4
FFN_FC = 256
MIX_TM = 1024
A_CHUNK = 256
A_SUB = 128
A_LOCAL = 32
A_SAFE = 60.0
ATT_T = 1024
ATT_R = 512
ATT_CAUSAL_R = 256


def _silu(z):
    return z * jax.nn.sigmoid(z)


def _dot(a, b):
    return jnp.dot(a, b, preferred_element_type=F32)


def _dot_nt(a, b):
    return lax.dot_general(a, b, (((1,), (1,)), ((), ())), preferred_element_type=F32)


def _dot_tn(a, b):
    return lax.dot_general(a, b, (((0,), (0,)), ((), ())), preferred_element_type=F32)


def _prenorm(x, g, mod):
    ms = jnp.mean(x * x, axis=-1, keepdims=True)
    y = x * lax.rsqrt(ms + EPS) * g
    return y * (1.0 + mod[1:2]) + mod[0:1]


def _resident(shape):
    nd = len(shape)
    return pl.BlockSpec(shape, lambda *_: (0,) * nd, pipeline_mode=pl.Buffered(1))


def _resident_at(shape, lead):
    nd = len(shape)
    return pl.BlockSpec((None,) * len(lead) + tuple(shape), lambda *_: tuple(lead) + (0,) * nd,
                        pipeline_mode=pl.Buffered(1))


def _params(sem, vmem_mib):
    return pltpu.CompilerParams(dimension_semantics=sem, vmem_limit_bytes=vmem_mib * MIB)


def _adaln_kernel(c_ref, w_ref, b_ref, o_ref):
    ca = _silu(c_ref[...]).astype(BF16)
    o_ref[0] = _dot(ca, w_ref[0].astype(BF16)) + b_ref[0]


def _adaln(c, ada_w, ada_b):
    depth, d, n = ada_w.shape
    bn = c.shape[0]
    rows = 8 * pl.cdiv(bn, 8)
    cp = jnp.zeros((rows, d), F32).at[:bn].set(c)
    tn = n // 6
    out = pl.pallas_call(
        _adaln_kernel,
        grid=(depth, n // tn),
        in_specs=[
            pl.BlockSpec((rows, d), lambda l, j: (0, 0)),
            pl.BlockSpec((1, d, tn), lambda l, j: (l, 0, j)),
            pl.BlockSpec((1, 1, tn), lambda l, j: (l, 0, j)),
        ],
        out_specs=pl.BlockSpec((1, rows, tn), lambda l, j: (l, 0, j)),
        out_shape=jax.ShapeDtypeStruct((depth, rows, n), F32),
        compiler_params=_params(("arbitrary", "arbitrary"), 40),
        name="adaln",
    )(cp, ada_w, ada_b.reshape(depth, 1, n))
    return out[:, :bn]


def _ffn_kernel(x_ref, mod_ref, g_ref, wi_ref, wo_ref, o_ref, h_ref, *, fc, parts):
    mod = mod_ref[0]
    ff = h_ref.shape[1]
    rows = x_ref.shape[1] // parts

    def hidden(p):
        rs = slice(p * rows, (p + 1) * rows)
        xn = _prenorm(x_ref[0, rs, :], g_ref[...], mod).astype(BF16)
        for c in range(ff // fc):
            gate = _dot(xn, wi_ref[:, c * fc:(c + 1) * fc])
            up = _dot(xn, wi_ref[:, ff + c * fc:ff + (c + 1) * fc])
            h_ref[rs, c * fc:(c + 1) * fc] = (_silu(gate) * up).astype(BF16)

    def project(p):
        rs = slice(p * rows, (p + 1) * rows)
        y = _dot(h_ref[rs, :], wo_ref[...])
        o_ref[0, rs, :] = x_ref[0, rs, :] + (0.5 * (1.0 + mod[2:3])) * y

    hidden(0)
    for p in range(1, parts):
        hidden(p)
        project(p - 1)
    project(parts - 1)


def _ffn(x, mod, g, wi, wo, which):
    bn, s, d = x.shape
    ff = wo.shape[-2]
    tm, fc = FFN_TM, FFN_FC
    return pl.pallas_call(
        functools.partial(_ffn_kernel, fc=fc, parts=FFN_PARTS),
        grid=(bn, s // tm),
        in_specs=[
            pl.BlockSpec((1, tm, d), lambda b, i: (b, i, 0)),
            pl.BlockSpec((1, 3, d), lambda b, i: (b, 0, 0)),
            _resident((1, d)),
            _resident_at((d, 2 * ff), which),
            _resident_at((ff, d), which),
        ],
        out_specs=pl.BlockSpec((1, tm, d), lambda b, i: (b, i, 0)),
        out_shape=jax.ShapeDtypeStruct(x.shape, x.dtype),
        scratch_shapes=[pltpu.VMEM((tm, ff), BF16)],
        compiler_params=_params(("arbitrary", "arbitrary"), 56),
        name="ffn",
    )(x, mod, g, wi, wo)


def _block_masks(n, local):
    row = lax.broadcasted_iota(jnp.int32, (n, n), 0)
    col = lax.broadcasted_iota(jnp.int32, (n, n), 1)
    sh = int(math.log2(local))
    masks = {"diag": row == col, "local": ((row >> sh) == (col >> sh)) & (row >= col)}
    w = n // 2
    while w >= 1:
        sh = int(math.log2(w))
        rb, cb = row >> sh, col >> sh
        masks[w] = (rb == cb + 1) & ((cb & 1) == 0)
        w //= 2
    return masks


def _block_scores(q, k, lf, b, masks, local):
    n, hw = q.shape
    if local:
        b3 = b.reshape(n // local, local, hw)
        first = b3[:, 0:1, :]
        qe = (q * jnp.exp(b3 - first).reshape(n, hw)).astype(BF16)
        ke = (k * jnp.exp(first - b3).reshape(n, hw)).astype(BF16)
        att = jnp.where(masks["local"], _dot_nt(qe, ke), 0.0)
    else:
        att = jnp.where(masks["diag"], _dot_nt(q.astype(BF16), k.astype(BF16)), 0.0)
    w = n // 2
    while w >= (local or 8):
        b3 = b.reshape(n // w, w, hw)
        last = b3[:, w - 1:w, :]
        prev = jnp.concatenate([last[:1], last[:-1]], axis=0)
        qa = jnp.minimum(b3 - prev, 0.0).reshape(n, hw)
        ka = (last - b3).reshape(n, hw)
        sc = _dot_nt((q * jnp.exp(qa)).astype(BF16), (k * jnp.exp(ka)).astype(BF16))
        att = jnp.where(masks[w], sc, att)
        w //= 2
    if local:
        return att
    t = lax.broadcasted_iota(jnp.int32, (n, 1), 0)
    back = [None] + [pltpu.roll(lf, i, 0) for i in range(1, 4)]
    fwd = [None] + [pltpu.roll(lf, n - i, 0) for i in range(1, 4)]
    while w >= 1:
        r = t & (w - 1)
        qa = lf
        ke = k
        if w > 1:
            ka = jnp.where(r <= w - 2, fwd[1], 0.0)
            for i in range(1, w):
                qa = qa + jnp.where(r >= i, back[i], 0.0)
            for i in range(2, w):
                ka = ka + jnp.where(r <= w - 1 - i, fwd[i], 0.0)
            ke = k * jnp.exp(ka)
        sc = _dot_nt((q * jnp.exp(qa)).astype(BF16), ke.astype(BF16))
        att = jnp.where(masks[w], sc, att)
        w //= 2
    return att


def _hgrn2_kernel(x_ref, mod_ref, g_ref, win_ref, alb_ref, on_ref, wout_ref, o_ref,
                  st_ref, q_ref, k_ref, lf_ref, b_ref, v_ref, gt_ref, oall_ref, *, idx, sub):
    chunk, d = x_ref.shape[1], x_ref.shape[2]
    hw = HEAD_W

    @pl.when(pl.program_id(1) == 0)
    def _():
        st_ref[...] = jnp.zeros_like(st_ref)

    x = x_ref[0]
    mod = mod_ref[0]
    xn = _prenorm(x, g_ref[...], mod).astype(BF16)

    alb = alb_ref[...]
    e = jnp.exp(alb - jnp.max(alb, axis=0, keepdims=True))
    sm = e / jnp.sum(e, axis=0, keepdims=True)
    lb = jnp.sum(sm[0:idx + 1], axis=0, keepdims=True) - sm[0:1]

    q_ref[...] = _silu(_dot(xn, win_ref[:, 0:d]))
    f = lb + (1.0 - lb) * jax.nn.sigmoid(_dot(xn, win_ref[:, d:2 * d]))
    k_ref[...] = 1.0 - f
    v_ref[...] = _dot(xn, win_ref[:, 2 * d:3 * d])
    gt_ref[...] = _silu(_dot(xn, win_ref[:, 3 * d:4 * d]))

    lf = jnp.log(f)
    lf_ref[...] = lf
    hi = lf.astype(BF16)
    r1 = lf - hi.astype(F32)
    mid = r1.astype(BF16)
    lo = (r1 - mid.astype(F32)).astype(BF16)
    row = lax.broadcasted_iota(jnp.int32, (chunk, chunk), 0)
    col = lax.broadcasted_iota(jnp.int32, (chunk, chunk), 1)
    tri = (row >= col).astype(BF16)
    b_ref[...] = _dot(tri, hi) + _dot(tri, mid) + _dot(tri, lo)

    masks = _block_masks(sub, A_LOCAL)
    nsub = chunk // sub

    def heads(local):
        staged = []
        for h in range(d // hw):
            hs = slice(h * hw, (h + 1) * hw)
            q, k, lf, b = q_ref[:, hs], k_ref[:, hs], lf_ref[:, hs], b_ref[:, hs]
            inter = _dot_nt((q * jnp.exp(b)).astype(BF16), st_ref[h].astype(BF16))
            atts, crosses = [], []
            for i in range(nsub):
                rs = slice(i * sub, (i + 1) * sub)
                atts.append(_block_scores(q[rs], k[rs], lf[rs], b[rs], masks, local).astype(BF16))
                if i > 0:
                    edge = b[i * sub - 1:i * sub, :]
                    qc = (q[rs] * jnp.exp(b[rs] - edge)).astype(BF16)
                    for j in range(i):
                        js = slice(j * sub, (j + 1) * sub)
                        kc = (k[js] * jnp.exp(edge - b[js])).astype(BF16)
                        crosses.append((i, j, _dot_nt(qc, kc).astype(BF16)))
            staged.append((inter, atts, crosses))
        for h in range(d // hw):
            hs = slice(h * hw, (h + 1) * hw)
            inter, atts, crosses = staged[h]
            k, b = k_ref[:, hs], b_ref[:, hs]
            vb = v_ref[:, hs].astype(BF16)
            outs = [inter[i * sub:(i + 1) * sub] + _dot(atts[i], vb[i * sub:(i + 1) * sub]) for i in range(nsub)]
            for i, j, cs in crosses:
                outs[i] = outs[i] + _dot(cs, vb[j * sub:(j + 1) * sub])
            o = jnp.concatenate(outs, axis=0)

            bl = b[chunk - 1:chunk, :]
            kt = k * jnp.exp(bl - b)
            st_ref[h] = st_ref[h] * jnp.exp(bl) + _dot_tn(vb, kt.astype(BF16))

            ms = jnp.mean(o * o, axis=-1, keepdims=True)
            on = o * lax.rsqrt(ms + EPS) * on_ref[...]
            oall_ref[:, hs] = (on * gt_ref[:, hs]).astype(BF16)

    b3 = b_ref[...].reshape(chunk // A_LOCAL, A_LOCAL, d)
    safe = jnp.max(b3[:, 0:1, :] - b3[:, A_LOCAL - 1:A_LOCAL, :]) <= A_SAFE

    @pl.when(safe)
    def _():
        heads(A_LOCAL)

    @pl.when(jnp.logical_not(safe))
    def _():
        heads(None)

    y = _dot(oall_ref[...], wout_ref[...])
    o_ref[0] = x + (1.0 + mod[2:3]) * y


def _hgrn2(x, mod, g, w_in, a_lb, onorm, w_out, idx):
    bn, s, d = x.shape
    chunk = A_CHUNK
    nh = d // HEAD_W
    act = pltpu.VMEM((chunk, d), F32)
    return pl.pallas_call(
        functools.partial(_hgrn2_kernel, idx=idx, sub=A_SUB),
        grid=(bn, s // chunk),
        in_specs=[
            pl.BlockSpec((1, chunk, d), lambda b, i: (b, i, 0)),
            pl.BlockSpec((1, 3, d), lambda b, i: (b, 0, 0)),
            _resident((1, d)),
            _resident_at((d, 4 * d), (idx,)),
            _resident(a_lb.shape),
            _resident((1, HEAD_W)),
            _resident_at((d, d), (idx,)),
        ],
        out_specs=pl.BlockSpec((1, chunk, d), lambda b, i: (b, i, 0)),
        out_shape=jax.ShapeDtypeStruct(x.shape, x.dtype),
        scratch_shapes=[pltpu.VMEM((nh, HEAD_W, HEAD_W), F32), act, act, act, act, act, act,
                        pltpu.VMEM((chunk, d), BF16)],
        compiler_params=_params(("arbitrary", "arbitrary"), 48),
        name="hgrn2",
    )(x, mod, g, w_in, a_lb, onorm, w_out)


def _conv_kernel(x_ref, mod_ref, g_ref, win_ref, cw_ref, wout_ref, o_ref, carry_ref):
    tm, d = x_ref.shape[1], x_ref.shape[2]

    @pl.when(pl.program_id(1) == 0)
    def _():
        carry_ref[...] = jnp.zeros_like(carry_ref)

    x = x_ref[0]
    mod = mod_ref[0]
    xn = _prenorm(x, g_ref[...], mod).astype(BF16)
    u = _dot(xn, win_ref[:, d:2 * d]) * _dot(xn, win_ref[:, 2 * d:3 * d])
    carry = carry_ref[...]
    row = lax.broadcasted_iota(jnp.int32, (tm, 1), 0)
    u1 = jnp.where(row == 0, carry[7:8], pltpu.roll(u, 1, 0))
    u2 = jnp.where(row == 0, carry[6:7], jnp.where(row == 1, carry[7:8], pltpu.roll(u, 2, 0)))
    cw = cw_ref[...]
    y = cw[0:1] * u2 + cw[1:2] * u1 + cw[2:3] * u
    carry_ref[...] = u[tm - 8:tm]
    z = (_dot(xn, win_ref[:, 0:d]) * y).astype(BF16)
    o_ref[0] = x + (1.0 + mod[2:3]) * _dot(z, wout_ref[...])


def _short_conv(x, mod, g, w_in, conv_w, w_out):
    bn, s, d = x.shape
    tm = MIX_TM
    return pl.pallas_call(
        _conv_kernel,
        grid=(bn, s // tm),
        in_specs=[
            pl.BlockSpec((1, tm, d), lambda b, i: (b, i, 0)),
            pl.BlockSpec((1, 3, d), lambda b, i: (b, 0, 0)),
            _resident((1, d)),
            _resident((d, 3 * d)),
            _resident(conv_w.shape),
            _resident((d, d)),
        ],
        out_specs=pl.BlockSpec((1, tm, d), lambda b, i: (b, i, 0)),
        out_shape=jax.ShapeDtypeStruct(x.shape, x.dtype),
        scratch_shapes=[pltpu.VMEM((8, d), F32)],
        compiler_params=_params(("arbitrary", "arbitrary"), 56),
        name="short_conv",
    )(x, mod, g, w_in, conv_w, w_out)


def _qkv_kernel(x_ref, mod_ref, g_ref, win_ref, gq_ref, gk_ref, bd_ref, pos_ref, invf_ref,
                q_ref, k_ref, v_ref):
    d = x_ref.shape[2]
    hw = HEAD_W
    xn = _prenorm(x_ref[0], g_ref[...], mod_ref[0]).astype(BF16)

    ang = pos_ref[0].astype(F32) * invf_ref[...]
    lane = lax.broadcasted_iota(jnp.int32, (1, hw), 1) & (B_DH - 1)
    cs, sn = jnp.cos(ang), jnp.sin(ang)
    ca = jnp.where(lane < B_ROT, cs, 1.0)
    cm = jnp.where(lane < B_ROT // 2, -sn, 0.0)
    cp = jnp.where((lane >= B_ROT // 2) & (lane < B_ROT), sn, 0.0)
    half = B_ROT // 2

    gw = 2 * hw
    for off, gain_ref, is_q in ((0, gq_ref, True), (d, gk_ref, False)):
        for j in range(d // gw):
            z = _dot(xn, win_ref[:, off + j * gw:off + (j + 1) * gw])
            ss = _dot((z * z).astype(BF16), bd_ref[...])
            zn = z * lax.rsqrt(ss * (1.0 / B_DH) + EPS) * gain_ref[:, j * gw:(j + 1) * gw]
            for t in range(gw // hw):
                zz = zn[:, t * hw:(t + 1) * hw]
                r = zz * ca + pltpu.roll(zz, hw - half, 1) * cm + pltpu.roll(zz, half, 1) * cp
                head = j * (gw // hw) + t
                if is_q:
                    q_ref[0, head] = (r * B_DH ** -0.5).astype(BF16)
                else:
                    k_ref[0, head, 0] = r.T.astype(BF16)
    v = _dot(xn, win_ref[:, 2 * d:3 * d]).astype(BF16)
    for head in range(d // hw):
        v_ref[0, head] = v[:, head * hw:(head + 1) * hw]


def _qkv(x, mod, g, w_in, qk_g, positions):
    bn, s, d = x.shape
    tm = MIX_TM
    reps = d // B_DH
    gq = jnp.tile(qk_g[0].astype(F32), reps)[None, :]
    gk = jnp.tile(qk_g[1].astype(F32), reps)[None, :]
    blk = jnp.arange(2 * HEAD_W) // B_DH
    bd = (blk[:, None] == blk[None, :]).astype(BF16)
    inv_freq = ROPE_THETA ** (-jnp.arange(0, B_ROT, 2, dtype=F32) / B_ROT)
    lane = jnp.arange(HEAD_W) % B_DH
    invf = jnp.where(lane < B_ROT, inv_freq[lane % (B_ROT // 2)], 0.0)[None, :]
    tok = pl.BlockSpec((1, tm, d), lambda b, i: (b, i, 0))
    nh = d // HEAD_W
    hm_spec = pl.BlockSpec((1, nh, tm, HEAD_W), lambda b, i: (b, 0, i, 0))
    hm_out = jax.ShapeDtypeStruct((bn, nh, s, HEAD_W), BF16)
    kt_spec = pl.BlockSpec((1, nh, 1, HEAD_W, tm), lambda b, i: (b, 0, i, 0, 0))
    kt_out = jax.ShapeDtypeStruct((bn, nh, s // tm, HEAD_W, tm), BF16)
    return pl.pallas_call(
        _qkv_kernel,
        grid=(bn, s // tm),
        in_specs=[
            tok,
            pl.BlockSpec((1, 3, d), lambda b, i: (b, 0, 0)),
            _resident((1, d)),
            _resident((d, 3 * d)),
            _resident((1, d)),
            _resident((1, d)),
            _resident((2 * HEAD_W, 2 * HEAD_W)),
            pl.BlockSpec((1, tm, 1), lambda b, i: (b, i, 0)),
            _resident((1, HEAD_W)),
        ],
        out_specs=[hm_spec, kt_spec, hm_spec],
        out_shape=[hm_out, kt_out, hm_out],
        compiler_params=_params(("arbitrary", "arbitrary"), 48),
        name="attn_qkv",
    )(x, mod, g, w_in, gq, gk, bd, positions.reshape(bn, s, 1), invf)


def _attn_kernel(q_ref, kt_ref, v_ref, lam_ref, sg_ref, o_ref, qs_ref, s_ref, m_ref, acc_ref, *,
                 lambda_init, full_rows, causal_rows):
    t = q_ref.shape[2]
    hw = HEAD_W
    i = pl.program_id(2)
    q = q_ref[0, 0]
    lane = lax.broadcasted_iota(jnp.int32, (1, hw), 1)
    qs_ref[0] = jnp.where(lane < B_DH, q, jnp.zeros_like(q))
    qs_ref[1] = jnp.where(lane >= B_DH, q, jnp.zeros_like(q))
    ones = jnp.ones((t, hw), BF16)

    def scores_to(slot, j):
        per = t // kt_ref.shape[-1]
        kt = jnp.concatenate([kt_ref[0, 0, j * per + a] for a in range(per)], axis=1)
        for c, rc in [(c, rc) for rc in range(t // full_rows) for c in range(2)]:
            rs = slice(rc * full_rows, (rc + 1) * full_rows)
            s_ref[slot, c, rs, :] = _dot(qs_ref[c, rs, :], kt)

    def softmax_pv(slot, j, causal):
        rows = causal_rows if causal else full_rows
        for c, rc in [(c, rc) for rc in range(t // rows) for c in range(2)]:
            rs = slice(rc * rows, (rc + 1) * rows)
            nk = hw * pl.cdiv((rc + 1) * rows, hw) if causal else t
            sc = s_ref[slot, c, rs, :nk]
            if causal:
                r = rc * rows + lax.broadcasted_iota(jnp.int32, (rows, nk), 0)
                cc = lax.broadcasted_iota(jnp.int32, (rows, nk), 1)
                sc = jnp.where(r >= cc, sc, -jnp.inf)
            cols = [sc[:, a * hw:(a + 1) * hw] for a in range(nk // hw)]
            rmax = jnp.max(functools.reduce(jnp.maximum, cols), axis=-1, keepdims=True)
            m_old = m_ref[c, rs, :]
            m_new = jnp.maximum(m_old, rmax)
            p = jnp.concatenate([jnp.exp(cl - m_new) for cl in cols], axis=1).astype(BF16)
            vx = jnp.concatenate([v_ref[0, 0, pl.ds(j * t, nk), :], ones[:nk]], axis=1)
            alpha = jnp.exp(m_old - m_new)
            acc_ref[c, rs, :] = acc_ref[c, rs, :] * jnp.concatenate([alpha, alpha], axis=1) + _dot(p, vx)
            m_ref[c, rs, :] = m_new

    scores_to(0, 0)
    m_ref[...] = jnp.full_like(m_ref, -jnp.inf)
    acc_ref[...] = jnp.zeros_like(acc_ref)

    def pair(p, carry):
        scores_to(1, 2 * p + 1)
        softmax_pv(0, 2 * p, False)
        scores_to(0, 2 * p + 2)
        softmax_pv(1, 2 * p + 1, False)
        return carry

    lax.fori_loop(0, i // 2, pair, 0)

    @pl.when(i % 2 == 1)
    def _():
        scores_to(1, i)
        softmax_pv(0, i - 1, False)

    softmax_pv(i % 2, i, True)

    lp = lam_ref[...]
    lam = (jnp.exp(jnp.sum(lp[0:1] * lp[1:2], axis=-1, keepdims=True))
           - jnp.exp(jnp.sum(lp[2:3] * lp[3:4], axis=-1, keepdims=True)) + lambda_init)
    a0, a1 = acc_ref[0], acc_ref[1]
    o = a0[:, :hw] / a0[:, hw:] - lam * (a1[:, :hw] / a1[:, hw:])
    ms = jnp.mean(o * o, axis=-1, keepdims=True)
    o_ref[0, 0] = (o * lax.rsqrt(ms + EPS) * sg_ref[...] * (1.0 - lambda_init)).astype(o_ref.dtype)


def _attention(q, kt, v, lam_p, subln, lambda_init):
    bn, nh, s, _ = q.shape
    t = ATT_T
    kb = kt.shape[-1]
    assert kt.shape == (bn, nh, s // kb, HEAD_W, kb) and t % kb == 0
    tile = pl.BlockSpec((1, 1, t, HEAD_W), lambda b, h, i: (b, h, i, 0))
    return pl.pallas_call(
        functools.partial(_attn_kernel, lambda_init=lambda_init, full_rows=ATT_R, causal_rows=ATT_CAUSAL_R),
        grid=(bn, nh, s // t),
        in_specs=[
            tile,
            pl.BlockSpec((1, 1, s // kb, HEAD_W, kb), lambda b, h, i: (b, h, 0, 0, 0)),
            pl.BlockSpec((1, 1, s, HEAD_W), lambda b, h, i: (b, h, 0, 0)),
            _resident(lam_p.shape),
            _resident((1, HEAD_W)),
        ],
        out_specs=tile,
        out_shape=jax.ShapeDtypeStruct((bn, nh, s, HEAD_W), BF16),
        scratch_shapes=[pltpu.VMEM((2, t, HEAD_W), BF16), pltpu.VMEM((2, 2, t, t), F32),
                        pltpu.VMEM((2, t, HEAD_W), F32),
                        pltpu.VMEM((2, t, 2 * HEAD_W), F32)],
        compiler_params=_params(("arbitrary", "arbitrary", "arbitrary"), 56),
        name="attn_core",
    )(q, kt, v, lam_p, subln)


def _outproj_kernel(a_ref, x_ref, mod_ref, w_ref, o_ref):
    a = jnp.concatenate([a_ref[0, h] for h in range(a_ref.shape[1])], axis=1)
    o_ref[0] = x_ref[0] + (1.0 + mod_ref[0][2:3]) * _dot(a, w_ref[...])


def _outproj(a, x, mod, w_out):
    bn, s, d = x.shape
    tm = MIX_TM
    tok = pl.BlockSpec((1, tm, d), lambda b, i: (b, i, 0))
    heads = pl.BlockSpec((1, d // HEAD_W, tm, HEAD_W), lambda b, i: (b, 0, i, 0))
    return pl.pallas_call(
        _outproj_kernel,
        grid=(bn, s // tm),
        in_specs=[heads, tok, pl.BlockSpec((1, 3, d), lambda b, i: (b, 0, 0)), _resident((d, d))],
        out_specs=tok,
        out_shape=jax.ShapeDtypeStruct(x.shape, x.dtype),
        compiler_params=_params(("arbitrary", "arbitrary"), 32),
        name="attn_out",
    )(a, x, mod, w_out)


def kernel(x, c, positions, ada_w, ada_b, norm_g, ffn_wi, ffn_wo, a_w_in, a_w_out, a_lb, a_onorm,
           b_w_in, b_w_out, b_qk_g, b_lam, b_subln, c_w_in, c_conv, c_w_out):
    bn, s, d = x.shape
    depth = ada_w.shape[0]
    mod = _adaln(c, ada_w, ada_b).reshape(depth, bn, N_SUB, 3, d)
    wi, wo = ffn_wi.astype(BF16), ffn_wo.astype(BF16)
    a_in, a_out = a_w_in.astype(BF16), a_w_out.astype(BF16)

    for l in range(depth):
        def sub(j):
            return mod[l, :, j], norm_g[l, j][None, :]

        m, g = sub(0)
        x = _ffn(x, m, g, wi, wo, (l, 0))
        m, g = sub(1)
        kind, idx = l % N_MIXERS, l // N_MIXERS
        if kind == 0:
            x = _hgrn2(x, m, g, a_in, a_lb.astype(F32), a_onorm[idx][None, :].astype(F32), a_out, idx)
        elif kind == 1:
            lambda_init = 0.8 - 0.6 * math.exp(-0.3 * l)
            q, k, v = _qkv(x, m, g, b_w_in[idx].astype(BF16), b_qk_g[idx], positions)
            a = _attention(q, k, v, b_lam[idx].astype(F32), b_subln[idx][None, :].astype(F32), lambda_init)
            x = _outproj(a, x, m, b_w_out[idx].astype(BF16))
        else:
            x = _short_conv(x, m, g, c_w_in[idx].astype(BF16), c_conv[idx].astype(F32), c_w_out[idx].astype(BF16))
        m, g = sub(2)
        x = _ffn(x, m, g, wi, wo, (l, 1))
    return x
```

```python
import functools
import math

import jax
import jax.numpy as jnp
from jax import lax
from jax.experimental import pallas as pl
from jax.experimental.pallas import tpu as pltpu

F32 = jnp.float32
BF16 = jnp.bfloat16

EPS = 1e-6
N_SUB = 3
N_MIXERS = 3
HEAD_W = 128
N_HEADS = 8
B_DH = 64
B_ROT = 16
ROPE_THETA = 500000.0
MIB = 1024 * 1024

FFN_TM = 1024
FFN_PARTS = 4
FFN_FC = 256
MIX_TM = 1024
A_CHUNK = 256
A_TILE = 1024
A_SUB = 128
A_LOCAL = 32
A_SAFE = 60.0
ATT_T = 1024
ATT_R = 512
ATT_CAUSAL_R = 256


def _silu(z):
    return z * jax.nn.sigmoid(z)


def _dot(a, b):
    return jnp.dot(a, b, preferred_element_type=F32)


def _dot_nt(a, b):
    return lax.dot_general(a, b, (((1,), (1,)), ((), ())), preferred_element_type=F32)


def _dot_tn(a, b):
    return lax.dot_general(a, b, (((0,), (0,)), ((), ())), preferred_element_type=F32)


def _prenorm(x, g, mod):
    ms = jnp.mean(x * x, axis=-1, keepdims=True)
    y = x * lax.rsqrt(ms + EPS) * g
    return y * (1.0 + mod[1:2]) + mod[0:1]


def _resident(shape):
    nd = len(shape)
    return pl.BlockSpec(shape, lambda *_: (0,) * nd, pipeline_mode=pl.Buffered(1))


def _resident_at(shape, lead):
    nd = len(shape)
    return pl.BlockSpec((None,) * len(lead) + tuple(shape), lambda *_: tuple(lead) + (0,) * nd,
                        pipeline_mode=pl.Buffered(1))


def _params(sem, vmem_mib):
    return pltpu.CompilerParams(dimension_semantics=sem, vmem_limit_bytes=vmem_mib * MIB)


def _adaln_kernel(c_ref, w_ref, b_ref, o_ref):
    ca = _silu(c_ref[...]).astype(BF16)
    o_ref[0] = _dot(ca, w_ref[0].astype(BF16)) + b_ref[0]


def _adaln(c, ada_w, ada_b):
    depth, d, n = ada_w.shape
    bn = c.shape[0]
    rows = 8 * pl.cdiv(bn, 8)
    cp = jnp.zeros((rows, d), F32).at[:bn].set(c)
    tn = n // 6
    out = pl.pallas_call(
        _adaln_kernel,
        grid=(depth, n // tn),
        in_specs=[
            pl.BlockSpec((rows, d), lambda l, j: (0, 0)),
            pl.BlockSpec((1, d, tn), lambda l, j: (l, 0, j)),
            pl.BlockSpec((1, 1, tn), lambda l, j: (l, 0, j)),
        ],
        out_specs=pl.BlockSpec((1, rows, tn), lambda l, j: (l, 0, j)),
        out_shape=jax.ShapeDtypeStruct((depth, rows, n), F32),
        compiler_params=_params(("arbitrary", "arbitrary"), 40),
        name="adaln",
    )(cp, ada_w, ada_b.reshape(depth, 1, n))
    return out[:, :bn]


def _ffn_kernel(x_ref, mod_ref, g_ref, wi_ref, wo_ref, o_ref, h_ref, *, fc, parts):
    mod = mod_ref[0]
    ff = h_ref.shape[1]
    rows = x_ref.shape[1] // parts

    def hidden(p):
        rs = slice(p * rows, (p + 1) * rows)
        xn = _prenorm(x_ref[0, rs, :], g_ref[...], mod).astype(BF16)
        for c in range(ff // fc):
            gate = _dot(xn, wi_ref[:, c * fc:(c + 1) * fc])
            up = _dot(xn, wi_ref[:, ff + c * fc:ff + (c + 1) * fc])
            h_ref[rs, c * fc:(c + 1) * fc] = (_silu(gate) * up).astype(BF16)

    def project(p):
        rs = slice(p * rows, (p + 1) * rows)
        y = _dot(h_ref[rs, :], wo_ref[...])
        o_ref[0, rs, :] = x_ref[0, rs, :] + (0.5 * (1.0 + mod[2:3])) * y

    hidden(0)
    for p in range(1, parts):
        hidden(p)
        project(p - 1)
    project(parts - 1)


def _ffn(x, mod, g, wi, wo, which):
    bn, s, d = x.shape
    ff = wo.shape[-2]
    tm, fc = FFN_TM, FFN_FC
    return pl.pallas_call(
        functools.partial(_ffn_kernel, fc=fc, parts=FFN_PARTS),
        grid=(bn, s // tm),
        in_specs=[
            pl.BlockSpec((1, tm, d), lambda b, i: (b, i, 0)),
            pl.BlockSpec((1, 3, d), lambda b, i: (b, 0, 0)),
            _resident((1, d)),
            _resident_at((d, 2 * ff), which),
            _resident_at((ff, d), which),
        ],
        out_specs=pl.BlockSpec((1, tm, d), lambda b, i: (b, i, 0)),
        out_shape=jax.ShapeDtypeStruct(x.shape, x.dtype),
        scratch_shapes=[pltpu.VMEM((tm, ff), BF16)],
        compiler_params=_params(("arbitrary", "arbitrary"), 56),
        name="ffn",
    )(x, mod, g, wi, wo)


def _block_masks(n, local):
    row = lax.broadcasted_iota(jnp.int32, (n, n), 0)
    col = lax.broadcasted_iota(jnp.int32, (n, n), 1)
    sh = int(math.log2(local))
    masks = {"diag": row == col, "local": ((row >> sh) == (col >> sh)) & (row >= col)}
    w = n // 2
    while w >= 1:
        sh = int(math.log2(w))
        rb, cb = row >> sh, col >> sh
        masks[w] = (rb == cb + 1) & ((cb & 1) == 0)
        w //= 2
    return masks


def _block_scores(q, k, lf, b, masks, local):
    n, hw = q.shape
    if local:
        b3 = b.reshape(n // local, local, hw)
        first = b3[:, 0:1, :]
        qe = (q * jnp.exp(b3 - first).reshape(n, hw)).astype(BF16)
        ke = (k * jnp.exp(first - b3).reshape(n, hw)).astype(BF16)
        att = jnp.where(masks["local"], _dot_nt(qe, ke), 0.0)
    else:
        att = jnp.where(masks["diag"], _dot_nt(q.astype(BF16), k.astype(BF16)), 0.0)
    w = n // 2
    while w >= (local or 8):
        b3 = b.reshape(n // w, w, hw)
        last = b3[:, w - 1:w, :]
        prev = jnp.concatenate([last[:1], last[:-1]], axis=0)
        qa = jnp.minimum(b3 - prev, 0.0).reshape(n, hw)
        ka = (last - b3).reshape(n, hw)
        sc = _dot_nt((q * jnp.exp(qa)).astype(BF16), (k * jnp.exp(ka)).astype(BF16))
        att = jnp.where(masks[w], sc, att)
        w //= 2
    if local:
        return att
    t = lax.broadcasted_iota(jnp.int32, (n, 1), 0)
    back = [None] + [pltpu.roll(lf, i, 0) for i in range(1, 4)]
    fwd = [None] + [pltpu.roll(lf, n - i, 0) for i in range(1, 4)]
    while w >= 1:
        r = t & (w - 1)
        qa = lf
        ke = k
        if w > 1:
            ka = jnp.where(r <= w - 2, fwd[1], 0.0)
            for i in range(1, w):
                qa = qa + jnp.where(r >= i, back[i], 0.0)
            for i in range(2, w):
                ka = ka + jnp.where(r <= w - 1 - i, fwd[i], 0.0)
            ke = k * jnp.exp(ka)
        sc = _dot_nt((q * jnp.exp(qa)).astype(BF16), ke.astype(BF16))
        att = jnp.where(masks[w], sc, att)
        w //= 2
    return att


def _hgrn2_kernel(x_ref, mod_ref, g_ref, win_ref, alb_ref, on_ref, wout_ref, o_ref, st_ref, *scratch,
                  chunk, idx, sub):
    @pl.when(pl.program_id(1) == 0)
    def _():
        st_ref[...] = jnp.zeros_like(st_ref)

    def one(ci, carry):
        rows = pl.ds(pl.multiple_of(ci * chunk, chunk), chunk)
        _hgrn2_chunk(x_ref.at[:, rows, :], mod_ref, g_ref, win_ref, alb_ref, on_ref, wout_ref,
                     o_ref.at[:, rows, :], st_ref, *scratch, idx=idx, sub=sub)
        return carry

    lax.fori_loop(0, x_ref.shape[1] // chunk, one, 0)


def _hgrn2_chunk(x_ref, mod_ref, g_ref, win_ref, alb_ref, on_ref, wout_ref, o_ref,
                 st_ref, q_ref, k_ref, lf_ref, b_ref, v_ref, gt_ref, oall_ref, *, idx, sub):
    chunk, d = x_ref.shape[1], x_ref.shape[2]
    hw = HEAD_W
    x = x_ref[0]
    mod = mod_ref[0]
    xn = _prenorm(x, g_ref[...], mod).astype(BF16)

    alb = alb_ref[...]
    e = jnp.exp(alb - jnp.max(alb, axis=0, keepdims=True))
    sm = e / jnp.sum(e, axis=0, keepdims=True)
    lb = jnp.sum(sm[0:idx + 1], axis=0, keepdims=True) - sm[0:1]

    q_ref[...] = _silu(_dot(xn, win_ref[:, 0:d]))
    f = lb + (1.0 - lb) * jax.nn.sigmoid(_dot(xn, win_ref[:, d:2 * d]))
    k_ref[...] = 1.0 - f
    v_ref[...] = _dot(xn, win_ref[:, 2 * d:3 * d])
    gt_ref[...] = _silu(_dot(xn, win_ref[:, 3 * d:4 * d]))

    lf = jnp.log(f)
    lf_ref[...] = lf
    hi = lf.astype(BF16)
    r1 = lf - hi.astype(F32)
    mid = r1.astype(BF16)
    lo = (r1 - mid.astype(F32)).astype(BF16)
    row = lax.broadcasted_iota(jnp.int32, (chunk, chunk), 0)
    col = lax.broadcasted_iota(jnp.int32, (chunk, chunk), 1)
    tri = (row >= col).astype(BF16)
    b_ref[...] = _dot(tri, hi) + _dot(tri, mid) + _dot(tri, lo)

    masks = _block_masks(sub, A_LOCAL)
    nsub = chunk // sub

    def heads(local):
        staged = []
        for h in range(d // hw):
            hs = slice(h * hw, (h + 1) * hw)
            q, k, lf, b = q_ref[:, hs], k_ref[:, hs], lf_ref[:, hs], b_ref[:, hs]
            inter = _dot_nt((q * jnp.exp(b)).astype(BF16), st_ref[h].astype(BF16))
            atts, crosses = [], []
            for i in range(nsub):
                rs = slice(i * sub, (i + 1) * sub)
                atts.append(_block_scores(q[rs], k[rs], lf[rs], b[rs], masks, local).astype(BF16))
                if i > 0:
                    edge = b[i * sub - 1:i * sub, :]
                    qc = (q[rs] * jnp.exp(b[rs] - edge)).astype(BF16)
                    for j in range(i):
                        js = slice(j * sub, (j + 1) * sub)
                        kc = (k[js] * jnp.exp(edge - b[js])).astype(BF16)
                        crosses.append((i, j, _dot_nt(qc, kc).astype(BF16)))
            staged.append((inter, atts, crosses))
        for h in range(d // hw):
            hs = slice(h * hw, (h + 1) * hw)
            inter, atts, crosses = staged[h]
            k, b = k_ref[:, hs], b_ref[:, hs]
            vb = v_ref[:, hs].astype(BF16)
            outs = [inter[i * sub:(i + 1) * sub] + _dot(atts[i], vb[i * sub:(i + 1) * sub]) for i in range(nsub)]
            for i, j, cs in crosses:
                outs[i] = outs[i] + _dot(cs, vb[j * sub:(j + 1) * sub])
            o = jnp.concatenate(outs, axis=0)

            bl = b[chunk - 1:chunk, :]
            kt = k * jnp.exp(bl - b)
            st_ref[h] = st_ref[h] * jnp.exp(bl) + _dot_tn(vb, kt.astype(BF16))

            ms = jnp.mean(o * o, axis=-1, keepdims=True)
            on = o * lax.rsqrt(ms + EPS) * on_ref[...]
            oall_ref[:, hs] = (on * gt_ref[:, hs]).astype(BF16)

    b3 = b_ref[...].reshape(chunk // A_LOCAL, A_LOCAL, d)
    safe = jnp.max(b3[:, 0:1, :] - b3[:, A_LOCAL - 1:A_LOCAL, :]) <= A_SAFE

    @pl.when(safe)
    def _():
        heads(A_LOCAL)

    @pl.when(jnp.logical_not(safe))
    def _():
        heads(None)

    y = _dot(oall_ref[...], wout_ref[...])
    o_ref[0] = x + (1.0 + mod[2:3]) * y


def _hgrn2(x, mod, g, w_in, a_lb, onorm, w_out, idx):
    bn, s, d = x.shape
    chunk, tile = A_CHUNK, A_TILE
    nh = d // HEAD_W
    act = pltpu.VMEM((chunk, d), F32)
    return pl.pallas_call(
        functools.partial(_hgrn2_kernel, chunk=chunk, idx=idx, sub=A_SUB),
        grid=(bn, s // tile),
        in_specs=[
            pl.BlockSpec((1, tile, d), lambda b, i: (b, i, 0)),
            pl.BlockSpec((1, 3, d), lambda b, i: (b, 0, 0)),
            _resident((1, d)),
            _resident_at((d, 4 * d), (idx,)),
            _resident(a_lb.shape),
            _resident((1, HEAD_W)),
            _resident_at((d, d), (idx,)),
        ],
        out_specs=pl.BlockSpec((1, tile, d), lambda b, i: (b, i, 0)),
        out_shape=jax.ShapeDtypeStruct(x.shape, x.dtype),
        scratch_shapes=[pltpu.VMEM((nh, HEAD_W, HEAD_W), F32), act, act, act, act, act, act,
                        pltpu.VMEM((chunk, d), BF16)],
        compiler_params=_params(("arbitrary", "arbitrary"), 48),
        name="hgrn2",
    )(x, mod, g, w_in, a_lb, onorm, w_out)


def _conv_kernel(x_ref, mod_ref, g_ref, win_ref, cw_ref, wout_ref, o_ref, carry_ref):
    tm, d = x_ref.shape[1], x_ref.shape[2]

    @pl.when(pl.program_id(1) == 0)
    def _():
        carry_ref[...] = jnp.zeros_like(carry_ref)

    x = x_ref[0]
    mod = mod_ref[0]
    xn = _prenorm(x, g_ref[...], mod).astype(BF16)
    u = _dot(xn, win_ref[:, d:2 * d]) * _dot(xn, win_ref[:, 2 * d:3 * d])
    carry = carry_ref[...]
    row = lax.broadcasted_iota(jnp.int32, (tm, 1), 0)
    u1 = jnp.where(row == 0, carry[7:8], pltpu.roll(u, 1, 0))
    u2 = jnp.where(row == 0, carry[6:7], jnp.where(row == 1, carry[7:8], pltpu.roll(u, 2, 0)))
    cw = cw_ref[...]
    y = cw[0:1] * u2 + cw[1:2] * u1 + cw[2:3] * u
    carry_ref[...] = u[tm - 8:tm]
    z = (_dot(xn, win_ref[:, 0:d]) * y).astype(BF16)
    o_ref[0] = x + (1.0 + mod[2:3]) * _dot(z, wout_ref[...])


def _short_conv(x, mod, g, w_in, conv_w, w_out):
    bn, s, d = x.shape
    tm = MIX_TM
    return pl.pallas_call(
        _conv_kernel,
        grid=(bn, s // tm),
        in_specs=[
            pl.BlockSpec((1, tm, d), lambda b, i: (b, i, 0)),
            pl.BlockSpec((1, 3, d), lambda b, i: (b, 0, 0)),
            _resident((1, d)),
            _resident((d, 3 * d)),
            _resident(conv_w.shape),
            _resident((d, d)),
        ],
        out_specs=pl.BlockSpec((1, tm, d), lambda b, i: (b, i, 0)),
        out_shape=jax.ShapeDtypeStruct(x.shape, x.dtype),
        scratch_shapes=[pltpu.VMEM((8, d), F32)],
        compiler_params=_params(("arbitrary", "arbitrary"), 56),
        name="short_conv",
    )(x, mod, g, w_in, conv_w, w_out)


def _qkv_kernel(x_ref, mod_ref, g_ref, win_ref, gq_ref, gk_ref, bd_ref, pos_ref, invf_ref,
                q_ref, k_ref, v_ref):
    d = x_ref.shape[2]
    hw = HEAD_W
    xn = _prenorm(x_ref[0], g_ref[...], mod_ref[0]).astype(BF16)

    ang = pos_ref[0].astype(F32) * invf_ref[...]
    lane = lax.broadcasted_iota(jnp.int32, (1, hw), 1) & (B_DH - 1)
    cs, sn = jnp.cos(ang), jnp.sin(ang)
    ca = jnp.where(lane < B_ROT, cs, 1.0)
    cm = jnp.where(lane < B_ROT // 2, -sn, 0.0)
    cp = jnp.where((lane >= B_ROT // 2) & (lane < B_ROT), sn, 0.0)
    half = B_ROT // 2

    gw = 2 * hw
    for off, gain_ref, is_q in ((0, gq_ref, True), (d, gk_ref, False)):
        for j in range(d // gw):
            z = _dot(xn, win_ref[:, off + j * gw:off + (j + 1) * gw])
            ss = _dot((z * z).astype(BF16), bd_ref[...])
            zn = z * lax.rsqrt(ss * (1.0 / B_DH) + EPS) * gain_ref[:, j * gw:(j + 1) * gw]
            for t in range(gw // hw):
                zz = zn[:, t * hw:(t + 1) * hw]
                r = zz * ca + pltpu.roll(zz, hw - half, 1) * cm + pltpu.roll(zz, half, 1) * cp
                head = j * (gw // hw) + t
                if is_q:
                    q_ref[0, head] = (r * B_DH ** -0.5).astype(BF16)
                else:
                    k_ref[0, head, 0] = r.T.astype(BF16)
    v = _dot(xn, win_ref[:, 2 * d:3 * d]).astype(BF16)
    for head in range(d // hw):
        v_ref[0, head] = v[:, head * hw:(head + 1) * hw]


def _qkv(x, mod, g, w_in, qk_g, positions):
    bn, s, d = x.shape
    tm = MIX_TM
    reps = d // B_DH
    gq = jnp.tile(qk_g[0].astype(F32), reps)[None, :]
    gk = jnp.tile(qk_g[1].astype(F32), reps)[None, :]
    blk = jnp.arange(2 * HEAD_W) // B_DH
    bd = (blk[:, None] == blk[None, :]).astype(BF16)
    inv_freq = ROPE_THETA ** (-jnp.arange(0, B_ROT, 2, dtype=F32) / B_ROT)
    lane = jnp.arange(HEAD_W) % B_DH
    invf = jnp.where(lane < B_ROT, inv_freq[lane % (B_ROT // 2)], 0.0)[None, :]
    tok = pl.BlockSpec((1, tm, d), lambda b, i: (b, i, 0))
    nh = d // HEAD_W
    hm_spec = pl.BlockSpec((1, nh, tm, HEAD_W), lambda b, i: (b, 0, i, 0))
    hm_out = jax.ShapeDtypeStruct((bn, nh, s, HEAD_W), BF16)
    kt_spec = pl.BlockSpec((1, nh, 1, HEAD_W, tm), lambda b, i: (b, 0, i, 0, 0))
    kt_out = jax.ShapeDtypeStruct((bn, nh, s // tm, HEAD_W, tm), BF16)
    return pl.pallas_call(
        _qkv_kernel,
        grid=(bn, s // tm),
        in_specs=[
            tok,
            pl.BlockSpec((1, 3, d), lambda b, i: (b, 0, 0)),
            _resident((1, d)),
            _resident((d, 3 * d)),
            _resident((1, d)),
            _resident((1, d)),
            _resident((2 * HEAD_W, 2 * HEAD_W)),
            pl.BlockSpec((1, tm, 1), lambda b, i: (b, i, 0)),
            _resident((1, HEAD_W)),
        ],
        out_specs=[hm_spec, kt_spec, hm_spec],
        out_shape=[hm_out, kt_out, hm_out],
        compiler_params=_params(("arbitrary", "arbitrary"), 48),
        name="attn_qkv",
    )(x, mod, g, w_in, gq, gk, bd, positions.reshape(bn, s, 1), invf)


def _attn_kernel(q_ref, kt_ref, v_ref, lam_ref, sg_ref, o_ref, qs_ref, s_ref, m_ref, acc_ref, *,
                 lambda_init, full_rows, causal_rows):
    t = q_ref.shape[2]
    hw = HEAD_W
    i = pl.program_id(2)
    q = q_ref[0, 0]
    lane = lax.broadcasted_iota(jnp.int32, (1, hw), 1)
    qs_ref[0] = jnp.where(lane < B_DH, q, jnp.zeros_like(q))
    qs_ref[1] = jnp.where(lane >= B_DH, q, jnp.zeros_like(q))
    ones = jnp.ones((t, hw), BF16)

    def scores_to(slot, j):
        per = t // kt_ref.shape[-1]
        kt = jnp.concatenate([kt_ref[0, 0, j * per + a] for a in range(per)], axis=1)
        for c, rc in [(c, rc) for rc in range(t // full_rows) for c in range(2)]:
            rs = slice(rc * full_rows, (rc + 1) * full_rows)
            s_ref[slot, c, rs, :] = _dot(qs_ref[c, rs, :], kt)

    def softmax_pv(slot, j, causal):
        rows = causal_rows if causal else full_rows
        for c, rc in [(c, rc) for rc in range(t // rows) for c in range(2)]:
            rs = slice(rc * rows, (rc + 1) * rows)
            nk = hw * pl.cdiv((rc + 1) * rows, hw) if causal else t
            sc = s_ref[slot, c, rs, :nk]
            if causal:
                r = rc * rows + lax.broadcasted_iota(jnp.int32, (rows, nk), 0)
                cc = lax.broadcasted_iota(jnp.int32, (rows, nk), 1)
                sc = jnp.where(r >= cc, sc, -jnp.inf)
            cols = [sc[:, a * hw:(a + 1) * hw] for a in range(nk // hw)]
            rmax = jnp.max(functools.reduce(jnp.maximum, cols), axis=-1, keepdims=True)
            m_old = m_ref[c, rs, :]
            m_new = jnp.maximum(m_old, rmax)
            p = jnp.concatenate([jnp.exp(cl - m_new) for cl in cols], axis=1).astype(BF16)
            vx = jnp.concatenate([v_ref[0, 0, pl.ds(j * t, nk), :], ones[:nk]], axis=1)
            alpha = jnp.exp(m_old - m_new)
            acc_ref[c, rs, :] = acc_ref[c, rs, :] * jnp.concatenate([alpha, alpha], axis=1) + _dot(p, vx)
            m_ref[c, rs, :] = m_new

    scores_to(0, 0)
    m_ref[...] = jnp.full_like(m_ref, -jnp.inf)
    acc_ref[...] = jnp.zeros_like(acc_ref)

    def pair(p, carry):
        scores_to(1, 2 * p + 1)
        softmax_pv(0, 2 * p, False)
        scores_to(0, 2 * p + 2)
        softmax_pv(1, 2 * p + 1, False)
        return carry

    lax.fori_loop(0, i // 2, pair, 0)

    @pl.when(i % 2 == 1)
    def _():
        scores_to(1, i)
        softmax_pv(0, i - 1, False)

    softmax_pv(i % 2, i, True)

    lp = lam_ref[...]
    lam = (jnp.exp(jnp.sum(lp[0:1] * lp[1:2], axis=-1, keepdims=True))
           - jnp.exp(jnp.sum(lp[2:3] * lp[3:4], axis=-1, keepdims=True)) + lambda_init)
    a0, a1 = acc_ref[0], acc_ref[1]
    o = a0[:, :hw] / a0[:, hw:] - lam * (a1[:, :hw] / a1[:, hw:])
    ms = jnp.mean(o * o, axis=-1, keepdims=True)
    o_ref[0, 0] = (o * lax.rsqrt(ms + EPS) * sg_ref[...] * (1.0 - lambda_init)).astype(o_ref.dtype)


def _attention(q, kt, v, lam_p, subln, lambda_init):
    bn, nh, s, _ = q.shape
    t = ATT_T
    kb = kt.shape[-1]
    assert kt.shape == (bn, nh, s // kb, HEAD_W, kb) and t % kb == 0
    tile = pl.BlockSpec((1, 1, t, HEAD_W), lambda b, h, i: (b, h, i, 0))
    return pl.pallas_call(
        functools.partial(_attn_kernel, lambda_init=lambda_init, full_rows=ATT_R, causal_rows=ATT_CAUSAL_R),
        grid=(bn, nh, s // t),
        in_specs=[
            tile,
            pl.BlockSpec((1, 1, s // kb, HEAD_W, kb), lambda b, h, i: (b, h, 0, 0, 0)),
            pl.BlockSpec((1, 1, s, HEAD_W), lambda b, h, i: (b, h, 0, 0)),
            _resident(lam_p.shape),
            _resident((1, HEAD_W)),
        ],
        out_specs=tile,
        out_shape=jax.ShapeDtypeStruct((bn, nh, s, HEAD_W), BF16),
        scratch_shapes=[pltpu.VMEM((2, t, HEAD_W), BF16), pltpu.VMEM((2, 2, t, t), F32),
                        pltpu.VMEM((2, t, HEAD_W), F32),
                        pltpu.VMEM((2, t, 2 * HEAD_W), F32)],
        compiler_params=_params(("arbitrary", "arbitrary", "arbitrary"), 56),
        name="attn_core",
    )(q, kt, v, lam_p, subln)


def _outproj_kernel(a_ref, x_ref, mod_ref, w_ref, o_ref):
    a = jnp.concatenate([a_ref[0, h] for h in range(a_ref.shape[1])], axis=1)
    o_ref[0] = x_ref[0] + (1.0 + mod_ref[0][2:3]) * _dot(a, w_ref[...])


def _outproj(a, x, mod, w_out):
    bn, s, d = x.shape
    tm = MIX_TM
    tok = pl.BlockSpec((1, tm, d), lambda b, i: (b, i, 0))
    heads = pl.BlockSpec((1, d // HEAD_W, tm, HEAD_W), lambda b, i: (b, 0, i, 0))
    return pl.pallas_call(
        _outproj_kernel,
        grid=(bn, s // tm),
        in_specs=[heads, tok, pl.BlockSpec((1, 3, d), lambda b, i: (b, 0, 0)), _resident((d, d))],
        out_specs=tok,
        out_shape=jax.ShapeDtypeStruct(x.shape, x.dtype),
        compiler_params=_params(("arbitrary", "arbitrary"), 32),
        name="attn_out",
    )(a, x, mod, w_out)


def kernel(x, c, positions, ada_w, ada_b, norm_g, ffn_wi, ffn_wo, a_w_in, a_w_out, a_lb, a_onorm,
           b_w_in, b_w_out, b_qk_g, b_lam, b_subln, c_w_in, c_conv, c_w_out):
    bn, s, d = x.shape
    depth = ada_w.shape[0]
    mod = _adaln(c, ada_w, ada_b).reshape(depth, bn, N_SUB, 3, d)
    wi, wo = ffn_wi.astype(BF16), ffn_wo.astype(BF16)
    a_in, a_out = a_w_in.astype(BF16), a_w_out.astype(BF16)

    for l in range(depth):
        def sub(j):
            return mod[l, :, j], norm_g[l, j][None, :]

        m, g = sub(0)
        x = _ffn(x, m, g, wi, wo, (l, 0))
        m, g = sub(1)
        kind, idx = l % N_MIXERS, l // N_MIXERS
        if kind == 0:
            x = _hgrn2(x, m, g, a_in, a_lb.astype(F32), a_onorm[idx][None, :].astype(F32), a_out, idx)
        elif kind == 1:
            lambda_init = 0.8 - 0.6 * math.exp(-0.3 * l)
            q, k, v = _qkv(x, m, g, b_w_in[idx].astype(BF16), b_qk_g[idx], positions)
            a = _attention(q, k, v, b_lam[idx].astype(F32), b_subln[idx][None, :].astype(F32), lambda_init)
            x = _outproj(a, x, m, b_w_out[idx].astype(BF16))
        else:
            x = _short_conv(x, m, g, c_w_in[idx].astype(BF16), c_conv[idx].astype(F32), c_w_out[idx].astype(BF16))
        m, g = sub(2)
        x = _ffn(x, m, g, wi, wo, (l, 1))
    return x
```

```python
import functools
import math

import jax
import jax.numpy as jnp
from jax import lax
from jax.experimental import pallas as pl
from jax.experimental.pallas import tpu as pltpu

F32 = jnp.float32
BF16 = jnp.bfloat16

EPS = 1e-6
N_SUB = 3
N_MIXERS = 3
HEAD_W = 128
SUBLANES = 8
B_DH = 64
B_ROT = 16
ROPE_THETA = 500000.0
MIB = 1024 * 1024

VMEM_MIB = {"adaln": 40, "ffn": 56, "hgrn2": 48, "short_conv": 56, "attn_qkv": 48, "attn_core": 56,
            "attn_out": 32}
ADALN_COL_TILES = 6

FFN_TM = 1024
FFN_PARTS = 8
FFN_FC = 256
MIX_TM = 1024
A_CHUNK = 256
A_SUB = 128
A_LOCAL = 32
A_SAFE = 60.0
ATT_T = 1024
ATT_SCORE_R = 1024
ATT_R = 512
ATT_CAUSAL_R = 128


def _silu(z):
    return z * jax.nn.sigmoid(z)


def _dot(a, b):
    return jnp.dot(a, b, preferred_element_type=F32)


def _dot_nt(a, b):
    return lax.dot_general(a, b, (((1,), (1,)), ((), ())), preferred_element_type=F32)


def _dot_tn(a, b):
    return lax.dot_general(a, b, (((0,), (0,)), ((), ())), preferred_element_type=F32)


def _prenorm(x, g, mod):
    ms = jnp.mean(x * x, axis=-1, keepdims=True)
    y = x * lax.rsqrt(ms + EPS) * g
    return y * (1.0 + mod[1:2]) + mod[0:1]


def _resident(shape):
    nd = len(shape)
    return pl.BlockSpec(shape, lambda *_: (0,) * nd, pipeline_mode=pl.Buffered(1))


def _resident_at(shape, lead):
    nd = len(shape)
    return pl.BlockSpec((None,) * len(lead) + tuple(shape), lambda *_: tuple(lead) + (0,) * nd,
                        pipeline_mode=pl.Buffered(1))


def _params(name, grid_rank):
    return pltpu.CompilerParams(dimension_semantics=("arbitrary",) * grid_rank,
                                vmem_limit_bytes=VMEM_MIB[name] * MIB)


def _adaln_kernel(c_ref, w_ref, b_ref, o_ref):
    ca = _silu(c_ref[...]).astype(BF16)
    o_ref[0] = _dot(ca, w_ref[0].astype(BF16)) + b_ref[0]


def _adaln(c, ada_w, ada_b):
    depth, d, n = ada_w.shape
    bn = c.shape[0]
    rows = SUBLANES * pl.cdiv(bn, SUBLANES)
    cp = jnp.zeros((rows, d), F32).at[:bn].set(c)
    tn = n // ADALN_COL_TILES
    out = pl.pallas_call(
        _adaln_kernel,
        grid=(depth, n // tn),
        in_specs=[
            pl.BlockSpec((rows, d), lambda l, j: (0, 0)),
            pl.BlockSpec((1, d, tn), lambda l, j: (l, 0, j)),
            pl.BlockSpec((1, 1, tn), lambda l, j: (l, 0, j)),
        ],
        out_specs=pl.BlockSpec((1, rows, tn), lambda l, j: (l, 0, j)),
        out_shape=jax.ShapeDtypeStruct((depth, rows, n), F32),
        compiler_params=_params("adaln", 2),
        name="adaln",
    )(cp, ada_w, ada_b.reshape(depth, 1, n))
    return out[:, :bn]


def _ffn_kernel(x_ref, mod_ref, g_ref, wi_ref, wo_ref, *rest, fc, parts, cast_next):
    if cast_next:
        next_wi_ref, next_wo_ref, o_ref, cast_wi_ref, cast_wo_ref, h_ref = rest
        cast_wi_ref[...] = next_wi_ref[...].astype(BF16)
        cast_wo_ref[...] = next_wo_ref[...].astype(BF16)
    else:
        o_ref, h_ref = rest
    mod = mod_ref[0]
    ff = h_ref.shape[1]
    rows = x_ref.shape[1] // parts

    def hidden(p):
        rs = slice(p * rows, (p + 1) * rows)
        xn = _prenorm(x_ref[0, rs, :], g_ref[...], mod).astype(BF16)
        for c in range(ff // fc):
            gate = _dot(xn, wi_ref[:, c * fc:(c + 1) * fc])
            up = _dot(xn, wi_ref[:, ff + c * fc:ff + (c + 1) * fc])
            h_ref[rs, c * fc:(c + 1) * fc] = (_silu(gate) * up).astype(BF16)

    def project(p):
        rs = slice(p * rows, (p + 1) * rows)
        y = _dot(h_ref[rs, :], wo_ref[...])
        o_ref[0, rs, :] = x_ref[0, rs, :] + (0.5 * (1.0 + mod[2:3])) * y

    hidden(0)
    for p in range(1, parts):
        hidden(p)
        project(p - 1)
    project(parts - 1)


def _row_blocks(rows, steps):
    k = steps
    while rows % k or (rows // k) % (2 * SUBLANES):
        k //= 2
    return k


def _ffn(x, mod, g, wi, wo, stacked_wi, stacked_wo, next_which):
    bn, s, d = x.shape
    ff = wo.shape[0]
    tm, fc = FFN_TM, FFN_FC
    nt = s // tm
    tok = pl.BlockSpec((1, tm, d), lambda b, i: (b, i, 0))
    in_specs = [tok, pl.BlockSpec((1, 3, d), lambda b, i: (b, 0, 0)), _resident((1, d)),
                _resident((d, 2 * ff)), _resident((ff, d))]
    out_specs, out_shape, args = tok, jax.ShapeDtypeStruct(x.shape, x.dtype), (x, mod, g, wi, wo)
    if next_which is not None:
        steps = bn * nt
        wi_steps, wo_steps = _row_blocks(d, steps), _row_blocks(ff, steps)
        assert wi_steps == steps
        layer, member = next_which
        in_specs += [
            pl.BlockSpec((None, None, d // steps, 2 * ff), lambda b, i: (layer, member, b * nt + i, 0)),
            pl.BlockSpec((None, None, ff // wo_steps, d),
                         lambda b, i: (layer, member, jnp.minimum(b * nt + i, wo_steps - 1), 0)),
        ]
        out_specs = [tok, pl.BlockSpec((d // steps, 2 * ff), lambda b, i: (b * nt + i, 0)),
                     pl.BlockSpec((ff // wo_steps, d), lambda b, i: (jnp.minimum(b * nt + i, wo_steps - 1), 0))]
        out_shape = [out_shape, jax.ShapeDtypeStruct((d, 2 * ff), BF16), jax.ShapeDtypeStruct((ff, d), BF16)]
        args += (stacked_wi, stacked_wo)
    out = pl.pallas_call(
        functools.partial(_ffn_kernel, fc=fc, parts=FFN_PARTS, cast_next=next_which is not None),
        grid=(bn, nt),
        in_specs=in_specs,
        out_specs=out_specs,
        out_shape=out_shape,
        scratch_shapes=[pltpu.VMEM((tm, ff), BF16)],
        compiler_params=_params("ffn", 2),
        name="ffn",
    )(*args)
    return out if next_which is not None else (out, None, None)


def _block_masks(n, local):
    row = lax.broadcasted_iota(jnp.int32, (n, n), 0)
    col = lax.broadcasted_iota(jnp.int32, (n, n), 1)
    sh = int(math.log2(local))
    masks = {"diag": row == col, "local": ((row >> sh) == (col >> sh)) & (row >= col)}
    w = n // 2
    while w >= 1:
        sh = int(math.log2(w))
        rb, cb = row >> sh, col >> sh
        masks[w] = (rb == cb + 1) & ((cb & 1) == 0)
        w //= 2
    return masks


def _block_scores(q, k, lf, b, masks, local):
    n, hw = q.shape
    if local:
        b3 = b.reshape(n // local, local, hw)
        first = b3[:, 0:1, :]
        qe = (q * jnp.exp(b3 - first).reshape(n, hw)).astype(BF16)
        ke = (k * jnp.exp(first - b3).reshape(n, hw)).astype(BF16)
        att = jnp.where(masks["local"], _dot_nt(qe, ke), 0.0)
    else:
        att = jnp.where(masks["diag"], _dot_nt(q.astype(BF16), k.astype(BF16)), 0.0)
    w = n // 2
    while w >= (local or SUBLANES):
        b3 = b.reshape(n // w, w, hw)
        last = b3[:, w - 1:w, :]
        prev = jnp.concatenate([last[:1], last[:-1]], axis=0)
        qa = jnp.minimum(b3 - prev, 0.0).reshape(n, hw)
        ka = (last - b3).reshape(n, hw)
        sc = _dot_nt((q * jnp.exp(qa)).astype(BF16), (k * jnp.exp(ka)).astype(BF16))
        att = jnp.where(masks[w], sc, att)
        w //= 2
    if local:
        return att
    t = lax.broadcasted_iota(jnp.int32, (n, 1), 0)
    back = [None] + [pltpu.roll(lf, i, 0) for i in range(1, SUBLANES // 2)]
    fwd = [None] + [pltpu.roll(lf, n - i, 0) for i in range(1, SUBLANES // 2)]
    while w >= 1:
        r = t & (w - 1)
        qa = lf
        ke = k
        if w > 1:
            ka = jnp.where(r <= w - 2, fwd[1], 0.0)
            for i in range(1, w):
                qa = qa + jnp.where(r >= i, back[i], 0.0)
            for i in range(2, w):
                ka = ka + jnp.where(r <= w - 1 - i, fwd[i], 0.0)
            ke = k * jnp.exp(ka)
        sc = _dot_nt((q * jnp.exp(qa)).astype(BF16), ke.astype(BF16))
        att = jnp.where(masks[w], sc, att)
        w //= 2
    return att


def _hgrn2_kernel(x_ref, mod_ref, g_ref, win_ref, alb_ref, on_ref, wout_ref, o_ref,
                  st_ref, q_ref, k_ref, lf_ref, b_ref, v_ref, gt_ref, oall_ref, *, idx, sub):
    chunk, d = x_ref.shape[1], x_ref.shape[2]
    hw = HEAD_W

    @pl.when(pl.program_id(1) == 0)
    def _():
        st_ref[...] = jnp.zeros_like(st_ref)

    x = x_ref[0]
    mod = mod_ref[0]
    xn = _prenorm(x, g_ref[...], mod).astype(BF16)

    alb = alb_ref[...]
    e = jnp.exp(alb - jnp.max(alb, axis=0, keepdims=True))
    sm = e / jnp.sum(e, axis=0, keepdims=True)
    lb = jnp.sum(sm[0:idx + 1], axis=0, keepdims=True) - sm[0:1]

    q_ref[...] = _silu(_dot(xn, win_ref[:, 0:d]))
    f = lb + (1.0 - lb) * jax.nn.sigmoid(_dot(xn, win_ref[:, d:2 * d]))
    k_ref[...] = 1.0 - f
    v_ref[...] = _dot(xn, win_ref[:, 2 * d:3 * d])
    gt_ref[...] = _silu(_dot(xn, win_ref[:, 3 * d:4 * d]))

    lf = jnp.log(f)
    lf_ref[...] = lf
    hi = lf.astype(BF16)
    r1 = lf - hi.astype(F32)
    mid = r1.astype(BF16)
    lo = (r1 - mid.astype(F32)).astype(BF16)
    row = lax.broadcasted_iota(jnp.int32, (chunk, chunk), 0)
    col = lax.broadcasted_iota(jnp.int32, (chunk, chunk), 1)
    tri = (row >= col).astype(BF16)
    b_ref[...] = _dot(tri, hi) + _dot(tri, mid) + _dot(tri, lo)

    masks = _block_masks(sub, A_LOCAL)
    nsub = chunk // sub

    def heads(local):
        staged = []
        for h in range(d // hw):
            hs = slice(h * hw, (h + 1) * hw)
            q, k, lf, b = q_ref[:, hs], k_ref[:, hs], lf_ref[:, hs], b_ref[:, hs]
            inter = _dot_nt((q * jnp.exp(b)).astype(BF16), st_ref[h].astype(BF16))
            atts, crosses = [], []
            for i in range(nsub):
                rs = slice(i * sub, (i + 1) * sub)
                atts.append(_block_scores(q[rs], k[rs], lf[rs], b[rs], masks, local).astype(BF16))
                if i > 0:
                    edge = b[i * sub - 1:i * sub, :]
                    qc = (q[rs] * jnp.exp(b[rs] - edge)).astype(BF16)
                    for j in range(i):
                        js = slice(j * sub, (j + 1) * sub)
                        kc = (k[js] * jnp.exp(edge - b[js])).astype(BF16)
                        crosses.append((i, j, _dot_nt(qc, kc).astype(BF16)))
            staged.append((inter, atts, crosses))
        for h in range(d // hw):
            hs = slice(h * hw, (h + 1) * hw)
            inter, atts, crosses = staged[h]
            k, b = k_ref[:, hs], b_ref[:, hs]
            vb = v_ref[:, hs].astype(BF16)
            outs = [inter[i * sub:(i + 1) * sub] + _dot(atts[i], vb[i * sub:(i + 1) * sub]) for i in range(nsub)]
            for i, j, cs in crosses:
                outs[i] = outs[i] + _dot(cs, vb[j * sub:(j + 1) * sub])
            o = jnp.concatenate(outs, axis=0)

            bl = b[chunk - 1:chunk, :]
            kt = k * jnp.exp(bl - b)
            st_ref[h] = st_ref[h] * jnp.exp(bl) + _dot_tn(vb, kt.astype(BF16))

            ms = jnp.mean(o * o, axis=-1, keepdims=True)
            on = o * lax.rsqrt(ms + EPS) * on_ref[...]
            oall_ref[:, hs] = (on * gt_ref[:, hs]).astype(BF16)

    b3 = b_ref[...].reshape(chunk // A_LOCAL, A_LOCAL, d)
    safe = jnp.max(b3[:, 0:1, :] - b3[:, A_LOCAL - 1:A_LOCAL, :]) <= A_SAFE

    @pl.when(safe)
    def _():
        heads(A_LOCAL)

    @pl.when(jnp.logical_not(safe))
    def _():
        heads(None)

    y = _dot(oall_ref[...], wout_ref[...])
    o_ref[0] = x + (1.0 + mod[2:3]) * y


def _hgrn2(x, mod, g, w_in, a_lb, onorm, w_out, idx):
    bn, s, d = x.shape
    chunk = A_CHUNK
    nh = d // HEAD_W
    act = pltpu.VMEM((chunk, d), F32)
    return pl.pallas_call(
        functools.partial(_hgrn2_kernel, idx=idx, sub=A_SUB),
        grid=(bn, s // chunk),
        in_specs=[
            pl.BlockSpec((1, chunk, d), lambda b, i: (b, i, 0)),
            pl.BlockSpec((1, 3, d), lambda b, i: (b, 0, 0)),
            _resident((1, d)),
            _resident_at((d, 4 * d), (idx,)),
            _resident(a_lb.shape),
            _resident((1, HEAD_W)),
            _resident_at((d, d), (idx,)),
        ],
        out_specs=pl.BlockSpec((1, chunk, d), lambda b, i: (b, i, 0)),
        out_shape=jax.ShapeDtypeStruct(x.shape, x.dtype),
        scratch_shapes=[pltpu.VMEM((nh, HEAD_W, HEAD_W), F32), act, act, act, act, act, act,
                        pltpu.VMEM((chunk, d), BF16)],
        compiler_params=_params("hgrn2", 2),
        name="hgrn2",
    )(x, mod, g, w_in, a_lb, onorm, w_out)


def _conv_kernel(x_ref, mod_ref, g_ref, win_ref, cw_ref, wout_ref, o_ref, carry_ref):
    tm, d = x_ref.shape[1], x_ref.shape[2]

    @pl.when(pl.program_id(1) == 0)
    def _():
        carry_ref[...] = jnp.zeros_like(carry_ref)

    x = x_ref[0]
    mod = mod_ref[0]
    xn = _prenorm(x, g_ref[...], mod).astype(BF16)
    u = _dot(xn, win_ref[:, d:2 * d]) * _dot(xn, win_ref[:, 2 * d:3 * d])
    carry = carry_ref[...]
    prev1, prev2 = carry[SUBLANES - 1:SUBLANES], carry[SUBLANES - 2:SUBLANES - 1]
    row = lax.broadcasted_iota(jnp.int32, (tm, 1), 0)
    u1 = jnp.where(row == 0, prev1, pltpu.roll(u, 1, 0))
    u2 = jnp.where(row == 0, prev2, jnp.where(row == 1, prev1, pltpu.roll(u, 2, 0)))
    cw = cw_ref[...]
    y = cw[0:1] * u2 + cw[1:2] * u1 + cw[2:3] * u
    carry_ref[...] = u[tm - SUBLANES:tm]
    z = (_dot(xn, win_ref[:, 0:d]) * y).astype(BF16)
    o_ref[0] = x + (1.0 + mod[2:3]) * _dot(z, wout_ref[...])


def _short_conv(x, mod, g, w_in, conv_w, w_out):
    bn, s, d = x.shape
    tm = MIX_TM
    return pl.pallas_call(
        _conv_kernel,
        grid=(bn, s // tm),
        in_specs=[
            pl.BlockSpec((1, tm, d), lambda b, i: (b, i, 0)),
            pl.BlockSpec((1, 3, d), lambda b, i: (b, 0, 0)),
            _resident((1, d)),
            _resident((d, 3 * d)),
            _resident(conv_w.shape),
            _resident((d, d)),
        ],
        out_specs=pl.BlockSpec((1, tm, d), lambda b, i: (b, i, 0)),
        out_shape=jax.ShapeDtypeStruct(x.shape, x.dtype),
        scratch_shapes=[pltpu.VMEM((SUBLANES, d), F32)],
        compiler_params=_params("short_conv", 2),
        name="short_conv",
    )(x, mod, g, w_in, conv_w, w_out)


def _qkv_kernel(x_ref, mod_ref, g_ref, win_ref, gq_ref, gk_ref, bd_ref, pos_ref, invf_ref,
                q_ref, k_ref, v_ref):
    d = x_ref.shape[2]
    hw = HEAD_W
    xn = _prenorm(x_ref[0], g_ref[...], mod_ref[0]).astype(BF16)

    ang = pos_ref[0].astype(F32) * invf_ref[...]
    lane = lax.broadcasted_iota(jnp.int32, (1, hw), 1) & (B_DH - 1)
    cs, sn = jnp.cos(ang), jnp.sin(ang)
    ca = jnp.where(lane < B_ROT, cs, 1.0)
    cm = jnp.where(lane < B_ROT // 2, -sn, 0.0)
    cp = jnp.where((lane >= B_ROT // 2) & (lane < B_ROT), sn, 0.0)
    half = B_ROT // 2

    gw = 2 * hw
    for off, gain_ref, is_q in ((0, gq_ref, True), (d, gk_ref, False)):
        for j in range(d // gw):
            z = _dot(xn, win_ref[:, off + j * gw:off + (j + 1) * gw])
            ss = _dot((z * z).astype(BF16), bd_ref[...])
            zn = z * lax.rsqrt(ss * (1.0 / B_DH) + EPS) * gain_ref[:, j * gw:(j + 1) * gw]
            for t in range(gw // hw):
                zz = zn[:, t * hw:(t + 1) * hw]
                r = zz * ca + pltpu.roll(zz, hw - half, 1) * cm + pltpu.roll(zz, half, 1) * cp
                head = j * (gw // hw) + t
                if is_q:
                    q_ref[0, head] = (r * B_DH ** -0.5).astype(BF16)
                else:
                    k_ref[0, head, 0] = r.T.astype(BF16)
    v = _dot(xn, win_ref[:, 2 * d:3 * d]).astype(BF16)
    for head in range(d // hw):
        v_ref[0, head] = v[:, head * hw:(head + 1) * hw]


def _qkv(x, mod, g, w_in, qk_g, positions):
    bn, s, d = x.shape
    tm = MIX_TM
    reps = d // B_DH
    gq = jnp.tile(qk_g[0].astype(F32), reps)[None, :]
    gk = jnp.tile(qk_g[1].astype(F32), reps)[None, :]
    blk = jnp.arange(2 * HEAD_W) // B_DH
    bd = (blk[:, None] == blk[None, :]).astype(BF16)
    inv_freq = ROPE_THETA ** (-jnp.arange(0, B_ROT, 2, dtype=F32) / B_ROT)
    lane = jnp.arange(HEAD_W) % B_DH
    invf = jnp.where(lane < B_ROT, inv_freq[lane % (B_ROT // 2)], 0.0)[None, :]
    tok = pl.BlockSpec((1, tm, d), lambda b, i: (b, i, 0))
    nh = d // HEAD_W
    hm_spec = pl.BlockSpec((1, nh, tm, HEAD_W), lambda b, i: (b, 0, i, 0))
    hm_out = jax.ShapeDtypeStruct((bn, nh, s, HEAD_W), BF16)
    kt_spec = pl.BlockSpec((1, nh, 1, HEAD_W, tm), lambda b, i: (b, 0, i, 0, 0))
    kt_out = jax.ShapeDtypeStruct((bn, nh, s // tm, HEAD_W, tm), BF16)
    return pl.pallas_call(
        _qkv_kernel,
        grid=(bn, s // tm),
        in_specs=[
            tok,
            pl.BlockSpec((1, 3, d), lambda b, i: (b, 0, 0)),
            _resident((1, d)),
            _resident((d, 3 * d)),
            _resident((1, d)),
            _resident((1, d)),
            _resident((2 * HEAD_W, 2 * HEAD_W)),
            pl.BlockSpec((1, tm, 1), lambda b, i: (b, i, 0)),
            _resident((1, HEAD_W)),
        ],
        out_specs=[hm_spec, kt_spec, hm_spec],
        out_shape=[hm_out, kt_out, hm_out],
        compiler_params=_params("attn_qkv", 2),
        name="attn_qkv",
    )(x, mod, g, w_in, gq, gk, bd, positions.reshape(bn, s, 1), invf)


def _attn_kernel(q_ref, kt_ref, v_ref, lam_ref, sg_ref, o_ref, qs_ref, s_ref, m_ref, acc_ref, *,
                 lambda_init, score_rows, full_rows, causal_rows):
    t = q_ref.shape[2]
    hw = HEAD_W
    i = pl.program_id(2)
    q = q_ref[0, 0]
    lane = lax.broadcasted_iota(jnp.int32, (1, hw), 1)
    qs_ref[0] = jnp.where(lane < B_DH, q, jnp.zeros_like(q))
    qs_ref[1] = jnp.where(lane >= B_DH, q, jnp.zeros_like(q))
    ones = jnp.ones((t, hw), BF16)

    def scores_to(slot, j):
        per = t // kt_ref.shape[-1]
        kt = jnp.concatenate([kt_ref[0, 0, j * per + a] for a in range(per)], axis=1)
        for c, rc in [(c, rc) for rc in range(t // score_rows) for c in range(2)]:
            rs = slice(rc * score_rows, (rc + 1) * score_rows)
            s_ref[slot, c, rs, :] = _dot(qs_ref[c, rs, :], kt)

    def softmax_pv(slot, j, causal):
        rows = causal_rows if causal else full_rows
        for c, rc in [(c, rc) for rc in range(t // rows) for c in range(2)]:
            rs = slice(rc * rows, (rc + 1) * rows)
            nk = hw * pl.cdiv((rc + 1) * rows, hw) if causal else t
            sc = s_ref[slot, c, rs, :nk]
            if causal:
                r = rc * rows + lax.broadcasted_iota(jnp.int32, (rows, nk), 0)
                cc = lax.broadcasted_iota(jnp.int32, (rows, nk), 1)
                sc = jnp.where(r >= cc, sc, -jnp.inf)
            cols = [sc[:, a * hw:(a + 1) * hw] for a in range(nk // hw)]
            rmax = jnp.max(functools.reduce(jnp.maximum, cols), axis=-1, keepdims=True)
            m_old = m_ref[c, rs, :]
            m_new = jnp.maximum(m_old, rmax)
            p = jnp.concatenate([jnp.exp(cl - m_new) for cl in cols], axis=1).astype(BF16)
            vx = jnp.concatenate([v_ref[0, 0, pl.ds(j * t, nk), :], ones[:nk]], axis=1)
            alpha = jnp.exp(m_old - m_new)
            acc_ref[c, rs, :] = acc_ref[c, rs, :] * jnp.concatenate([alpha, alpha], axis=1) + _dot(p, vx)
            m_ref[c, rs, :] = m_new

    scores_to(0, 0)
    m_ref[...] = jnp.full_like(m_ref, -jnp.inf)
    acc_ref[...] = jnp.zeros_like(acc_ref)

    def pair(p, carry):
        scores_to(1, 2 * p + 1)
        softmax_pv(0, 2 * p, False)
        scores_to(0, 2 * p + 2)
        softmax_pv(1, 2 * p + 1, False)
        return carry

    lax.fori_loop(0, i // 2, pair, 0)

    @pl.when(i % 2 == 1)
    def _():
        scores_to(1, i)
        softmax_pv(0, i - 1, False)

    softmax_pv(i % 2, i, True)

    lp = lam_ref[...]
    lam = (jnp.exp(jnp.sum(lp[0:1] * lp[1:2], axis=-1, keepdims=True))
           - jnp.exp(jnp.sum(lp[2:3] * lp[3:4], axis=-1, keepdims=True)) + lambda_init)
    a0, a1 = acc_ref[0], acc_ref[1]
    o = a0[:, :hw] / a0[:, hw:] - lam * (a1[:, :hw] / a1[:, hw:])
    ms = jnp.mean(o * o, axis=-1, keepdims=True)
    o_ref[0, 0] = (o * lax.rsqrt(ms + EPS) * sg_ref[...] * (1.0 - lambda_init)).astype(o_ref.dtype)


def _attention(q, kt, v, lam_p, subln, lambda_init):
    bn, nh, s, _ = q.shape
    t = ATT_T
    kb = kt.shape[-1]
    assert kt.shape == (bn, nh, s // kb, HEAD_W, kb) and t % kb == 0
    tile = pl.BlockSpec((1, 1, t, HEAD_W), lambda b, h, i: (b, h, i, 0))
    return pl.pallas_call(
        functools.partial(_attn_kernel, lambda_init=lambda_init, score_rows=ATT_SCORE_R, full_rows=ATT_R,
                          causal_rows=ATT_CAUSAL_R),
        grid=(bn, nh, s // t),
        in_specs=[
            tile,
            pl.BlockSpec((1, 1, s // kb, HEAD_W, kb), lambda b, h, i: (b, h, 0, 0, 0)),
            pl.BlockSpec((1, 1, s, HEAD_W), lambda b, h, i: (b, h, 0, 0)),
            _resident(lam_p.shape),
            _resident((1, HEAD_W)),
        ],
        out_specs=tile,
        out_shape=jax.ShapeDtypeStruct((bn, nh, s, HEAD_W), BF16),
        scratch_shapes=[pltpu.VMEM((2, t, HEAD_W), BF16), pltpu.VMEM((2, 2, t, t), F32),
                        pltpu.VMEM((2, t, HEAD_W), F32),
                        pltpu.VMEM((2, t, 2 * HEAD_W), F32)],
        compiler_params=_params("attn_core", 3),
        name="attn_core",
    )(q, kt, v, lam_p, subln)


def _outproj_kernel(a_ref, x_ref, mod_ref, w_ref, o_ref):
    a = jnp.concatenate([a_ref[0, h] for h in range(a_ref.shape[1])], axis=1)
    o_ref[0] = x_ref[0] + (1.0 + mod_ref[0][2:3]) * _dot(a, w_ref[...])


def _outproj(a, x, mod, w_out):
    bn, s, d = x.shape
    tm = MIX_TM
    tok = pl.BlockSpec((1, tm, d), lambda b, i: (b, i, 0))
    heads = pl.BlockSpec((1, d // HEAD_W, tm, HEAD_W), lambda b, i: (b, 0, i, 0))
    return pl.pallas_call(
        _outproj_kernel,
        grid=(bn, s // tm),
        in_specs=[heads, tok, pl.BlockSpec((1, 3, d), lambda b, i: (b, 0, 0)), _resident((d, d))],
        out_specs=tok,
        out_shape=jax.ShapeDtypeStruct(x.shape, x.dtype),
        compiler_params=_params("attn_out", 2),
        name="attn_out",
    )(a, x, mod, w_out)


def kernel(x, c, positions, ada_w, ada_b, norm_g, ffn_wi, ffn_wo, a_w_in, a_w_out, a_lb, a_onorm,
           b_w_in, b_w_out, b_qk_g, b_lam, b_subln, c_w_in, c_conv, c_w_out):
    bn, s, d = x.shape
    depth = ada_w.shape[0]
    mod = _adaln(c, ada_w, ada_b).reshape(depth, bn, N_SUB, 3, d)
    wi, wo = ffn_wi[0, 0].astype(BF16), ffn_wo[0, 0].astype(BF16)
    a_in, a_out = a_w_in.astype(BF16), a_w_out.astype(BF16)

    for l in range(depth):
        def sub(j):
            return mod[l, :, j], norm_g[l, j][None, :]

        m, g = sub(0)
        x, wi, wo = _ffn(x, m, g, wi, wo, ffn_wi, ffn_wo, (l, 1))
        m, g = sub(1)
        kind, idx = l % N_MIXERS, l // N_MIXERS
        if kind == 0:
            x = _hgrn2(x, m, g, a_in, a_lb.astype(F32), a_onorm[idx][None, :].astype(F32), a_out, idx)
        elif kind == 1:
            lambda_init = 0.8 - 0.6 * math.exp(-0.3 * l)
            q, k, v = _qkv(x, m, g, b_w_in[idx].astype(BF16), b_qk_g[idx], positions)
            a = _attention(q, k, v, b_lam[idx].astype(F32), b_subln[idx][None, :].astype(F32), lambda_init)
            x = _outproj(a, x, m, b_w_out[idx].astype(BF16))
        else:
            x = _short_conv(x, m, g, c_w_in[idx].astype(BF16), c_conv[idx].astype(F32), c_w_out[idx].astype(BF16))
        m, g = sub(2)
        x, wi, wo = _ffn(x, m, g, wi, wo, ffn_wi, ffn_wo, (l + 1, 0) if l + 1 < depth else None)
    return x
```

```python
import functools
import math

import jax
import jax.numpy as jnp
from jax import lax
from jax.experimental import pallas as pl
from jax.experimental.pallas import tpu as pltpu

F32 = jnp.float32
BF16 = jnp.bfloat16

EPS = 1e-6
N_SUB = 3
N_MIXERS = 3
HEAD_W = 128
SUBLANES = 8
B_DH = 64
B_ROT = 16
ROPE_THETA = 500000.0
MIB = 1024 * 1024

VMEM_MIB = {"adaln": 40, "ffn": 56, "hgrn2": 48, "short_conv": 56, "attn_qkv": 48, "attn_core": 56,
            "attn_out": 32}
ADALN_COL_TILES = 6

FFN_TM = 1024
FFN_PARTS = 8
FFN_FC = 256
MIX_TM = 1024
A_CHUNK = 256
A_SUB = 128
A_LOCAL = 32
A_SAFE = 60.0
ATT_T = 1024
ATT_SCORE_R = 1024
ATT_R = 512
ATT_CAUSAL_R = 128


def _silu(z):
    return z * jax.nn.sigmoid(z)


def _dot(a, b):
    return jnp.dot(a, b, preferred_element_type=F32)


def _dot_nt(a, b):
    return lax.dot_general(a, b, (((1,), (1,)), ((), ())), preferred_element_type=F32)


def _dot_tn(a, b):
    return lax.dot_general(a, b, (((0,), (0,)), ((), ())), preferred_element_type=F32)


def _prenorm(x, g, mod):
    ms = jnp.mean(x * x, axis=-1, keepdims=True)
    y = x * lax.rsqrt(ms + EPS) * g
    return y * (1.0 + mod[1:2]) + mod[0:1]


def _resident(shape):
    nd = len(shape)
    return pl.BlockSpec(shape, lambda *_: (0,) * nd, pipeline_mode=pl.Buffered(1))


def _params(name, grid_rank):
    return pltpu.CompilerParams(dimension_semantics=("arbitrary",) * grid_rank,
                                vmem_limit_bytes=VMEM_MIB[name] * MIB)


def _adaln_kernel(c_ref, w_ref, b_ref, o_ref):
    ca = _silu(c_ref[...]).astype(BF16)
    o_ref[0] = _dot(ca, w_ref[0].astype(BF16)) + b_ref[0]


def _adaln(c, ada_w, ada_b):
    depth, d, n = ada_w.shape
    bn = c.shape[0]
    rows = SUBLANES * pl.cdiv(bn, SUBLANES)
    cp = jnp.zeros((rows, d), F32).at[:bn].set(c)
    tn = n // ADALN_COL_TILES
    out = pl.pallas_call(
        _adaln_kernel,
        grid=(depth, n // tn),
        in_specs=[
            pl.BlockSpec((rows, d), lambda l, j: (0, 0)),
            pl.BlockSpec((1, d, tn), lambda l, j: (l, 0, j)),
            pl.BlockSpec((1, 1, tn), lambda l, j: (l, 0, j)),
        ],
        out_specs=pl.BlockSpec((1, rows, tn), lambda l, j: (l, 0, j)),
        out_shape=jax.ShapeDtypeStruct((depth, rows, n), F32),
        compiler_params=_params("adaln", 2),
        name="adaln",
    )(cp, ada_w, ada_b.reshape(depth, 1, n))
    return out[:, :bn]


def _ffn_kernel(x_ref, mod_ref, g_ref, wi_ref, wo_ref, *rest, fc, parts, n_side):
    side_in, o_ref, side_out, h_ref = rest[:n_side], rest[n_side], rest[n_side + 1:-1], rest[-1]
    for src_ref, dst_ref in zip(side_in, side_out):
        dst_ref[...] = src_ref[...].astype(BF16)
    mod = mod_ref[0]
    ff = h_ref.shape[1]
    rows = x_ref.shape[1] // parts

    def hidden(p):
        rs = slice(p * rows, (p + 1) * rows)
        xn = _prenorm(x_ref[0, rs, :], g_ref[...], mod).astype(BF16)
        for c in range(ff // fc):
            gate = _dot(xn, wi_ref[:, c * fc:(c + 1) * fc])
            up = _dot(xn, wi_ref[:, ff + c * fc:ff + (c + 1) * fc])
            h_ref[rs, c * fc:(c + 1) * fc] = (_silu(gate) * up).astype(BF16)

    def project(p):
        rs = slice(p * rows, (p + 1) * rows)
        y = _dot(h_ref[rs, :], wo_ref[...])
        o_ref[0, rs, :] = x_ref[0, rs, :] + (0.5 * (1.0 + mod[2:3])) * y

    hidden(0)
    for p in range(1, parts):
        hidden(p)
        project(p - 1)
    project(parts - 1)


def _row_blocks(rows, steps):
    k = steps
    while rows % k or (rows // k) % (2 * SUBLANES):
        k //= 2
    return k


def _ffn(x, mod, g, wi, wo, side):
    bn, s, d = x.shape
    ff = wo.shape[0]
    tm, fc = FFN_TM, FFN_FC
    nt = s // tm
    steps = bn * nt
    tok = pl.BlockSpec((1, tm, d), lambda b, i: (b, i, 0))
    in_specs = [tok, pl.BlockSpec((1, 3, d), lambda b, i: (b, 0, 0)), _resident((1, d)),
                _resident((d, 2 * ff)), _resident((ff, d))]
    out_specs, out_shape = [tok], [jax.ShapeDtypeStruct(x.shape, x.dtype)]
    for stacked, lead in side:
        rows, cols = stacked.shape[-2:]
        k = _row_blocks(rows, steps)

        def block(b, i, k=k):
            return jnp.minimum(b * nt + i, k - 1)

        in_specs.append(pl.BlockSpec((None,) * len(lead) + (rows // k, cols),
                                     lambda b, i, lead=lead, block=block: tuple(lead) + (block(b, i), 0)))
        out_specs.append(pl.BlockSpec((rows // k, cols), lambda b, i, block=block: (block(b, i), 0)))
        out_shape.append(jax.ShapeDtypeStruct((rows, cols), BF16))
    out = pl.pallas_call(
        functools.partial(_ffn_kernel, fc=fc, parts=FFN_PARTS, n_side=len(side)),
        grid=(bn, nt),
        in_specs=in_specs,
        out_specs=out_specs,
        out_shape=out_shape,
        scratch_shapes=[pltpu.VMEM((tm, ff), BF16)],
        compiler_params=_params("ffn", 2),
        name="ffn",
    )(x, mod, g, wi, wo, *[stacked for stacked, _ in side])
    return out[0], out[1:]


def _block_masks(n, local):
    row = lax.broadcasted_iota(jnp.int32, (n, n), 0)
    col = lax.broadcasted_iota(jnp.int32, (n, n), 1)
    sh = int(math.log2(local))
    masks = {"diag": row == col, "local": ((row >> sh) == (col >> sh)) & (row >= col)}
    w = n // 2
    while w >= 1:
        sh = int(math.log2(w))
        rb, cb = row >> sh, col >> sh
        masks[w] = (rb == cb + 1) & ((cb & 1) == 0)
        w //= 2
    return masks


def _block_scores(q, k, lf, b, masks, local):
    n, hw = q.shape
    if local:
        b3 = b.reshape(n // local, local, hw)
        first = b3[:, 0:1, :]
        qe = (q * jnp.exp(b3 - first).reshape(n, hw)).astype(BF16)
        ke = (k * jnp.exp(first - b3).reshape(n, hw)).astype(BF16)
        att = jnp.where(masks["local"], _dot_nt(qe, ke), 0.0)
    else:
        att = jnp.where(masks["diag"], _dot_nt(q.astype(BF16), k.astype(BF16)), 0.0)
    w = n // 2
    while w >= (local or SUBLANES):
        b3 = b.reshape(n // w, w, hw)
        last = b3[:, w - 1:w, :]
        prev = jnp.concatenate([last[:1], last[:-1]], axis=0)
        qa = jnp.minimum(b3 - prev, 0.0).reshape(n, hw)
        ka = (last - b3).reshape(n, hw)
        sc = _dot_nt((q * jnp.exp(qa)).astype(BF16), (k * jnp.exp(ka)).astype(BF16))
        att = jnp.where(masks[w], sc, att)
        w //= 2
    if local:
        return att
    t = lax.broadcasted_iota(jnp.int32, (n, 1), 0)
    back = [None] + [pltpu.roll(lf, i, 0) for i in range(1, SUBLANES // 2)]
    fwd = [None] + [pltpu.roll(lf, n - i, 0) for i in range(1, SUBLANES // 2)]
    while w >= 1:
        r = t & (w - 1)
        qa = lf
        ke = k
        if w > 1:
            ka = jnp.where(r <= w - 2, fwd[1], 0.0)
            for i in range(1, w):
                qa = qa + jnp.where(r >= i, back[i], 0.0)
            for i in range(2, w):
                ka = ka + jnp.where(r <= w - 1 - i, fwd[i], 0.0)
            ke = k * jnp.exp(ka)
        sc = _dot_nt((q * jnp.exp(qa)).astype(BF16), ke.astype(BF16))
        att = jnp.where(masks[w], sc, att)
        w //= 2
    return att


def _hgrn2_kernel(x_ref, mod_ref, g_ref, win_ref, alb_ref, on_ref, wout_ref, o_ref,
                  st_ref, q_ref, k_ref, lf_ref, b_ref, v_ref, gt_ref, oall_ref, *, idx, sub):
    chunk, d = x_ref.shape[1], x_ref.shape[2]
    hw = HEAD_W

    @pl.when(pl.program_id(1) == 0)
    def _():
        st_ref[...] = jnp.zeros_like(st_ref)

    x = x_ref[0]
    mod = mod_ref[0]
    xn = _prenorm(x, g_ref[...], mod).astype(BF16)

    alb = alb_ref[...]
    e = jnp.exp(alb - jnp.max(alb, axis=0, keepdims=True))
    sm = e / jnp.sum(e, axis=0, keepdims=True)
    lb = jnp.sum(sm[0:idx + 1], axis=0, keepdims=True) - sm[0:1]

    q_ref[...] = _silu(_dot(xn, win_ref[:, 0:d]))
    f = lb + (1.0 - lb) * jax.nn.sigmoid(_dot(xn, win_ref[:, d:2 * d]))
    k_ref[...] = 1.0 - f
    v_ref[...] = _dot(xn, win_ref[:, 2 * d:3 * d])
    gt_ref[...] = _silu(_dot(xn, win_ref[:, 3 * d:4 * d]))

    lf = jnp.log(f)
    lf_ref[...] = lf
    hi = lf.astype(BF16)
    r1 = lf - hi.astype(F32)
    mid = r1.astype(BF16)
    lo = (r1 - mid.astype(F32)).astype(BF16)
    row = lax.broadcasted_iota(jnp.int32, (chunk, chunk), 0)
    col = lax.broadcasted_iota(jnp.int32, (chunk, chunk), 1)
    tri = (row >= col).astype(BF16)
    b_ref[...] = _dot(tri, hi) + _dot(tri, mid) + _dot(tri, lo)

    masks = _block_masks(sub, A_LOCAL)
    nsub = chunk // sub

    def heads(local):
        staged = []
        for h in range(d // hw):
            hs = slice(h * hw, (h + 1) * hw)
            q, k, lf, b = q_ref[:, hs], k_ref[:, hs], lf_ref[:, hs], b_ref[:, hs]
            inter = _dot_nt((q * jnp.exp(b)).astype(BF16), st_ref[h].astype(BF16))
            atts, crosses = [], []
            for i in range(nsub):
                rs = slice(i * sub, (i + 1) * sub)
                atts.append(_block_scores(q[rs], k[rs], lf[rs], b[rs], masks, local).astype(BF16))
                if i > 0:
                    edge = b[i * sub - 1:i * sub, :]
                    qc = (q[rs] * jnp.exp(b[rs] - edge)).astype(BF16)
                    for j in range(i):
                        js = slice(j * sub, (j + 1) * sub)
                        kc = (k[js] * jnp.exp(edge - b[js])).astype(BF16)
                        crosses.append((i, j, _dot_nt(qc, kc).astype(BF16)))
            staged.append((inter, atts, crosses))
        for h in range(d // hw):
            hs = slice(h * hw, (h + 1) * hw)
            inter, atts, crosses = staged[h]
            k, b = k_ref[:, hs], b_ref[:, hs]
            vb = v_ref[:, hs].astype(BF16)
            outs = [inter[i * sub:(i + 1) * sub] + _dot(atts[i], vb[i * sub:(i + 1) * sub]) for i in range(nsub)]
            for i, j, cs in crosses:
                outs[i] = outs[i] + _dot(cs, vb[j * sub:(j + 1) * sub])
            o = jnp.concatenate(outs, axis=0)

            bl = b[chunk - 1:chunk, :]
            kt = k * jnp.exp(bl - b)
            st_ref[h] = st_ref[h] * jnp.exp(bl) + _dot_tn(vb, kt.astype(BF16))

            ms = jnp.mean(o * o, axis=-1, keepdims=True)
            on = o * lax.rsqrt(ms + EPS) * on_ref[...]
            oall_ref[:, hs] = (on * gt_ref[:, hs]).astype(BF16)

    b3 = b_ref[...].reshape(chunk // A_LOCAL, A_LOCAL, d)
    safe = jnp.max(b3[:, 0:1, :] - b3[:, A_LOCAL - 1:A_LOCAL, :]) <= A_SAFE

    @pl.when(safe)
    def _():
        heads(A_LOCAL)

    @pl.when(jnp.logical_not(safe))
    def _():
        heads(None)

    y = _dot(oall_ref[...], wout_ref[...])
    o_ref[0] = x + (1.0 + mod[2:3]) * y


def _hgrn2(x, mod, g, w_in, a_lb, onorm, w_out, idx):
    bn, s, d = x.shape
    chunk = A_CHUNK
    nh = d // HEAD_W
    act = pltpu.VMEM((chunk, d), F32)
    return pl.pallas_call(
        functools.partial(_hgrn2_kernel, idx=idx, sub=A_SUB),
        grid=(bn, s // chunk),
        in_specs=[
            pl.BlockSpec((1, chunk, d), lambda b, i: (b, i, 0)),
            pl.BlockSpec((1, 3, d), lambda b, i: (b, 0, 0)),
            _resident((1, d)),
            _resident((d, 4 * d)),
            _resident(a_lb.shape),
            _resident((1, HEAD_W)),
            _resident((d, d)),
        ],
        out_specs=pl.BlockSpec((1, chunk, d), lambda b, i: (b, i, 0)),
        out_shape=jax.ShapeDtypeStruct(x.shape, x.dtype),
        scratch_shapes=[pltpu.VMEM((nh, HEAD_W, HEAD_W), F32), act, act, act, act, act, act,
                        pltpu.VMEM((chunk, d), BF16)],
        compiler_params=_params("hgrn2", 2),
        name="hgrn2",
    )(x, mod, g, w_in, a_lb, onorm, w_out)


def _conv_kernel(x_ref, mod_ref, g_ref, win_ref, cw_ref, wout_ref, o_ref, carry_ref):
    tm, d = x_ref.shape[1], x_ref.shape[2]

    @pl.when(pl.program_id(1) == 0)
    def _():
        carry_ref[...] = jnp.zeros_like(carry_ref)

    x = x_ref[0]
    mod = mod_ref[0]
    xn = _prenorm(x, g_ref[...], mod).astype(BF16)
    u = _dot(xn, win_ref[:, d:2 * d]) * _dot(xn, win_ref[:, 2 * d:3 * d])
    carry = carry_ref[...]
    prev1, prev2 = carry[SUBLANES - 1:SUBLANES], carry[SUBLANES - 2:SUBLANES - 1]
    row = lax.broadcasted_iota(jnp.int32, (tm, 1), 0)
    u1 = jnp.where(row == 0, prev1, pltpu.roll(u, 1, 0))
    u2 = jnp.where(row == 0, prev2, jnp.where(row == 1, prev1, pltpu.roll(u, 2, 0)))
    cw = cw_ref[...]
    y = cw[0:1] * u2 + cw[1:2] * u1 + cw[2:3] * u
    carry_ref[...] = u[tm - SUBLANES:tm]
    z = (_dot(xn, win_ref[:, 0:d]) * y).astype(BF16)
    o_ref[0] = x + (1.0 + mod[2:3]) * _dot(z, wout_ref[...])


def _short_conv(x, mod, g, w_in, conv_w, w_out):
    bn, s, d = x.shape
    tm = MIX_TM
    return pl.pallas_call(
        _conv_kernel,
        grid=(bn, s // tm),
        in_specs=[
            pl.BlockSpec((1, tm, d), lambda b, i: (b, i, 0)),
            pl.BlockSpec((1, 3, d), lambda b, i: (b, 0, 0)),
            _resident((1, d)),
            _resident((d, 3 * d)),
            _resident(conv_w.shape),
            _resident((d, d)),
        ],
        out_specs=pl.BlockSpec((1, tm, d), lambda b, i: (b, i, 0)),
        out_shape=jax.ShapeDtypeStruct(x.shape, x.dtype),
        scratch_shapes=[pltpu.VMEM((SUBLANES, d), F32)],
        compiler_params=_params("short_conv", 2),
        name="short_conv",
    )(x, mod, g, w_in, conv_w, w_out)


def _qkv_kernel(x_ref, mod_ref, g_ref, win_ref, gq_ref, gk_ref, bd_ref, pos_ref, invf_ref,
                q_ref, k_ref, v_ref):
    d = x_ref.shape[2]
    hw = HEAD_W
    xn = _prenorm(x_ref[0], g_ref[...], mod_ref[0]).astype(BF16)

    ang = pos_ref[0].astype(F32) * invf_ref[...]
    lane = lax.broadcasted_iota(jnp.int32, (1, hw), 1) & (B_DH - 1)
    cs, sn = jnp.cos(ang), jnp.sin(ang)
    ca = jnp.where(lane < B_ROT, cs, 1.0)
    cm = jnp.where(lane < B_ROT // 2, -sn, 0.0)
    cp = jnp.where((lane >= B_ROT // 2) & (lane < B_ROT), sn, 0.0)
    half = B_ROT // 2

    gw = 2 * hw
    for off, gain_ref, is_q in ((0, gq_ref, True), (d, gk_ref, False)):
        for j in range(d // gw):
            z = _dot(xn, win_ref[:, off + j * gw:off + (j + 1) * gw])
            ss = _dot((z * z).astype(BF16), bd_ref[...])
            zn = z * lax.rsqrt(ss * (1.0 / B_DH) + EPS) * gain_ref[:, j * gw:(j + 1) * gw]
            for t in range(gw // hw):
                zz = zn[:, t * hw:(t + 1) * hw]
                r = zz * ca + pltpu.roll(zz, hw - half, 1) * cm + pltpu.roll(zz, half, 1) * cp
                head = j * (gw // hw) + t
                if is_q:
                    q_ref[0, head] = (r * B_DH ** -0.5).astype(BF16)
                else:
                    k_ref[0, head, 0] = r.T.astype(BF16)
    v = _dot(xn, win_ref[:, 2 * d:3 * d]).astype(BF16)
    for head in range(d // hw):
        v_ref[0, head] = v[:, head * hw:(head + 1) * hw]


def _qkv(x, mod, g, w_in, qk_g, positions):
    bn, s, d = x.shape
    tm = MIX_TM
    reps = d // B_DH
    gq = jnp.tile(qk_g[0].astype(F32), reps)[None, :]
    gk = jnp.tile(qk_g[1].astype(F32), reps)[None, :]
    blk = jnp.arange(2 * HEAD_W) // B_DH
    bd = (blk[:, None] == blk[None, :]).astype(BF16)
    inv_freq = ROPE_THETA ** (-jnp.arange(0, B_ROT, 2, dtype=F32) / B_ROT)
    lane = jnp.arange(HEAD_W) % B_DH
    invf = jnp.where(lane < B_ROT, inv_freq[lane % (B_ROT // 2)], 0.0)[None, :]
    tok = pl.BlockSpec((1, tm, d), lambda b, i: (b, i, 0))
    nh = d // HEAD_W
    hm_spec = pl.BlockSpec((1, nh, tm, HEAD_W), lambda b, i: (b, 0, i, 0))
    hm_out = jax.ShapeDtypeStruct((bn, nh, s, HEAD_W), BF16)
    kt_spec = pl.BlockSpec((1, nh, 1, HEAD_W, tm), lambda b, i: (b, 0, i, 0, 0))
    kt_out = jax.ShapeDtypeStruct((bn, nh, s // tm, HEAD_W, tm), BF16)
    return pl.pallas_call(
        _qkv_kernel,
        grid=(bn, s // tm),
        in_specs=[
            tok,
            pl.BlockSpec((1, 3, d), lambda b, i: (b, 0, 0)),
            _resident((1, d)),
            _resident((d, 3 * d)),
            _resident((1, d)),
            _resident((1, d)),
            _resident((2 * HEAD_W, 2 * HEAD_W)),
            pl.BlockSpec((1, tm, 1), lambda b, i: (b, i, 0)),
            _resident((1, HEAD_W)),
        ],
        out_specs=[hm_spec, kt_spec, hm_spec],
        out_shape=[hm_out, kt_out, hm_out],
        compiler_params=_params("attn_qkv", 2),
        name="attn_qkv",
    )(x, mod, g, w_in, gq, gk, bd, positions.reshape(bn, s, 1), invf)


def _attn_kernel(q_ref, kt_ref, v_ref, lam_ref, sg_ref, o_ref, qs_ref, s_ref, m_ref, acc_ref, *,
                 lambda_init, score_rows, full_rows, causal_rows):
    t = q_ref.shape[2]
    hw = HEAD_W
    i = pl.program_id(2)
    q = q_ref[0, 0]
    lane = lax.broadcasted_iota(jnp.int32, (1, hw), 1)
    qs_ref[0] = jnp.where(lane < B_DH, q, jnp.zeros_like(q))
    qs_ref[1] = jnp.where(lane >= B_DH, q, jnp.zeros_like(q))
    ones = jnp.ones((t, hw), BF16)

    def scores_to(slot, j):
        per = t // kt_ref.shape[-1]
        kt = jnp.concatenate([kt_ref[0, 0, j * per + a] for a in range(per)], axis=1)
        for c, rc in [(c, rc) for rc in range(t // score_rows) for c in range(2)]:
            rs = slice(rc * score_rows, (rc + 1) * score_rows)
            s_ref[slot, c, rs, :] = _dot(qs_ref[c, rs, :], kt)

    def softmax_pv(slot, j, causal):
        rows = causal_rows if causal else full_rows
        for c, rc in [(c, rc) for rc in range(t // rows) for c in range(2)]:
            rs = slice(rc * rows, (rc + 1) * rows)
            nk = hw * pl.cdiv((rc + 1) * rows, hw) if causal else t
            sc = s_ref[slot, c, rs, :nk]
            if causal:
                r = rc * rows + lax.broadcasted_iota(jnp.int32, (rows, nk), 0)
                cc = lax.broadcasted_iota(jnp.int32, (rows, nk), 1)
                sc = jnp.where(r >= cc, sc, -jnp.inf)
            cols = [sc[:, a * hw:(a + 1) * hw] for a in range(nk // hw)]
            rmax = jnp.max(functools.reduce(jnp.maximum, cols), axis=-1, keepdims=True)
            m_old = m_ref[c, rs, :]
            m_new = jnp.maximum(m_old, rmax)
            p = jnp.concatenate([jnp.exp(cl - m_new) for cl in cols], axis=1).astype(BF16)
            vx = jnp.concatenate([v_ref[0, 0, pl.ds(j * t, nk), :], ones[:nk]], axis=1)
            alpha = jnp.exp(m_old - m_new)
            acc_ref[c, rs, :] = acc_ref[c, rs, :] * jnp.concatenate([alpha, alpha], axis=1) + _dot(p, vx)
            m_ref[c, rs, :] = m_new

    scores_to(0, 0)
    m_ref[...] = jnp.full_like(m_ref, -jnp.inf)
    acc_ref[...] = jnp.zeros_like(acc_ref)

    def pair(p, carry):
        scores_to(1, 2 * p + 1)
        softmax_pv(0, 2 * p, False)
        scores_to(0, 2 * p + 2)
        softmax_pv(1, 2 * p + 1, False)
        return carry

    lax.fori_loop(0, i // 2, pair, 0)

    @pl.when(i % 2 == 1)
    def _():
        scores_to(1, i)
        softmax_pv(0, i - 1, False)

    softmax_pv(i % 2, i, True)

    lp = lam_ref[...]
    lam = (jnp.exp(jnp.sum(lp[0:1] * lp[1:2], axis=-1, keepdims=True))
           - jnp.exp(jnp.sum(lp[2:3] * lp[3:4], axis=-1, keepdims=True)) + lambda_init)
    a0, a1 = acc_ref[0], acc_ref[1]
    o = a0[:, :hw] / a0[:, hw:] - lam * (a1[:, :hw] / a1[:, hw:])
    ms = jnp.mean(o * o, axis=-1, keepdims=True)
    o_ref[0, 0] = (o * lax.rsqrt(ms + EPS) * sg_ref[...] * (1.0 - lambda_init)).astype(o_ref.dtype)


def _attention(q, kt, v, lam_p, subln, lambda_init):
    bn, nh, s, _ = q.shape
    t = ATT_T
    kb = kt.shape[-1]
    assert kt.shape == (bn, nh, s // kb, HEAD_W, kb) and t % kb == 0
    tile = pl.BlockSpec((1, 1, t, HEAD_W), lambda b, h, i: (b, h, i, 0))
    return pl.pallas_call(
        functools.partial(_attn_kernel, lambda_init=lambda_init, score_rows=ATT_SCORE_R, full_rows=ATT_R,
                          causal_rows=ATT_CAUSAL_R),
        grid=(bn, nh, s // t),
        in_specs=[
            tile,
            pl.BlockSpec((1, 1, s // kb, HEAD_W, kb), lambda b, h, i: (b, h, 0, 0, 0)),
            pl.BlockSpec((1, 1, s, HEAD_W), lambda b, h, i: (b, h, 0, 0)),
            _resident(lam_p.shape),
            _resident((1, HEAD_W)),
        ],
        out_specs=tile,
        out_shape=jax.ShapeDtypeStruct((bn, nh, s, HEAD_W), BF16),
        scratch_shapes=[pltpu.VMEM((2, t, HEAD_W), BF16), pltpu.VMEM((2, 2, t, t), F32),
                        pltpu.VMEM((2, t, HEAD_W), F32),
                        pltpu.VMEM((2, t, 2 * HEAD_W), F32)],
        compiler_params=_params("attn_core", 3),
        name="attn_core",
    )(q, kt, v, lam_p, subln)


def _outproj_kernel(a_ref, x_ref, mod_ref, w_ref, o_ref):
    a = jnp.concatenate([a_ref[0, h] for h in range(a_ref.shape[1])], axis=1)
    o_ref[0] = x_ref[0] + (1.0 + mod_ref[0][2:3]) * _dot(a, w_ref[...])


def _outproj(a, x, mod, w_out):
    bn, s, d = x.shape
    tm = MIX_TM
    tok = pl.BlockSpec((1, tm, d), lambda b, i: (b, i, 0))
    heads = pl.BlockSpec((1, d // HEAD_W, tm, HEAD_W), lambda b, i: (b, 0, i, 0))
    return pl.pallas_call(
        _outproj_kernel,
        grid=(bn, s // tm),
        in_specs=[heads, tok, pl.BlockSpec((1, 3, d), lambda b, i: (b, 0, 0)), _resident((d, d))],
        out_specs=tok,
        out_shape=jax.ShapeDtypeStruct(x.shape, x.dtype),
        compiler_params=_params("attn_out", 2),
        name="attn_out",
    )(a, x, mod, w_out)


def kernel(x, c, positions, ada_w, ada_b, norm_g, ffn_wi, ffn_wo, a_w_in, a_w_out, a_lb, a_onorm,
           b_w_in, b_w_out, b_qk_g, b_lam, b_subln, c_w_in, c_conv, c_w_out):
    bn, s, d = x.shape
    depth = ada_w.shape[0]
    mod = _adaln(c, ada_w, ada_b).reshape(depth, bn, N_SUB, 3, d)
    wi, wo = ffn_wi[0, 0].astype(BF16), ffn_wo[0, 0].astype(BF16)
    mixer_weights = ((a_w_in, a_w_out), (b_w_in, b_w_out), (c_w_in, c_w_out))

    for l in range(depth):
        def sub(j):
            return mod[l, :, j], norm_g[l, j][None, :]

        kind, idx = l % N_MIXERS, l // N_MIXERS
        m, g = sub(0)
        side = [(ffn_wi, (l, 1)), (ffn_wo, (l, 1))] + [(w, (idx,)) for w in mixer_weights[kind]]
        x, (wi, wo, w_in, w_out) = _ffn(x, m, g, wi, wo, side)
        m, g = sub(1)
        if kind == 0:
            x = _hgrn2(x, m, g, w_in, a_lb.astype(F32), a_onorm[idx][None, :].astype(F32), w_out, idx)
        elif kind == 1:
            lambda_init = 0.8 - 0.6 * math.exp(-0.3 * l)
            q, k, v = _qkv(x, m, g, w_in, b_qk_g[idx], positions)
            a = _attention(q, k, v, b_lam[idx].astype(F32), b_subln[idx][None, :].astype(F32), lambda_init)
            x = _outproj(a, x, m, w_out)
        else:
            x = _short_conv(x, m, g, w_in, c_conv[idx].astype(F32), w_out)
        m, g = sub(2)
        side = [(ffn_wi, (l + 1, 0)), (ffn_wo, (l + 1, 0))] if l + 1 < depth else []
        x, nxt = _ffn(x, m, g, wi, wo, side)
        wi, wo = nxt if nxt else (None, None)
    return x
```

```python
import functools
import math

import jax
import jax.numpy as jnp
from jax import lax
from jax.experimental import pallas as pl
from jax.experimental.pallas import tpu as pltpu

F32 = jnp.float32
BF16 = jnp.bfloat16

EPS = 1e-6
N_SUB = 3
N_MIXERS = 3
HEAD_W = 128
SUBLANES = 8
B_DH = 64
B_ROT = 16
ROPE_THETA = 500000.0
MIB = 1024 * 1024

VMEM_MIB = {"adaln": 40, "ffn": 56, "hgrn2": 48, "short_conv": 56, "attn_qkv": 48, "attn_core": 56,
            "attn_out": 32}
ADALN_COL_TILES = 6

FFN_TM = 1024
FFN_PARTS = 8
FFN_FC = 256
MIX_TM = 1024
A_CHUNK = 256
A_SUB = 128
A_LOCAL = 32
A_SAFE = 60.0
ATT_T = 1024
ATT_SCORE_R = 1024
ATT_R = 512
ATT_CAUSAL_R = 128


def _silu(z):
    return z * jax.nn.sigmoid(z)


def _dot(a, b):
    return jnp.dot(a, b, preferred_element_type=F32)


def _dot_nt(a, b):
    return lax.dot_general(a, b, (((1,), (1,)), ((), ())), preferred_element_type=F32)


def _dot_tn(a, b):
    return lax.dot_general(a, b, (((0,), (0,)), ((), ())), preferred_element_type=F32)


def _prenorm(x, g, mod):
    ms = jnp.mean(x * x, axis=-1, keepdims=True)
    y = x * lax.rsqrt(ms + EPS) * g
    return y * (1.0 + mod[1:2]) + mod[0:1]


def _resident(shape):
    nd = len(shape)
    return pl.BlockSpec(shape, lambda *_: (0,) * nd, pipeline_mode=pl.Buffered(1))


def _params(name, grid_rank):
    return pltpu.CompilerParams(dimension_semantics=("arbitrary",) * grid_rank,
                                vmem_limit_bytes=VMEM_MIB[name] * MIB)


def _adaln_kernel(c_ref, w_ref, b_ref, o_ref):
    ca = _silu(c_ref[...]).astype(BF16)
    o_ref[0] = _dot(ca, w_ref[0].astype(BF16)) + b_ref[0]


def _adaln(c, ada_w, ada_b):
    depth, d, n = ada_w.shape
    bn = c.shape[0]
    rows = SUBLANES * pl.cdiv(bn, SUBLANES)
    cp = jnp.zeros((rows, d), F32).at[:bn].set(c)
    tn = n // ADALN_COL_TILES
    out = pl.pallas_call(
        _adaln_kernel,
        grid=(depth, n // tn),
        in_specs=[
            pl.BlockSpec((rows, d), lambda l, j: (0, 0)),
            pl.BlockSpec((1, d, tn), lambda l, j: (l, 0, j)),
            pl.BlockSpec((1, 1, tn), lambda l, j: (l, 0, j)),
        ],
        out_specs=pl.BlockSpec((1, rows, tn), lambda l, j: (l, 0, j)),
        out_shape=jax.ShapeDtypeStruct((depth, rows, n), F32),
        compiler_params=_params("adaln", 2),
        name="adaln",
    )(cp, ada_w, ada_b.reshape(depth, 1, n))
    return out[:, :bn]


def _ffn_kernel(x_ref, mod_ref, g_ref, wi_ref, wo_ref, *rest, fc, parts, n_side):
    side_in, o_ref, side_out, h_ref = rest[:n_side], rest[n_side], rest[n_side + 1:-1], rest[-1]
    for src_ref, dst_ref in zip(side_in, side_out):
        dst_ref[...] = src_ref[...].astype(BF16)
    mod = mod_ref[0]
    ff = h_ref.shape[1]
    rows = x_ref.shape[1] // parts

    def hidden(p):
        rs = slice(p * rows, (p + 1) * rows)
        xn = _prenorm(x_ref[0, rs, :], g_ref[...], mod).astype(BF16)
        for c in range(ff // fc):
            gate = _dot(xn, wi_ref[:, c * fc:(c + 1) * fc])
            up = _dot(xn, wi_ref[:, ff + c * fc:ff + (c + 1) * fc])
            h_ref[rs, c * fc:(c + 1) * fc] = (_silu(gate) * up).astype(BF16)

    def project(p):
        rs = slice(p * rows, (p + 1) * rows)
        y = _dot(h_ref[rs, :], wo_ref[...])
        o_ref[0, rs, :] = x_ref[0, rs, :] + (0.5 * (1.0 + mod[2:3])) * y

    hidden(0)
    for p in range(1, parts):
        hidden(p)
        project(p - 1)
    project(parts - 1)


def _row_blocks(rows, steps):
    k = steps
    while rows % k or (rows // k) % (2 * SUBLANES):
        k //= 2
    return k


def _ffn(x, mod, g, wi, wo, side, in_place):
    bn, s, d = x.shape
    ff = wo.shape[0]
    tm, fc = FFN_TM, FFN_FC
    nt = s // tm
    steps = bn * nt
    tok = pl.BlockSpec((1, tm, d), lambda b, i: (b, i, 0))
    in_specs = [tok, pl.BlockSpec((1, 3, d), lambda b, i: (b, 0, 0)), _resident((1, d)),
                _resident((d, 2 * ff)), _resident((ff, d))]
    out_specs, out_shape = [tok], [jax.ShapeDtypeStruct(x.shape, x.dtype)]
    for stacked, lead in side:
        rows, cols = stacked.shape[-2:]
        k = _row_blocks(rows, steps)

        def block(b, i, k=k):
            return jnp.minimum(b * nt + i, k - 1)

        in_specs.append(pl.BlockSpec((None,) * len(lead) + (rows // k, cols),
                                     lambda b, i, lead=lead, block=block: tuple(lead) + (block(b, i), 0)))
        out_specs.append(pl.BlockSpec((rows // k, cols), lambda b, i, block=block: (block(b, i), 0)))
        out_shape.append(jax.ShapeDtypeStruct((rows, cols), BF16))
    out = pl.pallas_call(
        functools.partial(_ffn_kernel, fc=fc, parts=FFN_PARTS, n_side=len(side)),
        grid=(bn, nt),
        in_specs=in_specs,
        out_specs=out_specs,
        out_shape=out_shape,
        scratch_shapes=[pltpu.VMEM((tm, ff), BF16)],
        input_output_aliases={0: 0} if in_place else {},
        compiler_params=_params("ffn", 2),
        name="ffn",
    )(x, mod, g, wi, wo, *[stacked for stacked, _ in side])
    return out[0], out[1:]


def _block_masks(n, local):
    row = lax.broadcasted_iota(jnp.int32, (n, n), 0)
    col = lax.broadcasted_iota(jnp.int32, (n, n), 1)
    sh = int(math.log2(local))
    masks = {"diag": row == col, "local": ((row >> sh) == (col >> sh)) & (row >= col)}
    w = n // 2
    while w >= 1:
        sh = int(math.log2(w))
        rb, cb = row >> sh, col >> sh
        masks[w] = (rb == cb + 1) & ((cb & 1) == 0)
        w //= 2
    return masks


def _block_scores(q, k, lf, b, masks, local):
    n, hw = q.shape
    if local:
        b3 = b.reshape(n // local, local, hw)
        first = b3[:, 0:1, :]
        qe = (q * jnp.exp(b3 - first).reshape(n, hw)).astype(BF16)
        ke = (k * jnp.exp(first - b3).reshape(n, hw)).astype(BF16)
        att = jnp.where(masks["local"], _dot_nt(qe, ke), 0.0)
    else:
        att = jnp.where(masks["diag"], _dot_nt(q.astype(BF16), k.astype(BF16)), 0.0)
    w = n // 2
    while w >= (local or SUBLANES):
        b3 = b.reshape(n // w, w, hw)
        last = b3[:, w - 1:w, :]
        prev = jnp.concatenate([last[:1], last[:-1]], axis=0)
        qa = jnp.minimum(b3 - prev, 0.0).reshape(n, hw)
        ka = (last - b3).reshape(n, hw)
        sc = _dot_nt((q * jnp.exp(qa)).astype(BF16), (k * jnp.exp(ka)).astype(BF16))
        att = jnp.where(masks[w], sc, att)
        w //= 2
    if local:
        return att
    t = lax.broadcasted_iota(jnp.int32, (n, 1), 0)
    back = [None] + [pltpu.roll(lf, i, 0) for i in range(1, SUBLANES // 2)]
    fwd = [None] + [pltpu.roll(lf, n - i, 0) for i in range(1, SUBLANES // 2)]
    while w >= 1:
        r = t & (w - 1)
        qa = lf
        ke = k
        if w > 1:
            ka = jnp.where(r <= w - 2, fwd[1], 0.0)
            for i in range(1, w):
                qa = qa + jnp.where(r >= i, back[i], 0.0)
            for i in range(2, w):
                ka = ka + jnp.where(r <= w - 1 - i, fwd[i], 0.0)
            ke = k * jnp.exp(ka)
        sc = _dot_nt((q * jnp.exp(qa)).astype(BF16), ke.astype(BF16))
        att = jnp.where(masks[w], sc, att)
        w //= 2
    return att


def _hgrn2_kernel(x_ref, mod_ref, g_ref, win_ref, alb_ref, on_ref, wout_ref, o_ref,
                  st_ref, q_ref, k_ref, lf_ref, b_ref, v_ref, gt_ref, oall_ref, *, idx, sub):
    chunk, d = x_ref.shape[1], x_ref.shape[2]
    hw = HEAD_W

    @pl.when(pl.program_id(1) == 0)
    def _():
        st_ref[...] = jnp.zeros_like(st_ref)

    x = x_ref[0]
    mod = mod_ref[0]
    xn = _prenorm(x, g_ref[...], mod).astype(BF16)

    alb = alb_ref[...]
    e = jnp.exp(alb - jnp.max(alb, axis=0, keepdims=True))
    sm = e / jnp.sum(e, axis=0, keepdims=True)
    lb = jnp.sum(sm[0:idx + 1], axis=0, keepdims=True) - sm[0:1]

    q_ref[...] = _silu(_dot(xn, win_ref[:, 0:d]))
    f = lb + (1.0 - lb) * jax.nn.sigmoid(_dot(xn, win_ref[:, d:2 * d]))
    k_ref[...] = 1.0 - f
    v_ref[...] = _dot(xn, win_ref[:, 2 * d:3 * d])
    gt_ref[...] = _silu(_dot(xn, win_ref[:, 3 * d:4 * d]))

    lf = jnp.log(f)
    lf_ref[...] = lf
    hi = lf.astype(BF16)
    r1 = lf - hi.astype(F32)
    mid = r1.astype(BF16)
    lo = (r1 - mid.astype(F32)).astype(BF16)
    row = lax.broadcasted_iota(jnp.int32, (chunk, chunk), 0)
    col = lax.broadcasted_iota(jnp.int32, (chunk, chunk), 1)
    tri = (row >= col).astype(BF16)
    b_ref[...] = _dot(tri, hi) + _dot(tri, mid) + _dot(tri, lo)

    masks = _block_masks(sub, A_LOCAL)
    nsub = chunk // sub

    def heads(local):
        staged = []
        for h in range(d // hw):
            hs = slice(h * hw, (h + 1) * hw)
            q, k, lf, b = q_ref[:, hs], k_ref[:, hs], lf_ref[:, hs], b_ref[:, hs]
            inter = _dot_nt((q * jnp.exp(b)).astype(BF16), st_ref[h].astype(BF16))
            atts, crosses = [], []
            for i in range(nsub):
                rs = slice(i * sub, (i + 1) * sub)
                atts.append(_block_scores(q[rs], k[rs], lf[rs], b[rs], masks, local).astype(BF16))
                if i > 0:
                    edge = b[i * sub - 1:i * sub, :]
                    qc = (q[rs] * jnp.exp(b[rs] - edge)).astype(BF16)
                    for j in range(i):
                        js = slice(j * sub, (j + 1) * sub)
                        kc = (k[js] * jnp.exp(edge - b[js])).astype(BF16)
                        crosses.append((i, j, _dot_nt(qc, kc).astype(BF16)))
            staged.append((inter, atts, crosses))
        for h in range(d // hw):
            hs = slice(h * hw, (h + 1) * hw)
            inter, atts, crosses = staged[h]
            k, b = k_ref[:, hs], b_ref[:, hs]
            vb = v_ref[:, hs].astype(BF16)
            outs = [inter[i * sub:(i + 1) * sub] + _dot(atts[i], vb[i * sub:(i + 1) * sub]) for i in range(nsub)]
            for i, j, cs in crosses:
                outs[i] = outs[i] + _dot(cs, vb[j * sub:(j + 1) * sub])
            o = jnp.concatenate(outs, axis=0)

            bl = b[chunk - 1:chunk, :]
            kt = k * jnp.exp(bl - b)
            st_ref[h] = st_ref[h] * jnp.exp(bl) + _dot_tn(vb, kt.astype(BF16))

            ms = jnp.mean(o * o, axis=-1, keepdims=True)
            on = o * lax.rsqrt(ms + EPS) * on_ref[...]
            oall_ref[:, hs] = (on * gt_ref[:, hs]).astype(BF16)

    b3 = b_ref[...].reshape(chunk // A_LOCAL, A_LOCAL, d)
    safe = jnp.max(b3[:, 0:1, :] - b3[:, A_LOCAL - 1:A_LOCAL, :]) <= A_SAFE

    @pl.when(safe)
    def _():
        heads(A_LOCAL)

    @pl.when(jnp.logical_not(safe))
    def _():
        heads(None)

    y = _dot(oall_ref[...], wout_ref[...])
    o_ref[0] = x + (1.0 + mod[2:3]) * y


def _hgrn2(x, mod, g, w_in, a_lb, onorm, w_out, idx):
    bn, s, d = x.shape
    chunk = A_CHUNK
    nh = d // HEAD_W
    act = pltpu.VMEM((chunk, d), F32)
    return pl.pallas_call(
        functools.partial(_hgrn2_kernel, idx=idx, sub=A_SUB),
        grid=(bn, s // chunk),
        in_specs=[
            pl.BlockSpec((1, chunk, d), lambda b, i: (b, i, 0)),
            pl.BlockSpec((1, 3, d), lambda b, i: (b, 0, 0)),
            _resident((1, d)),
            _resident((d, 4 * d)),
            _resident(a_lb.shape),
            _resident((1, HEAD_W)),
            _resident((d, d)),
        ],
        out_specs=pl.BlockSpec((1, chunk, d), lambda b, i: (b, i, 0)),
        out_shape=jax.ShapeDtypeStruct(x.shape, x.dtype),
        scratch_shapes=[pltpu.VMEM((nh, HEAD_W, HEAD_W), F32), act, act, act, act, act, act,
                        pltpu.VMEM((chunk, d), BF16)],
        compiler_params=_params("hgrn2", 2),
        name="hgrn2",
    )(x, mod, g, w_in, a_lb, onorm, w_out)


def _conv_kernel(x_ref, mod_ref, g_ref, win_ref, cw_ref, wout_ref, o_ref, carry_ref):
    tm, d = x_ref.shape[1], x_ref.shape[2]

    @pl.when(pl.program_id(1) == 0)
    def _():
        carry_ref[...] = jnp.zeros_like(carry_ref)

    x = x_ref[0]
    mod = mod_ref[0]
    xn = _prenorm(x, g_ref[...], mod).astype(BF16)
    u = _dot(xn, win_ref[:, d:2 * d]) * _dot(xn, win_ref[:, 2 * d:3 * d])
    carry = carry_ref[...]
    prev1, prev2 = carry[SUBLANES - 1:SUBLANES], carry[SUBLANES - 2:SUBLANES - 1]
    row = lax.broadcasted_iota(jnp.int32, (tm, 1), 0)
    u1 = jnp.where(row == 0, prev1, pltpu.roll(u, 1, 0))
    u2 = jnp.where(row == 0, prev2, jnp.where(row == 1, prev1, pltpu.roll(u, 2, 0)))
    cw = cw_ref[...]
    y = cw[0:1] * u2 + cw[1:2] * u1 + cw[2:3] * u
    carry_ref[...] = u[tm - SUBLANES:tm]
    z = (_dot(xn, win_ref[:, 0:d]) * y).astype(BF16)
    o_ref[0] = x + (1.0 + mod[2:3]) * _dot(z, wout_ref[...])


def _short_conv(x, mod, g, w_in, conv_w, w_out):
    bn, s, d = x.shape
    tm = MIX_TM
    return pl.pallas_call(
        _conv_kernel,
        grid=(bn, s // tm),
        in_specs=[
            pl.BlockSpec((1, tm, d), lambda b, i: (b, i, 0)),
            pl.BlockSpec((1, 3, d), lambda b, i: (b, 0, 0)),
            _resident((1, d)),
            _resident((d, 3 * d)),
            _resident(conv_w.shape),
            _resident((d, d)),
        ],
        out_specs=pl.BlockSpec((1, tm, d), lambda b, i: (b, i, 0)),
        out_shape=jax.ShapeDtypeStruct(x.shape, x.dtype),
        scratch_shapes=[pltpu.VMEM((SUBLANES, d), F32)],
        compiler_params=_params("short_conv", 2),
        name="short_conv",
    )(x, mod, g, w_in, conv_w, w_out)


def _qkv_kernel(x_ref, mod_ref, g_ref, win_ref, gq_ref, gk_ref, bd_ref, pos_ref, invf_ref,
                q_ref, k_ref, v_ref):
    d = x_ref.shape[2]
    hw = HEAD_W
    xn = _prenorm(x_ref[0], g_ref[...], mod_ref[0]).astype(BF16)

    ang = pos_ref[0].astype(F32) * invf_ref[...]
    lane = lax.broadcasted_iota(jnp.int32, (1, hw), 1) & (B_DH - 1)
    cs, sn = jnp.cos(ang), jnp.sin(ang)
    ca = jnp.where(lane < B_ROT, cs, 1.0)
    cm = jnp.where(lane < B_ROT // 2, -sn, 0.0)
    cp = jnp.where((lane >= B_ROT // 2) & (lane < B_ROT), sn, 0.0)
    half = B_ROT // 2

    gw = 2 * hw
    for off, gain_ref, is_q in ((0, gq_ref, True), (d, gk_ref, False)):
        for j in range(d // gw):
            z = _dot(xn, win_ref[:, off + j * gw:off + (j + 1) * gw])
            ss = _dot((z * z).astype(BF16), bd_ref[...])
            zn = z * lax.rsqrt(ss * (1.0 / B_DH) + EPS) * gain_ref[:, j * gw:(j + 1) * gw]
            for t in range(gw // hw):
                zz = zn[:, t * hw:(t + 1) * hw]
                r = zz * ca + pltpu.roll(zz, hw - half, 1) * cm + pltpu.roll(zz, half, 1) * cp
                head = j * (gw // hw) + t
                if is_q:
                    q_ref[0, head] = (r * B_DH ** -0.5).astype(BF16)
                else:
                    k_ref[0, head, 0] = r.T.astype(BF16)
    v = _dot(xn, win_ref[:, 2 * d:3 * d]).astype(BF16)
    for head in range(d // hw):
        v_ref[0, head] = v[:, head * hw:(head + 1) * hw]


def _qkv(x, mod, g, w_in, qk_g, positions):
    bn, s, d = x.shape
    tm = MIX_TM
    reps = d // B_DH
    gq = jnp.tile(qk_g[0].astype(F32), reps)[None, :]
    gk = jnp.tile(qk_g[1].astype(F32), reps)[None, :]
    blk = jnp.arange(2 * HEAD_W) // B_DH
    bd = (blk[:, None] == blk[None, :]).astype(BF16)
    inv_freq = ROPE_THETA ** (-jnp.arange(0, B_ROT, 2, dtype=F32) / B_ROT)
    lane = jnp.arange(HEAD_W) % B_DH
    invf = jnp.where(lane < B_ROT, inv_freq[lane % (B_ROT // 2)], 0.0)[None, :]
    tok = pl.BlockSpec((1, tm, d), lambda b, i: (b, i, 0))
    nh = d // HEAD_W
    hm_spec = pl.BlockSpec((1, nh, tm, HEAD_W), lambda b, i: (b, 0, i, 0))
    hm_out = jax.ShapeDtypeStruct((bn, nh, s, HEAD_W), BF16)
    kt_spec = pl.BlockSpec((1, nh, 1, HEAD_W, tm), lambda b, i: (b, 0, i, 0, 0))
    kt_out = jax.ShapeDtypeStruct((bn, nh, s // tm, HEAD_W, tm), BF16)
    return pl.pallas_call(
        _qkv_kernel,
        grid=(bn, s // tm),
        in_specs=[
            tok,
            pl.BlockSpec((1, 3, d), lambda b, i: (b, 0, 0)),
            _resident((1, d)),
            _resident((d, 3 * d)),
            _resident((1, d)),
            _resident((1, d)),
            _resident((2 * HEAD_W, 2 * HEAD_W)),
            pl.BlockSpec((1, tm, 1), lambda b, i: (b, i, 0)),
            _resident((1, HEAD_W)),
        ],
        out_specs=[hm_spec, kt_spec, hm_spec],
        out_shape=[hm_out, kt_out, hm_out],
        compiler_params=_params("attn_qkv", 2),
        name="attn_qkv",
    )(x, mod, g, w_in, gq, gk, bd, positions.reshape(bn, s, 1), invf)


def _attn_kernel(q_ref, kt_ref, v_ref, lam_ref, sg_ref, o_ref, qs_ref, s_ref, m_ref, acc_ref, *,
                 lambda_init, score_rows, full_rows, causal_rows):
    t = q_ref.shape[2]
    hw = HEAD_W
    i = pl.program_id(2)
    q = q_ref[0, 0]
    lane = lax.broadcasted_iota(jnp.int32, (1, hw), 1)
    qs_ref[0] = jnp.where(lane < B_DH, q, jnp.zeros_like(q))
    qs_ref[1] = jnp.where(lane >= B_DH, q, jnp.zeros_like(q))
    ones = jnp.ones((t, hw), BF16)

    def scores_to(slot, j):
        per = t // kt_ref.shape[-1]
        kt = jnp.concatenate([kt_ref[0, 0, j * per + a] for a in range(per)], axis=1)
        for c, rc in [(c, rc) for rc in range(t // score_rows) for c in range(2)]:
            rs = slice(rc * score_rows, (rc + 1) * score_rows)
            s_ref[slot, c, rs, :] = _dot(qs_ref[c, rs, :], kt)

    def softmax_pv(slot, j, causal):
        rows = causal_rows if causal else full_rows
        for c, rc in [(c, rc) for rc in range(t // rows) for c in range(2)]:
            rs = slice(rc * rows, (rc + 1) * rows)
            nk = hw * pl.cdiv((rc + 1) * rows, hw) if causal else t
            sc = s_ref[slot, c, rs, :nk]
            if causal:
                r = rc * rows + lax.broadcasted_iota(jnp.int32, (rows, nk), 0)
                cc = lax.broadcasted_iota(jnp.int32, (rows, nk), 1)
                sc = jnp.where(r >= cc, sc, -jnp.inf)
            cols = [sc[:, a * hw:(a + 1) * hw] for a in range(nk // hw)]
            rmax = jnp.max(functools.reduce(jnp.maximum, cols), axis=-1, keepdims=True)
            m_old = m_ref[c, rs, :]
            m_new = jnp.maximum(m_old, rmax)
            p = jnp.concatenate([jnp.exp(cl - m_new) for cl in cols], axis=1).astype(BF16)
            vx = jnp.concatenate([v_ref[0, 0, pl.ds(j * t, nk), :], ones[:nk]], axis=1)
            alpha = jnp.exp(m_old - m_new)
            acc_ref[c, rs, :] = acc_ref[c, rs, :] * jnp.concatenate([alpha, alpha], axis=1) + _dot(p, vx)
            m_ref[c, rs, :] = m_new

    scores_to(0, 0)
    m_ref[...] = jnp.full_like(m_ref, -jnp.inf)
    acc_ref[...] = jnp.zeros_like(acc_ref)

    def pair(p, carry):
        scores_to(1, 2 * p + 1)
        softmax_pv(0, 2 * p, False)
        scores_to(0, 2 * p + 2)
        softmax_pv(1, 2 * p + 1, False)
        return carry

    lax.fori_loop(0, i // 2, pair, 0)

    @pl.when(i % 2 == 1)
    def _():
        scores_to(1, i)
        softmax_pv(0, i - 1, False)

    softmax_pv(i % 2, i, True)

    lp = lam_ref[...]
    lam = (jnp.exp(jnp.sum(lp[0:1] * lp[1:2], axis=-1, keepdims=True))
           - jnp.exp(jnp.sum(lp[2:3] * lp[3:4], axis=-1, keepdims=True)) + lambda_init)
    a0, a1 = acc_ref[0], acc_ref[1]
    o = a0[:, :hw] / a0[:, hw:] - lam * (a1[:, :hw] / a1[:, hw:])
    ms = jnp.mean(o * o, axis=-1, keepdims=True)
    o_ref[0, 0] = (o * lax.rsqrt(ms + EPS) * sg_ref[...] * (1.0 - lambda_init)).astype(o_ref.dtype)


def _attention(q, kt, v, lam_p, subln, lambda_init):
    bn, nh, s, _ = q.shape
    t = ATT_T
    kb = kt.shape[-1]
    assert kt.shape == (bn, nh, s // kb, HEAD_W, kb) and t % kb == 0
    tile = pl.BlockSpec((1, 1, t, HEAD_W), lambda b, h, i: (b, h, i, 0))
    return pl.pallas_call(
        functools.partial(_attn_kernel, lambda_init=lambda_init, score_rows=ATT_SCORE_R, full_rows=ATT_R,
                          causal_rows=ATT_CAUSAL_R),
        grid=(bn, nh, s // t),
        in_specs=[
            tile,
            pl.BlockSpec((1, 1, s // kb, HEAD_W, kb), lambda b, h, i: (b, h, 0, 0, 0)),
            pl.BlockSpec((1, 1, s, HEAD_W), lambda b, h, i: (b, h, 0, 0)),
            _resident(lam_p.shape),
            _resident((1, HEAD_W)),
        ],
        out_specs=tile,
        out_shape=jax.ShapeDtypeStruct((bn, nh, s, HEAD_W), BF16),
        scratch_shapes=[pltpu.VMEM((2, t, HEAD_W), BF16), pltpu.VMEM((2, 2, t, t), F32),
                        pltpu.VMEM((2, t, HEAD_W), F32),
                        pltpu.VMEM((2, t, 2 * HEAD_W), F32)],
        compiler_params=_params("attn_core", 3),
        name="attn_core",
    )(q, kt, v, lam_p, subln)


def _outproj_kernel(a_ref, x_ref, mod_ref, w_ref, o_ref):
    a = jnp.concatenate([a_ref[0, h] for h in range(a_ref.shape[1])], axis=1)
    o_ref[0] = x_ref[0] + (1.0 + mod_ref[0][2:3]) * _dot(a, w_ref[...])


def _outproj(a, x, mod, w_out):
    bn, s, d = x.shape
    tm = MIX_TM
    tok = pl.BlockSpec((1, tm, d), lambda b, i: (b, i, 0))
    heads = pl.BlockSpec((1, d // HEAD_W, tm, HEAD_W), lambda b, i: (b, 0, i, 0))
    return pl.pallas_call(
        _outproj_kernel,
        grid=(bn, s // tm),
        in_specs=[heads, tok, pl.BlockSpec((1, 3, d), lambda b, i: (b, 0, 0)), _resident((d, d))],
        out_specs=tok,
        out_shape=jax.ShapeDtypeStruct(x.shape, x.dtype),
        compiler_params=_params("attn_out", 2),
        name="attn_out",
    )(a, x, mod, w_out)


def kernel(x, c, positions, ada_w, ada_b, norm_g, ffn_wi, ffn_wo, a_w_in, a_w_out, a_lb, a_onorm,
           b_w_in, b_w_out, b_qk_g, b_lam, b_subln, c_w_in, c_conv, c_w_out):
    bn, s, d = x.shape
    depth = ada_w.shape[0]
    mod = _adaln(c, ada_w, ada_b).reshape(depth, bn, N_SUB, 3, d)
    wi, wo = ffn_wi[0, 0].astype(BF16), ffn_wo[0, 0].astype(BF16)
    mixer_weights = ((a_w_in, a_w_out), (b_w_in, b_w_out), (c_w_in, c_w_out))

    for l in range(depth):
        def sub(j):
            return mod[l, :, j], norm_g[l, j][None, :]

        kind, idx = l % N_MIXERS, l // N_MIXERS
        m, g = sub(0)
        side = [(ffn_wi, (l, 1)), (ffn_wo, (l, 1))] + [(w, (idx,)) for w in mixer_weights[kind]]
        x, (wi, wo, w_in, w_out) = _ffn(x, m, g, wi, wo, side, in_place=l > 0)
        m, g = sub(1)
        if kind == 0:
            x = _hgrn2(x, m, g, w_in, a_lb.astype(F32), a_onorm[idx][None, :].astype(F32), w_out, idx)
        elif kind == 1:
            lambda_init = 0.8 - 0.6 * math.exp(-0.3 * l)
            q, k, v = _qkv(x, m, g, w_in, b_qk_g[idx], positions)
            a = _attention(q, k, v, b_lam[idx].astype(F32), b_subln[idx][None, :].astype(F32), lambda_init)
            x = _outproj(a, x, m, w_out)
        else:
            x = _short_conv(x, m, g, w_in, c_conv[idx].astype(F32), w_out)
        m, g = sub(2)
        side = [(ffn_wi, (l + 1, 0)), (ffn_wo, (l + 1, 0))] if l + 1 < depth else []
        x, nxt = _ffn(x, m, g, wi, wo, side, in_place=True)
        wi, wo = nxt if nxt else (None, None)
    return x
```
